```python
import functools
import jax, jax.numpy as jnp
from jax import lax
import numpy as np

D_MODEL = 1024
BATCH = 8
SEQ = 4096
DEPTH = 2
DEC_BATCH = 128
DEC_SEQ = 4
PAST_LEN = 16384
PAGE_SIZE = 128

MIX_WIDTH = D_MODEL
HEAD_DIM = 64
GLA_HEADS = 4
GLA_DV = (MIX_WIDTH // 2) // GLA_HEADS
GLA_DK = GLA_DV // 2
GLA_QK_W = GLA_HEADS * GLA_DK
GLA_V_W = GLA_HEADS * GLA_DV
GATE_RANK = 16
GATE_NORM = 16.0
GLA_CHUNK = 64
SWA_HEADS = (MIX_WIDTH - GLA_V_W) // HEAD_DIM
SWA_KV_HEADS = 2
SWA_GROUP = SWA_HEADS // SWA_KV_HEADS
SWA_Q_W = SWA_HEADS * HEAD_DIM
SWA_KV_W = SWA_KV_HEADS * HEAD_DIM
WINDOW = 128
ROT_DIM = HEAD_DIM // 4
ROPE_THETA = 500000.0
D_FF = 2816
EPS = 1e-6
IN_SPLITS = (GLA_QK_W, GLA_QK_W, GLA_V_W, GLA_V_W, GATE_RANK, SWA_Q_W, SWA_KV_W, SWA_KV_W)
IN_WIDTH = 2 * GLA_QK_W + 2 * GLA_V_W + GATE_RANK + SWA_Q_W + 2 * SWA_KV_W

kernel_name = 'hymba_gla_swa_macaron_step'


def rmsnorm(x, g):
    xf = x.astype(jnp.float32)
    y = xf * lax.rsqrt(jnp.mean(xf * xf, axis=-1, keepdims=True) + EPS)
    return (y * g.astype(jnp.float32)).astype(x.dtype)


def swiglu(h, w_gu, w_down):
    gate, up = jnp.split(h @ w_gu, 2, axis=-1)
    return (jax.nn.silu(gate) * up) @ w_down


def rope_partial(x, pos):
    half = ROT_DIM // 2
    inv = jnp.power(jnp.float32(ROPE_THETA), -jnp.arange(half, dtype=jnp.float32) * (2.0 / ROT_DIM))
    ang = pos.astype(jnp.float32)[:, None] * inv[None, :]
    cos = jnp.cos(ang)[None, :, None, :].astype(x.dtype)
    sin = jnp.sin(ang)[None, :, None, :].astype(x.dtype)
    x1 = x[..., :half]
    x2 = x[..., half:ROT_DIM]
    return jnp.concatenate([x1 * cos - x2 * sin, x2 * cos + x1 * sin, x[..., ROT_DIM:]], axis=-1)


def gla_chunked(q, k, v, g, s0):
    B, L, H, _ = q.shape
    C = min(GLA_CHUNK, L)
    n = -(-L // C)
    pad = n * C - L

    def blocks(t):
        t = jnp.pad(t.astype(jnp.float32), ((0, 0), (0, pad), (0, 0), (0, 0)))
        return t.reshape(B, n, C, H, t.shape[-1]).transpose(1, 0, 3, 2, 4)

    causal = jnp.tril(jnp.ones((C, C), dtype=bool))

    def step(S, inp):
        qc, kc, vc, gc = inp
        b = jnp.cumsum(gc, axis=-2)
        b_last = b[..., -1:, :]
        qb = qc * jnp.exp(b)
        att = jnp.where(causal, jnp.einsum('bhid,bhjd->bhij', qb, kc * jnp.exp(-b)), 0.0)
        o = jnp.einsum('bhij,bhjv->bhiv', att, vc) + jnp.einsum('bhid,bhdv->bhiv', qb, S)
        S = jnp.exp(b_last[..., 0, :])[..., None] * S + jnp.einsum('bhjd,bhjv->bhdv', kc * jnp.exp(b_last - b), vc)
        return S, o

    S, o = lax.scan(step, s0.astype(jnp.float32), (blocks(q), blocks(k), blocks(v), blocks(g)))
    o = o.transpose(1, 0, 3, 2, 4).reshape(B, n * C, H, -1)[:, :L]
    return o, S


def sink_attention(q, k, v, q_pos, k_pos, sinks):
    B, N, Q = q.shape[:3]
    qg = q.reshape(B, N, Q, SWA_KV_HEADS, SWA_GROUP, HEAD_DIM)
    s = jnp.einsum('bnqhgd,bnkhd->bnhgqk', qg, k, preferred_element_type=jnp.float32) * (HEAD_DIM ** -0.5)
    qp = q_pos[:, :, None]
    kp = k_pos[:, None, :]
    mask = (kp <= qp) & (kp > qp - WINDOW) & (kp >= 0)
    s = jnp.where(mask[None, :, None, None], s, -jnp.inf)
    sink = sinks.astype(jnp.float32).reshape(SWA_KV_HEADS, SWA_GROUP)[:, :, None, None]
    m = jnp.maximum(jnp.max(s, axis=-1, keepdims=True), sink)
    p = jnp.exp(s - m)
    probs = p / (jnp.sum(p, axis=-1, keepdims=True) + jnp.exp(sink - m))
    o = jnp.einsum('bnhgqk,bnkhd->bnqhgd', probs.astype(v.dtype), v)
    return o.reshape(B, N * Q, SWA_HEADS * HEAD_DIM)


def swa_prompt(q, k, v, sinks):
    B, L = q.shape[:2]
    nb = L // WINDOW
    qb = q.reshape(B, nb, WINDOW, SWA_HEADS, HEAD_DIM)

    def ctx(t):
        tp = jnp.pad(t, ((0, 0), (WINDOW, 0), (0, 0), (0, 0))).reshape(B, nb + 1, WINDOW, SWA_KV_HEADS, HEAD_DIM)
        return jnp.concatenate([tp[:, :-1], tp[:, 1:]], axis=2)

    q_pos = jnp.arange(L).reshape(nb, WINDOW)
    k_pos = jnp.concatenate([q_pos - WINDOW, q_pos], axis=1)
    out = sink_attention(qb, ctx(k), ctx(v), q_pos, k_pos, sinks)
    wb = min(WINDOW, L)
    return out, k[:, L - wb:], v[:, L - wb:]


def swa_sample(q, k, v, sinks, *, k_buf, v_buf, pos):
    wb = k_buf.shape[1]
    k_all = jnp.concatenate([k_buf.astype(k.dtype), k], axis=1)
    v_all = jnp.concatenate([v_buf.astype(v.dtype), v], axis=1)
    k_pos = jnp.concatenate([PAST_LEN - wb + jnp.arange(wb), pos])[None]
    out = sink_attention(q[:, None], k_all[:, None], v_all[:, None], pos[None], k_pos, sinks)
    return out, k_all[:, -wb:], v_all[:, -wb:]


def layer(x, pos, gla_s0, swa_fn, p):
    (n1, ffn1_gu, ffn1_down, n_mix, w_in, w_a2, b_a, gla_g, sinks, swa_g, w_out, n2, ffn2_gu, ffn2_down) = p
    x = x + 0.5 * swiglu(rmsnorm(x, n1), ffn1_gu, ffn1_down)
    h = rmsnorm(x, n_mix)
    B, L, _ = h.shape
    offs = np.cumsum(IN_SPLITS)[:-1].tolist()
    qg, kg, vg, rg, a_low, qs, ks, vs = jnp.split(h @ w_in, offs, axis=-1)

    def heads(t, nh):
        return t.reshape(B, L, nh, -1)

    log_a = jax.nn.log_sigmoid((a_low @ w_a2 + b_a).astype(jnp.float32)) / GATE_NORM
    o_g, s_new = gla_chunked(heads(qg, GLA_HEADS) * (GLA_DK ** -0.5), heads(kg, GLA_HEADS),
                             heads(vg, GLA_HEADS), heads(log_a, GLA_HEADS), gla_s0)
    o_g = o_g * lax.rsqrt(jnp.mean(o_g * o_g, axis=-1, keepdims=True) + EPS)
    o_g = (o_g.reshape(B, L, GLA_V_W) * gla_g.astype(jnp.float32)
           * jax.nn.silu(rg.astype(jnp.float32))).astype(x.dtype)
    qs = rope_partial(heads(qs, SWA_HEADS), pos)
    ks = rope_partial(heads(ks, SWA_KV_HEADS), pos)
    vs = heads(vs, SWA_KV_HEADS)
    o_s, k_keep, v_keep = swa_fn(qs, ks, vs, sinks)
    o_s = rmsnorm(o_s, swa_g)
    x = x + jnp.concatenate([o_g, o_s.astype(x.dtype)], axis=-1) @ w_out
    x = x + 0.5 * swiglu(rmsnorm(x, n2), ffn2_gu, ffn2_down)
    return x, s_new.astype(gla_s0.dtype), k_keep, v_keep


def setup_inputs(seed: int = 0) -> dict:
    key = jax.random.key(seed)
    ks = jax.random.split(key, 24)
    f32 = jnp.float32
    win_buf = min(WINDOW, PAST_LEN)

    def nrm(k, shape, scale):
        return jax.random.normal(k, shape, f32) * scale

    def gain(k, shape):
        return 1.0 + 0.02 * jax.random.normal(k, shape, f32)

    return {
        'x_prompt': nrm(ks[0], (BATCH, SEQ, D_MODEL), 1.0),
        'x_sample': nrm(ks[1], (DEC_BATCH, DEC_SEQ, D_MODEL), 1.0),
        'state_gla': nrm(ks[2], (DEPTH, DEC_BATCH, GLA_HEADS, GLA_DK, GLA_DV), 0.5),
        'cache_swa_k': nrm(ks[3], (DEPTH, DEC_BATCH, win_buf, SWA_KV_HEADS, HEAD_DIM), 1.0),
        'cache_swa_v': nrm(ks[4], (DEPTH, DEC_BATCH, win_buf, SWA_KV_HEADS, HEAD_DIM), 1.0),
        'norm_ffn1': gain(ks[5], (DEPTH, D_MODEL)),
        'w_ffn1_gu': nrm(ks[6], (DEPTH, D_MODEL, 2 * D_FF), D_MODEL ** -0.5),
        'w_ffn1_down': nrm(ks[7], (DEPTH, D_FF, D_MODEL), D_FF ** -0.5),
        'norm_mix': gain(ks[8], (DEPTH, D_MODEL)),
        'w_in': nrm(ks[9], (DEPTH, D_MODEL, IN_WIDTH), D_MODEL ** -0.5),
        'w_gate_up': nrm(ks[10], (DEPTH, GATE_RANK, GLA_QK_W), GATE_RANK ** -0.5),
        'b_gate': nrm(ks[11], (DEPTH, GLA_QK_W), 0.1),
        'gla_norm': gain(ks[12], (DEPTH, GLA_V_W)),
        'attn_sinks': nrm(ks[13], (DEPTH, SWA_HEADS), 0.5),
        'swa_norm': gain(ks[14], (DEPTH, SWA_Q_W)),
        'w_out': nrm(ks[15], (DEPTH, MIX_WIDTH, D_MODEL), MIX_WIDTH ** -0.5),
        'norm_ffn2': gain(ks[16], (DEPTH, D_MODEL)),
        'w_ffn2_gu': nrm(ks[17], (DEPTH, D_MODEL, 2 * D_FF), D_MODEL ** -0.5),
        'w_ffn2_down': nrm(ks[18], (DEPTH, D_FF, D_MODEL), D_FF ** -0.5),
        'norm_final': gain(ks[19], (D_MODEL,)),
    }


def reference(x_prompt, x_sample, state_gla, cache_swa_k, cache_swa_v,
              norm_ffn1, w_ffn1_gu, w_ffn1_down, norm_mix, w_in, w_gate_up, b_gate,
              gla_norm, attn_sinks, swa_norm, w_out, norm_ffn2, w_ffn2_gu, w_ffn2_down,
              norm_final):
    def params(l):
        return (norm_ffn1[l], w_ffn1_gu[l], w_ffn1_down[l], norm_mix[l], w_in[l], w_gate_up[l],
                b_gate[l], gla_norm[l], attn_sinks[l], swa_norm[l], w_out[l], norm_ffn2[l],
                w_ffn2_gu[l], w_ffn2_down[l])

    pos_p = jnp.arange(SEQ)
    s0_p = jnp.zeros((BATCH, GLA_HEADS, GLA_DK, GLA_DV), state_gla.dtype)
    xp = x_prompt
    gla_p, k_p, v_p = [], [], []
    for l in range(DEPTH):
        xp, s, kk, vv = layer(xp, pos_p, s0_p, swa_prompt, params(l))
        gla_p.append(s)
        k_p.append(kk)
        v_p.append(vv)

    pos_s = PAST_LEN + jnp.arange(DEC_SEQ)
    xs = x_sample
    gla_s, k_s, v_s = [], [], []
    for l in range(DEPTH):
        swa_fn = functools.partial(swa_sample, k_buf=cache_swa_k[l], v_buf=cache_swa_v[l], pos=pos_s)
        xs, s, kk, vv = layer(xs, pos_s, state_gla[l], swa_fn, params(l))
        gla_s.append(s)
        k_s.append(kk)
        v_s.append(vv)

    y_prompt = rmsnorm(xp, norm_final)
    y_sample = rmsnorm(xs, norm_final)
    return (y_prompt, y_sample, jnp.stack(gla_p), jnp.stack(k_p), jnp.stack(v_p),
            jnp.stack(gla_s), jnp.stack(k_s), jnp.stack(v_s))
```

```python
import functools

import jax
import jax.numpy as jnp
import numpy as np
from jax import lax
from jax.experimental import pallas as pl
from jax.experimental.pallas import tpu as pltpu

F32 = jnp.float32
BF16 = jnp.bfloat16

D_MODEL = 1024
D_FF = 2816
EPS = 1e-6
GLA_HEADS = 4
GLA_DK = 64
GLA_DV = 128
GLA_QK_W = GLA_HEADS * GLA_DK
GLA_V_W = GLA_HEADS * GLA_DV
GATE_RANK = 16
GATE_NORM = 16.0
GLA_CHUNK = 64
HEAD_DIM = 64
SWA_HEADS = 8
SWA_KV_HEADS = 2
SWA_GROUP = SWA_HEADS // SWA_KV_HEADS
SWA_Q_W = SWA_HEADS * HEAD_DIM
SWA_KV_W = SWA_KV_HEADS * HEAD_DIM
WINDOW = 128
ROT_DIM = 16
ROPE_THETA = 500000.0
PAST_LEN = 16384
QK_SCALE = 0.125

LANES = 128
C_QG, C_KG, C_VG, C_RG = 0, 256, 512, 1024
C_QS, C_KS, C_VS, C_A = 1536, 2048, 2176, 2304
IN_W_PAD = C_A + LANES

FFN_TILE = 512
FFN_CHUNK = 256
MIX_TILE = 512
SMP_SEQS = 32
SMP_SUB = 8
VMEM_LIMIT = 56 * 1024 * 1024

NT_DIMS = (((1,), (1,)), ((), ()))
TN_DIMS = (((0,), (0,)), ((), ()))


def _dot(a, b):
    return jnp.dot(a, b, preferred_element_type=F32)


def _dot_nt(a, b):
    return lax.dot_general(a, b, NT_DIMS, preferred_element_type=F32)


def _dot_tn(a, b):
    return lax.dot_general(a, b, TN_DIMS, preferred_element_type=F32)


def _rmsnorm(x, g):
    return x * lax.rsqrt(jnp.mean(x * x, axis=-1, keepdims=True) + EPS) * g


def _log_sigmoid(z):
    return jnp.minimum(z, 0.0) - jnp.log1p(jnp.exp(-jnp.abs(z)))


def _silu(z):
    return z * jax.nn.sigmoid(z)


def _split_dot(m_bf16, g):
    hi = g.astype(BF16)
    lo = (g - hi.astype(F32)).astype(BF16)
    return _dot(m_bf16, hi) + _dot(m_bf16, lo)


def _rope(x, cos, sa, sb):
    half = ROT_DIM // 2
    outs = []
    for p in range(x.shape[1] // LANES):
        s = x[:, p * LANES:(p + 1) * LANES]
        outs.append(s * cos + pltpu.roll(s, half, 1) * sa + pltpu.roll(s, LANES - half, 1) * sb)
    return outs[0] if len(outs) == 1 else jnp.concatenate(outs, axis=1)


def _ffn_kernel(*refs, final):
    if final:
        x_ref, n_ref, wgu_ref, wd_ref, nf_ref, o_ref, acc_ref = refs
    else:
        x_ref, n_ref, wgu_ref, wd_ref, o_ref, acc_ref = refs
    x = x_ref[...]
    h = _rmsnorm(x, n_ref[...]).astype(BF16)
    for c in range(D_FF // FFN_CHUNK):
        lo = c * FFN_CHUNK
        gate = _dot(h, wgu_ref[:, lo:lo + FFN_CHUNK])
        up = _dot(h, wgu_ref[:, D_FF + lo:D_FF + lo + FFN_CHUNK])
        act = (_silu(gate) * up).astype(BF16)
        part = _dot(act, wd_ref[lo:lo + FFN_CHUNK, :])
        if c == 0:
            acc_ref[...] = part
        else:
            acc_ref[...] += part
    y = x + 0.5 * acc_ref[...]
    if final:
        y = _rmsnorm(y, nf_ref[...])
    o_ref[...] = y


def _ffn(x2d, norm, w_gu, w_down, norm_final=None):
    n_tok = x2d.shape[0]
    final = norm_final is not None
    const = lambda i: (0, 0)
    in_specs = [
        pl.BlockSpec((FFN_TILE, D_MODEL), lambda i: (i, 0)),
        pl.BlockSpec((1, D_MODEL), const),
        pl.BlockSpec((D_MODEL, 2 * D_FF), const, pipeline_mode=pl.Buffered(1)),
        pl.BlockSpec((D_FF, D_MODEL), const, pipeline_mode=pl.Buffered(1)),
    ]
    args = [x2d, norm.reshape(1, D_MODEL), w_gu, w_down]
    if final:
        in_specs.append(pl.BlockSpec((1, D_MODEL), const))
        args.append(norm_final.reshape(1, D_MODEL))
    return pl.pallas_call(
        functools.partial(_ffn_kernel, final=final),
        grid=(n_tok // FFN_TILE,),
        in_specs=in_specs,
        out_specs=pl.BlockSpec((FFN_TILE, D_MODEL), lambda i: (i, 0)),
        out_shape=jax.ShapeDtypeStruct((n_tok, D_MODEL), F32),
        scratch_shapes=[pltpu.VMEM((FFN_TILE, D_MODEL), F32)],
        compiler_params=pltpu.CompilerParams(
            dimension_semantics=("arbitrary",), vmem_limit_bytes=VMEM_LIMIT),
        name="ffn_final" if final else "ffn",
    )(*args)


def _mix_prompt_kernel(sink_ref, x_ref, nmix_ref, win_ref, wa2_ref, ba_ref, glag_ref, swag_ref,
                       wout_ref, cos_ref, sa_ref, sb_ref,
                       xo_ref, so_ref, ko_ref, vo_ref,
                       proj_s, g_s, st_s, qrot_s, kt_s, vt_s, og_s, cat_s, mask_s):
    T = MIX_TILE
    l = pl.program_id(1)
    n_l = pl.num_programs(1)

    @pl.when(l == 0)
    def _():
        st_s[...] = jnp.zeros_like(st_s)
        kt_s[0:WINDOW, :] = jnp.zeros((WINDOW, SWA_KV_W), BF16)
        vt_s[0:WINDOW, :] = jnp.zeros((WINDOW, SWA_KV_W), BF16)
        r = lax.broadcasted_iota(jnp.int32, (SWA_GROUP * WINDOW, 2 * WINDOW), 0) & (WINDOW - 1)
        c = lax.broadcasted_iota(jnp.int32, (SWA_GROUP * WINDOW, 2 * WINDOW), 1)
        margin = jnp.where(c < WINDOW, c - r, r - c + WINDOW + 1)
        mask_s[...] = jnp.where(margin > 0, 1.0, 0.0)

    x = x_ref[...]
    h = _rmsnorm(x, nmix_ref[...]).astype(BF16)
    proj_s[...] = _dot(h, win_ref[...])

    a_low = proj_s[:, C_A:C_A + LANES].astype(BF16)
    z = _dot(a_low, wa2_ref[...]) + ba_ref[...]
    g_s[...] = _log_sigmoid(z) * (1.0 / GATE_NORM)

    cos, sa, sb = cos_ref[...], sa_ref[...], sb_ref[...]
    qrot_s[...] = (_rope(proj_s[:, C_QS:C_QS + SWA_Q_W], cos, sa, sb) * QK_SCALE).astype(BF16)
    k_rot = _rope(proj_s[:, C_KS:C_KS + SWA_KV_W], cos, sa, sb)
    kt_s[WINDOW:WINDOW + T, :] = k_rot.astype(BF16)
    vt_s[WINDOW:WINDOW + T, :] = proj_s[:, C_VS:C_VS + SWA_KV_W].astype(BF16)
    ko_ref[...] = k_rot[T - WINDOW:T, :]
    vo_ref[...] = proj_s[T - WINDOW:T, C_VS:C_VS + SWA_KV_W]

    C = GLA_CHUNK
    ri = lax.broadcasted_iota(jnp.int32, (C, C), 0)
    ci = lax.broadcasted_iota(jnp.int32, (C, C), 1)
    causal = ci <= ri
    tri = jnp.where(causal, 1.0, 0.0).astype(BF16)

    def gla_chunk(c, carry):
        r0 = pl.multiple_of(c * C, C)
        rows = pl.ds(r0, C)
        g = g_s[rows, :]
        b = _split_dot(tri, g)
        b_last = b[C - 1:C, :]
        q = proj_s[rows, C_QG:C_QG + GLA_QK_W]
        k = proj_s[rows, C_KG:C_KG + GLA_QK_W]
        qb = (q * QK_SCALE * jnp.exp(b)).astype(BF16)
        kd = (k * jnp.exp(-b)).astype(BF16)
        kl = (k * jnp.exp(b_last - b)).astype(BF16)
        decay = jnp.exp(b_last)
        for hd in range(GLA_HEADS):
            ks_ = slice(hd * GLA_DK, (hd + 1) * GLA_DK)
            vs_ = slice(hd * GLA_DV, (hd + 1) * GLA_DV)
            v = proj_s[rows, C_VG + hd * GLA_DV:C_VG + (hd + 1) * GLA_DV].astype(BF16)
            att = jnp.where(causal, _dot_nt(qb[:, ks_], kd[:, ks_]), 0.0).astype(BF16)
            st = st_s[:, ks_]
            o = _dot(att, v) + _dot_nt(qb[:, ks_], st.astype(BF16))
            og_s[rows, vs_] = o
            st_s[:, ks_] = st * decay[:, ks_] + _dot_tn(v, kl[:, ks_])
        return carry

    lax.fori_loop(0, T // C, gla_chunk, 0)

    glag = glag_ref[...]
    for hd in range(GLA_HEADS):
        vs_ = slice(hd * GLA_DV, (hd + 1) * GLA_DV)
        o = og_s[:, vs_]
        o = o * lax.rsqrt(jnp.mean(o * o, axis=-1, keepdims=True) + EPS)
        rg = proj_s[:, C_RG + hd * GLA_DV:C_RG + (hd + 1) * GLA_DV]
        cat_s[:, vs_] = (o * glag[:, vs_] * _silu(rg)).astype(BF16)

    swag = swag_ref[...]
    grp_row = lax.broadcasted_iota(jnp.int32, (SWA_GROUP * WINDOW, 1), 0) >> 7

    def swa_block(i, carry):
        r0 = pl.multiple_of(i * WINDOW, WINDOW)
        rows = pl.ds(r0, WINDOW)
        kc = kt_s[pl.ds(r0, 2 * WINDOW), :]
        vc = vt_s[pl.ds(r0, 2 * WINDOW), :]
        first_col = jnp.where((l * (T // WINDOW) + i) == 0, WINDOW, 0)
        col = lax.broadcasted_iota(jnp.int32, (SWA_GROUP * WINDOW, 2 * WINDOW), 1)
        vis = jnp.logical_and(mask_s[...] > 0.0, col >= first_col)
        pieces = []
        for j in range(SWA_KV_HEADS):
            kj = kc[:, j * HEAD_DIM:(j + 1) * HEAD_DIM]
            vj = vc[:, j * HEAD_DIM:(j + 1) * HEAD_DIM]
            qj = jnp.concatenate(
                [qrot_s[rows, (j * SWA_GROUP + gq) * HEAD_DIM:(j * SWA_GROUP + gq + 1) * HEAD_DIM]
                 for gq in range(SWA_GROUP)], axis=0)
            s = jnp.where(vis, _dot_nt(qj, kj), -jnp.inf)
            sink = jnp.zeros((SWA_GROUP * WINDOW, 1), F32)
            for gq in range(SWA_GROUP):
                sink = jnp.where(grp_row == gq, sink_ref[j * SWA_GROUP + gq], sink)
            m = jnp.maximum(jnp.max(s, axis=-1, keepdims=True), sink)
            p = jnp.exp(s - m)
            den = jnp.sum(p, axis=-1, keepdims=True) + jnp.exp(sink - m)
            o = _dot((p / den).astype(BF16), vj)
            pieces.extend(o[gq * WINDOW:(gq + 1) * WINDOW, :] for gq in range(SWA_GROUP))
        o_s = jnp.concatenate(pieces, axis=1)
        cat_s[rows, GLA_V_W:GLA_V_W + SWA_Q_W] = _rmsnorm(o_s, swag).astype(BF16)
        return carry

    lax.fori_loop(0, T // WINDOW, swa_block, 0)

    xo_ref[...] = x + _dot(cat_s[...], wout_ref[...])

    kt_s[0:WINDOW, :] = kt_s[T:T + WINDOW, :]
    vt_s[0:WINDOW, :] = vt_s[T:T + WINDOW, :]

    @pl.when(l == n_l - 1)
    def _():
        s_all = st_s[...].T
        for hd in range(GLA_HEADS):
            so_ref[hd] = s_all[hd * GLA_DK:(hd + 1) * GLA_DK, :]


def _mix_prompt(x, sinks, nmix, win, wa2, ba, glag, swag, wout, tabs):
    B, L, _ = x.shape
    T = MIX_TILE
    const = lambda b, l: (0, 0)
    tab_spec = pl.BlockSpec((T, LANES), lambda b, l: (l, 0))
    return pl.pallas_call(
        _mix_prompt_kernel,
        grid=(B, L // T),
        in_specs=[
            pl.BlockSpec(memory_space=pltpu.SMEM),
            pl.BlockSpec((None, T, D_MODEL), lambda b, l: (b, l, 0)),
            pl.BlockSpec((1, D_MODEL), const),
            pl.BlockSpec((D_MODEL, IN_W_PAD), const),
            pl.BlockSpec((LANES, GLA_QK_W), const),
            pl.BlockSpec((1, GLA_QK_W), const),
            pl.BlockSpec((1, GLA_V_W), const),
            pl.BlockSpec((1, SWA_Q_W), const),
            pl.BlockSpec((D_MODEL, D_MODEL), const),
            tab_spec, tab_spec, tab_spec,
        ],
        out_specs=[
            pl.BlockSpec((None, T, D_MODEL), lambda b, l: (b, l, 0)),
            pl.BlockSpec((None, GLA_HEADS, GLA_DK, GLA_DV), lambda b, l: (b, 0, 0, 0)),
            pl.BlockSpec((None, WINDOW, SWA_KV_W), lambda b, l: (b, 0, 0)),
            pl.BlockSpec((None, WINDOW, SWA_KV_W), lambda b, l: (b, 0, 0)),
        ],
        out_shape=[
            jax.ShapeDtypeStruct((B, L, D_MODEL), F32),
            jax.ShapeDtypeStruct((B, GLA_HEADS, GLA_DK, GLA_DV), F32),
            jax.ShapeDtypeStruct((B, WINDOW, SWA_KV_W), F32),
            jax.ShapeDtypeStruct((B, WINDOW, SWA_KV_W), F32),
        ],
        scratch_shapes=[
            pltpu.VMEM((T, IN_W_PAD), F32),
            pltpu.VMEM((T, GLA_QK_W), F32),
            pltpu.VMEM((GLA_DV, GLA_QK_W), F32),
            pltpu.VMEM((T, SWA_Q_W), BF16),
            pltpu.VMEM((WINDOW + T, SWA_KV_W), BF16),
            pltpu.VMEM((WINDOW + T, SWA_KV_W), BF16),
            pltpu.VMEM((T, GLA_V_W), F32),
            pltpu.VMEM((T, D_MODEL), BF16),
            pltpu.VMEM((SWA_GROUP * WINDOW, 2 * WINDOW), F32),
        ],
        compiler_params=pltpu.CompilerParams(
            dimension_semantics=("arbitrary", "arbitrary"), vmem_limit_bytes=VMEM_LIMIT),
        name="mix_prompt",
    )(sinks, x, nmix, win, wa2, ba, glag, swag, wout, *tabs)


def _mix_sample_kernel(sink_ref, x_ref, s_ref, kc_ref, vc_ref, nmix_ref, win_ref, wa2_ref, ba_ref,
                       glag_ref, swag_ref, wout_ref, cos_ref, sa_ref, sb_ref,
                       xo_ref, so_ref, ko_ref, vo_ref):
    R = x_ref.shape[0]
    NS = SMP_SEQS
    Q = R // NS
    q_shift = Q.bit_length() - 1

    x = x_ref[...]
    h = _rmsnorm(x, nmix_ref[...]).astype(BF16)
    proj = _dot(h, win_ref[...])
    z = _dot(proj[:, C_A:C_A + LANES].astype(BF16), wa2_ref[...]) + ba_ref[...]
    g = _log_sigmoid(z) * (1.0 / GATE_NORM)

    ri = lax.broadcasted_iota(jnp.int32, (R, R), 0)
    ci = lax.broadcasted_iota(jnp.int32, (R, R), 1)
    same = (ri >> q_shift) == (ci >> q_shift)
    causal = jnp.logical_and(same, ci <= ri)
    after = jnp.logical_and(same, ci > ri)
    b = _split_dot(jnp.where(causal, 1.0, 0.0).astype(BF16), g)
    rest = _split_dot(jnp.where(after, 1.0, 0.0).astype(BF16), g)

    qg = proj[:, C_QG:C_QG + GLA_QK_W]
    kg = proj[:, C_KG:C_KG + GLA_QK_W]
    qb = qg * QK_SCALE * jnp.exp(b)
    kd = (kg * jnp.exp(-b)).astype(BF16)
    kl_t = (kg * jnp.exp(rest)).T
    g_t = g.T

    own_tok = ((lax.broadcasted_iota(jnp.int32, (NS * GLA_DK, R), 0) >> 6)
               == (lax.broadcasted_iota(jnp.int32, (NS * GLA_DK, R), 1) >> q_shift))
    own_seq = ((lax.broadcasted_iota(jnp.int32, (R, NS * GLA_DK), 1) >> 6)
               == (lax.broadcasted_iota(jnp.int32, (R, NS * GLA_DK), 0) >> q_shift))
    lane_half = lax.broadcasted_iota(jnp.int32, (R, LANES), 1) >> 6
    ones = jnp.ones((R, GLA_DV), BF16)

    glag = glag_ref[...]
    og = []
    for hd in range(GLA_HEADS):
        ks_ = slice(hd * GLA_DK, (hd + 1) * GLA_DK)
        v = proj[:, C_VG + hd * GLA_DV:C_VG + (hd + 1) * GLA_DV].astype(BF16)
        att = jnp.where(causal, _dot_nt(qb[:, ks_].astype(BF16), kd[:, ks_]), 0.0).astype(BF16)
        o = _dot(att, v)
        s_old = s_ref[:, hd].reshape(NS * GLA_DK, GLA_DV)
        slab = qb[:, (hd // 2) * LANES:(hd // 2 + 1) * LANES]
        dup = jnp.where(lane_half == hd % 2, slab, pltpu.roll(slab, HEAD_DIM, 1))
        q_exp = jnp.where(own_seq, jnp.concatenate([dup] * (NS // 2), axis=1), 0.0).astype(BF16)
        o = o + _dot(q_exp, s_old.astype(BF16))
        o = o * lax.rsqrt(jnp.mean(o * o, axis=-1, keepdims=True) + EPS)
        rg = proj[:, C_RG + hd * GLA_DV:C_RG + (hd + 1) * GLA_DV]
        og.append(o * glag[:, hd * GLA_DV:(hd + 1) * GLA_DV] * _silu(rg))
        kl_exp = jnp.where(own_tok, jnp.concatenate([kl_t[ks_, :]] * NS, axis=0), 0.0).astype(BF16)
        g_exp = jnp.where(own_tok, jnp.concatenate([g_t[ks_, :]] * NS, axis=0), 0.0)
        b_last = _split_dot_rhs(g_exp, ones)
        s_new = jnp.exp(b_last) * s_old + _dot(kl_exp, v)
        so_ref[:, hd] = s_new.reshape(NS, GLA_DK, GLA_DV)

    cos, sa, sb = cos_ref[...], sa_ref[...], sb_ref[...]
    q_rot = _rope(proj[:, C_QS:C_QS + SWA_Q_W], cos, sa, sb) * QK_SCALE
    k_rot = _rope(proj[:, C_KS:C_KS + SWA_KV_W], cos, sa, sb)
    v_new = proj[:, C_VS:C_VS + SWA_KV_W]

    W = kc_ref.shape[1]
    SR = SMP_SUB * Q
    GR = SWA_GROUP * SR
    sub_shift = SR.bit_length() - 1
    w_shift = W.bit_length() - 1
    r_c = lax.broadcasted_iota(jnp.int32, (GR, SMP_SUB * W), 0)
    c_c = lax.broadcasted_iota(jnp.int32, (GR, SMP_SUB * W), 1)
    vis_c = jnp.logical_and(((r_c & (SR - 1)) >> q_shift) == (c_c >> w_shift),
                            (c_c & (W - 1)) > (r_c & (Q - 1)))
    r_n = lax.broadcasted_iota(jnp.int32, (GR, SR), 0)
    c_n = lax.broadcasted_iota(jnp.int32, (GR, SR), 1)
    vis_n = jnp.logical_and(((r_n & (SR - 1)) >> q_shift) == (c_n >> q_shift),
                            (c_n & (Q - 1)) <= (r_n & (Q - 1)))
    grp_row = lax.broadcasted_iota(jnp.int32, (GR, 1), 0) >> sub_shift
    half_sr = lax.broadcasted_iota(jnp.int32, (SR, LANES), 1) >> 6
    half_gr = lax.broadcasted_iota(jnp.int32, (GR, LANES), 1) >> 6

    o_rows = []
    for sg in range(NS // SMP_SUB):
        rows = slice(sg * SR, (sg + 1) * SR)
        k_c = kc_ref[sg * SMP_SUB:(sg + 1) * SMP_SUB].reshape(SMP_SUB * W, SWA_KV_W).astype(BF16)
        v_c = vc_ref[sg * SMP_SUB:(sg + 1) * SMP_SUB].reshape(SMP_SUB * W, SWA_KV_W).astype(BF16)
        k_n = k_rot[rows, :].astype(BF16)
        v_n = v_new[rows, :].astype(BF16)
        o_kv = []
        for j in range(SWA_KV_HEADS):
            parts = []
            for gq in range(SWA_GROUP):
                hq = j * SWA_GROUP + gq
                slab = q_rot[rows, (hq // 2) * LANES:(hq // 2 + 1) * LANES]
                if hq % 2 != j:
                    slab = pltpu.roll(slab, HEAD_DIM, 1)
                parts.append(jnp.where(half_sr == j, slab, 0.0))
            q8 = jnp.concatenate(parts, axis=0).astype(BF16)
            s_c = jnp.where(vis_c, _dot_nt(q8, k_c), -jnp.inf)
            s_n = jnp.where(vis_n, _dot_nt(q8, k_n), -jnp.inf)
            sink = jnp.zeros((GR, 1), F32)
            for gq in range(SWA_GROUP):
                sink = jnp.where(grp_row == gq, sink_ref[j * SWA_GROUP + gq], sink)
            m = jnp.maximum(jnp.maximum(jnp.max(s_c, axis=-1, keepdims=True),
                                        jnp.max(s_n, axis=-1, keepdims=True)), sink)
            p_c = jnp.exp(s_c - m)
            p_n = jnp.exp(s_n - m)
            den = (jnp.sum(p_c, axis=-1, keepdims=True) + jnp.sum(p_n, axis=-1, keepdims=True)
                   + jnp.exp(sink - m))
            o = _dot((p_c / den).astype(BF16), v_c) + _dot((p_n / den).astype(BF16), v_n)
            o_kv.append(o)
        slabs = []
        for p in range(SWA_HEADS // 2):
            halves = []
            for hq in (2 * p, 2 * p + 1):
                j, gq = hq // SWA_GROUP, hq % SWA_GROUP
                piece = o_kv[j][gq * SR:(gq + 1) * SR, :]
                if hq % 2 != j:
                    piece = pltpu.roll(piece, HEAD_DIM, 1)
                halves.append(piece)
            slabs.append(jnp.where(half_sr == 0, halves[0], halves[1]))
        o_rows.append(jnp.concatenate(slabs, axis=1))
    o_s = _rmsnorm(jnp.concatenate(o_rows, axis=0), swag_ref[...])

    cat = jnp.concatenate(og + [o_s], axis=1).astype(BF16)
    xo_ref[...] = x + _dot(cat, wout_ref[...])

    ko_ref[:, 0:W - Q, :] = kc_ref[:, Q:W, :]
    vo_ref[:, 0:W - Q, :] = vc_ref[:, Q:W, :]
    for s in range(NS):
        ko_ref[s, W - Q:W, :] = k_rot[s * Q:(s + 1) * Q, :]
        vo_ref[s, W - Q:W, :] = v_new[s * Q:(s + 1) * Q, :]


def _split_dot_rhs(g, m_bf16):
    hi = g.astype(BF16)
    lo = (g - hi.astype(F32)).astype(BF16)
    return _dot(hi, m_bf16) + _dot(lo, m_bf16)


def _mix_sample(x2d, state, kc, vc, sinks, nmix, win, wa2, ba, glag, swag, wout, tabs):
    NB, W = kc.shape[0], kc.shape[1]
    Q = x2d.shape[0] // NB
    R = SMP_SEQS * Q
    const = lambda i: (0, 0)
    tab_spec = pl.BlockSpec((R, LANES), const)
    st_spec = pl.BlockSpec((SMP_SEQS, GLA_HEADS, GLA_DK, GLA_DV), lambda i: (i, 0, 0, 0))
    kv_spec = pl.BlockSpec((SMP_SEQS, W, SWA_KV_W), lambda i: (i, 0, 0))
    return pl.pallas_call(
        _mix_sample_kernel,
        grid=(NB // SMP_SEQS,),
        in_specs=[
            pl.BlockSpec(memory_space=pltpu.SMEM),
            pl.BlockSpec((R, D_MODEL), lambda i: (i, 0)),
            st_spec, kv_spec, kv_spec,
            pl.BlockSpec((1, D_MODEL), const),
            pl.BlockSpec((D_MODEL, IN_W_PAD), const),
            pl.BlockSpec((LANES, GLA_QK_W), const),
            pl.BlockSpec((1, GLA_QK_W), const),
            pl.BlockSpec((1, GLA_V_W), const),
            pl.BlockSpec((1, SWA_Q_W), const),
            pl.BlockSpec((D_MODEL, D_MODEL), const),
            tab_spec, tab_spec, tab_spec,
        ],
        out_specs=[pl.BlockSpec((R, D_MODEL), lambda i: (i, 0)), st_spec, kv_spec, kv_spec],
        out_shape=[
            jax.ShapeDtypeStruct(x2d.shape, F32),
            jax.ShapeDtypeStruct(state.shape, F32),
            jax.ShapeDtypeStruct(kc.shape, F32),
            jax.ShapeDtypeStruct(vc.shape, F32),
        ],
        compiler_params=pltpu.CompilerParams(
            dimension_semantics=("arbitrary",), vmem_limit_bytes=VMEM_LIMIT),
        name="mix_sample",
    )(sinks, x2d, state, kc, vc, nmix, win, wa2, ba, glag, swag, wout, *tabs)


def _rope_tables(pos):
    half = ROT_DIM // 2
    inv = jnp.power(jnp.float32(ROPE_THETA), -jnp.arange(half, dtype=jnp.float32) * (2.0 / ROT_DIM))
    ang = pos.astype(jnp.float32)[:, None] * inv[None, :]
    cos, sin = jnp.cos(ang), jnp.sin(ang)
    n = pos.shape[0]
    pad = jnp.zeros((n, HEAD_DIM - ROT_DIM), F32)
    zero = jnp.zeros((n, half), F32)
    c64 = jnp.concatenate([cos, cos, pad + 1.0], axis=1)
    a64 = jnp.concatenate([zero, sin, pad], axis=1)
    b64 = jnp.concatenate([-sin, zero, pad], axis=1)
    rep = LANES // HEAD_DIM
    return tuple(jnp.tile(t, (1, rep)) for t in (c64, a64, b64))


def _reorder_w_in(w):
    n_a = C_RG + GLA_V_W
    pad = jnp.zeros((D_MODEL, LANES - GATE_RANK), w.dtype)
    return jnp.concatenate([w[:, :n_a], w[:, n_a + GATE_RANK:], w[:, n_a:n_a + GATE_RANK], pad], axis=1)


def kernel(x_prompt, x_sample, state_gla, cache_swa_k, cache_swa_v, norm_ffn1, w_ffn1_gu, w_ffn1_down,
           norm_mix, w_in, w_gate_up, b_gate, gla_norm, attn_sinks, swa_norm, w_out, norm_ffn2,
           w_ffn2_gu, w_ffn2_down, norm_final):
    B, L, _ = x_prompt.shape
    NB, Q, _ = x_sample.shape
    depth = w_in.shape[0]
    W = cache_swa_k.shape[2]

    tabs_p = _rope_tables(jnp.arange(L))
    tabs_s = _rope_tables(jnp.tile(PAST_LEN + jnp.arange(Q), SMP_SEQS))

    xp = x_prompt
    xs = x_sample.reshape(NB * Q, D_MODEL)
    kc_all = cache_swa_k.reshape(depth, NB, W, SWA_KV_W)
    vc_all = cache_swa_v.reshape(depth, NB, W, SWA_KV_W)
    gla_p, k_p, v_p, gla_s, k_s, v_s = [], [], [], [], [], []
    for l in range(depth):
        last = l == depth - 1
        gu1, dn1 = w_ffn1_gu[l].astype(BF16), w_ffn1_down[l].astype(BF16)
        gu2, dn2 = w_ffn2_gu[l].astype(BF16), w_ffn2_down[l].astype(BF16)
        win = _reorder_w_in(w_in[l]).astype(BF16)
        wa2 = jnp.pad(w_gate_up[l], ((0, LANES - GATE_RANK), (0, 0))).astype(BF16)
        wout = w_out[l].astype(BF16)
        mix_w = (norm_mix[l].reshape(1, -1), win, wa2, b_gate[l].reshape(1, -1),
                 gla_norm[l].reshape(1, -1), swa_norm[l].reshape(1, -1), wout)
        nf = norm_final if last else None

        xp = _ffn(xp.reshape(B * L, D_MODEL), norm_ffn1[l], gu1, dn1).reshape(B, L, D_MODEL)
        xp, s, kk, vv = _mix_prompt(xp, attn_sinks[l], *mix_w, tabs_p)
        xp = _ffn(xp.reshape(B * L, D_MODEL), norm_ffn2[l], gu2, dn2, nf).reshape(B, L, D_MODEL)
        gla_p.append(s)
        k_p.append(kk.reshape(B, WINDOW, SWA_KV_HEADS, HEAD_DIM))
        v_p.append(vv.reshape(B, WINDOW, SWA_KV_HEADS, HEAD_DIM))

        xs = _ffn(xs, norm_ffn1[l], gu1, dn1)
        xs, s, kk, vv = _mix_sample(xs, state_gla[l], kc_all[l], vc_all[l], attn_sinks[l], *mix_w, tabs_s)
        xs = _ffn(xs, norm_ffn2[l], gu2, dn2, nf)
        gla_s.append(s)
        k_s.append(kk.reshape(NB, W, SWA_KV_HEADS, HEAD_DIM))
        v_s.append(vv.reshape(NB, W, SWA_KV_HEADS, HEAD_DIM))

    return (xp, xs.reshape(NB, Q, D_MODEL), jnp.stack(gla_p), jnp.stack(k_p), jnp.stack(v_p),
            jnp.stack(gla_s), jnp.stack(k_s), jnp.stack(v_s))
```

```python
import functools

import jax
import jax.numpy as jnp
import numpy as np
from jax import lax
from jax.experimental import pallas as pl
from jax.experimental.pallas import tpu as pltpu

F32 = jnp.float32
BF16 = jnp.bfloat16

D_MODEL = 1024
D_FF = 2816
EPS = 1e-6
GLA_HEADS = 4
GLA_DK = 64
GLA_DV = 128
GLA_QK_W = GLA_HEADS * GLA_DK
GLA_V_W = GLA_HEADS * GLA_DV
GATE_RANK = 16
GATE_NORM = 16.0
GLA_CHUNK = 64
HEAD_DIM = 64
SWA_HEADS = 8
SWA_KV_HEADS = 2
SWA_GROUP = SWA_HEADS // SWA_KV_HEADS
SWA_Q_W = SWA_HEADS * HEAD_DIM
SWA_KV_W = SWA_KV_HEADS * HEAD_DIM
WINDOW = 128
ROT_DIM = 16
ROPE_THETA = 500000.0
PAST_LEN = 16384
QK_SCALE = 0.125

LANES = 128
C_QG, C_KG, C_VG, C_RG = 0, 256, 512, 1024
C_QS, C_KS, C_VS, C_A = 1536, 2048, 2176, 2304
IN_W_PAD = C_A + LANES

FFN_TILE = 512
FFN_CHUNK = 256
MIX_TILE = 512
SMP_SEQS = 32
SMP_SUB = 8
VMEM_LIMIT = 56 * 1024 * 1024

NT_DIMS = (((1,), (1,)), ((), ()))
TN_DIMS = (((0,), (0,)), ((), ()))


def _dot(a, b):
    return jnp.dot(a, b, preferred_element_type=F32)


def _dot_nt(a, b):
    return lax.dot_general(a, b, NT_DIMS, preferred_element_type=F32)


def _dot_tn(a, b):
    return lax.dot_general(a, b, TN_DIMS, preferred_element_type=F32)


def _rmsnorm(x, g):
    return x * lax.rsqrt(jnp.mean(x * x, axis=-1, keepdims=True) + EPS) * g


def _log_sigmoid(z):
    return jnp.minimum(z, 0.0) - jnp.log1p(jnp.exp(-jnp.abs(z)))


def _silu(z):
    return z * jax.nn.sigmoid(z)


def _split_dot(m_bf16, g):
    hi = g.astype(BF16)
    lo = (g - hi.astype(F32)).astype(BF16)
    return _dot(m_bf16, hi) + _dot(m_bf16, lo)


def _rope(x, cos, sa, sb):
    half = ROT_DIM // 2
    outs = []
    for p in range(x.shape[1] // LANES):
        s = x[:, p * LANES:(p + 1) * LANES]
        outs.append(s * cos + pltpu.roll(s, half, 1) * sa + pltpu.roll(s, LANES - half, 1) * sb)
    return outs[0] if len(outs) == 1 else jnp.concatenate(outs, axis=1)


def _ffn_kernel(*refs, final):
    if final:
        x_ref, n_ref, wgu_ref, wd_ref, nf_ref, o_ref, acc_ref = refs
    else:
        x_ref, n_ref, wgu_ref, wd_ref, o_ref, acc_ref = refs
    x = x_ref[...]
    h = _rmsnorm(x, n_ref[...]).astype(BF16)
    for c in range(D_FF // FFN_CHUNK):
        lo = c * FFN_CHUNK
        gate = _dot(h, wgu_ref[:, lo:lo + FFN_CHUNK])
        up = _dot(h, wgu_ref[:, D_FF + lo:D_FF + lo + FFN_CHUNK])
        act = (_silu(gate) * up).astype(BF16)
        part = _dot(act, wd_ref[lo:lo + FFN_CHUNK, :])
        if c == 0:
            acc_ref[...] = part
        else:
            acc_ref[...] += part
    y = x + 0.5 * acc_ref[...]
    if final:
        y = _rmsnorm(y, nf_ref[...])
    o_ref[...] = y


def _ffn(x2d, norm, w_gu, w_down, norm_final=None):
    n_tok = x2d.shape[0]
    final = norm_final is not None
    const = lambda i: (0, 0)
    in_specs = [
        pl.BlockSpec((FFN_TILE, D_MODEL), lambda i: (i, 0)),
        pl.BlockSpec((1, D_MODEL), const),
        pl.BlockSpec((D_MODEL, 2 * D_FF), const, pipeline_mode=pl.Buffered(1)),
        pl.BlockSpec((D_FF, D_MODEL), const, pipeline_mode=pl.Buffered(1)),
    ]
    args = [x2d, norm.reshape(1, D_MODEL), w_gu, w_down]
    if final:
        in_specs.append(pl.BlockSpec((1, D_MODEL), const))
        args.append(norm_final.reshape(1, D_MODEL))
    return pl.pallas_call(
        functools.partial(_ffn_kernel, final=final),
        grid=(n_tok // FFN_TILE,),
        in_specs=in_specs,
        out_specs=pl.BlockSpec((FFN_TILE, D_MODEL), lambda i: (i, 0)),
        out_shape=jax.ShapeDtypeStruct((n_tok, D_MODEL), F32),
        scratch_shapes=[pltpu.VMEM((FFN_TILE, D_MODEL), F32)],
        compiler_params=pltpu.CompilerParams(
            dimension_semantics=("arbitrary",), vmem_limit_bytes=VMEM_LIMIT),
        name="ffn_final" if final else "ffn",
    )(*args)


def _mix_prompt_kernel(sink_ref, x_ref, nmix_ref, win_ref, wa2_ref, ba_ref, glag_ref, swag_ref,
                       wout_ref, cos_ref, sa_ref, sb_ref,
                       xo_ref, so_ref, ko_ref, vo_ref,
                       sbd_s, qb_s, kd_s, klt_s, dect_s, v_s, gate_s, og_s, qh_s, kt_s, vt_s, cat_s):
    T = MIX_TILE
    C = GLA_CHUNK
    G = 256
    l = pl.program_id(1)
    n_l = pl.num_programs(1)

    @pl.when(l == 0)
    def _():
        sbd_s[...] = jnp.zeros_like(sbd_s)
        kt_s[0:WINDOW, :] = jnp.zeros((WINDOW, SWA_KV_W), BF16)
        vt_s[0:WINDOW, :] = jnp.zeros((WINDOW, SWA_KV_W), BF16)

    x = x_ref[...]
    h = _rmsnorm(x, nmix_ref[...]).astype(BF16)

    a_low = _dot(h, win_ref[:, C_A:C_A + LANES]).astype(BF16)
    z = _dot(a_low, wa2_ref[...]) + ba_ref[...]
    g = _log_sigmoid(z) * (1.0 / GATE_NORM)
    qk = _dot(h, win_ref[:, C_QG:C_QG + 2 * GLA_QK_W])

    ri = lax.broadcasted_iota(jnp.int32, (G, G), 0)
    ci = lax.broadcasted_iota(jnp.int32, (G, G), 1)
    same_chunk = (ri >> 6) == (ci >> 6)
    incl = jnp.where(jnp.logical_and(same_chunk, ci <= ri), 1.0, 0.0).astype(BF16)
    excl = jnp.where(jnp.logical_and(same_chunk, ci > ri), 1.0, 0.0).astype(BF16)
    for r in range(T // G):
        rows = slice(r * G, (r + 1) * G)
        b = _split_dot(incl, g[rows])
        rest = _split_dot(excl, g[rows])
        q = qk[rows, 0:GLA_QK_W]
        k = qk[rows, GLA_QK_W:2 * GLA_QK_W]
        qb_s[rows, :] = (q * QK_SCALE * jnp.exp(b)).astype(BF16)
        kd_s[rows, :] = k * jnp.exp(-b)
        klt_s[:, rows] = (k * jnp.exp(rest)).T.astype(BF16)
    rs = lax.broadcasted_iota(jnp.int32, (LANES, T), 0)
    cs = lax.broadcasted_iota(jnp.int32, (LANES, T), 1)
    chunk_sel = jnp.where(rs == (cs >> 6), 1.0, 0.0).astype(BF16)
    dect_s[...] = jnp.exp(_split_dot(chunk_sel, g)).T

    v_s[...] = _dot(h, win_ref[:, C_VG:C_VG + GLA_V_W])
    gate_s[...] = _silu(_dot(h, win_ref[:, C_RG:C_RG + GLA_V_W])) * glag_ref[...]

    hk_r = lax.broadcasted_iota(jnp.int32, (GLA_QK_W, GLA_QK_W), 0) >> 6
    hk_c = lax.broadcasted_iota(jnp.int32, (GLA_QK_W, GLA_QK_W), 1) >> 6
    bd_k = hk_r == hk_c
    hv_r = lax.broadcasted_iota(jnp.int32, (GLA_QK_W, GLA_V_W), 0) >> 6
    hv_c = lax.broadcasted_iota(jnp.int32, (GLA_QK_W, GLA_V_W), 1) >> 7
    bd_v = hv_r == hv_c
    qi = lax.broadcasted_iota(jnp.int32, (C, GLA_QK_W), 0)
    kj = lax.broadcasted_iota(jnp.int32, (C, GLA_QK_W), 1) & (C - 1)
    causal = kj <= qi
    zeros_v = jnp.zeros((C, GLA_V_W), BF16)
    for c in range(T // C):
        rows = slice(c * C, (c + 1) * C)
        qb = qb_s[rows, :]
        kd_bd = jnp.where(bd_k, jnp.concatenate([kd_s[rows, :]] * GLA_HEADS, axis=0), 0.0).astype(BF16)
        att = jnp.where(causal, _dot_nt(qb, kd_bd), 0.0).astype(BF16)
        v = v_s[rows, :]
        v_bd = jnp.where(bd_v, jnp.concatenate([v] * GLA_HEADS, axis=0), 0.0).astype(BF16)
        s_old = sbd_s[...]
        og_s[rows, :] = _dot(att, v_bd) + _dot(qb, s_old.astype(BF16))
        v16 = v.astype(BF16)
        v_pair = jnp.concatenate([v16, zeros_v] if c % 2 == 0 else [zeros_v, v16], axis=0)
        p = c // 2
        upd = _dot(klt_s[:, p * 2 * C:(p + 1) * 2 * C], v_pair)
        sbd_s[...] = dect_s[:, c:c + 1] * s_old + jnp.where(bd_v, upd, 0.0)

    for hd in range(GLA_HEADS):
        vs_ = slice(hd * GLA_DV, (hd + 1) * GLA_DV)
        o = og_s[:, vs_]
        o = o * lax.rsqrt(jnp.mean(o * o, axis=-1, keepdims=True) + EPS)
        cat_s[:, vs_] = (o * gate_s[:, vs_]).astype(BF16)

    cos, sa, sb = cos_ref[...], sa_ref[...], sb_ref[...]
    half_t = lax.broadcasted_iota(jnp.int32, (T, LANES), 1) >> 6
    q_rot = _rope(_dot(h, win_ref[:, C_QS:C_QS + SWA_Q_W]), cos, sa, sb) * QK_SCALE
    for hq in range(SWA_HEADS):
        slab = q_rot[:, (hq // 2) * LANES:(hq // 2 + 1) * LANES]
        j = hq // SWA_GROUP
        if hq % 2 != j:
            slab = pltpu.roll(slab, HEAD_DIM, 1)
        qh_s[hq] = jnp.where(half_t == j, slab, 0.0).astype(BF16)
    k_rot = _rope(_dot(h, win_ref[:, C_KS:C_KS + SWA_KV_W]), cos, sa, sb)
    v_swa = _dot(h, win_ref[:, C_VS:C_VS + SWA_KV_W])
    kt_s[WINDOW:WINDOW + T, :] = k_rot.astype(BF16)
    vt_s[WINDOW:WINDOW + T, :] = v_swa.astype(BF16)
    ko_ref[...] = k_rot[T - WINDOW:T, :]
    vo_ref[...] = v_swa[T - WINDOW:T, :]

    swag = swag_ref[...]
    GR = SWA_GROUP * WINDOW
    grp_row = lax.broadcasted_iota(jnp.int32, (GR, 1), 0) >> 7
    qr = lax.broadcasted_iota(jnp.int32, (GR, 2 * WINDOW), 0) & (WINDOW - 1)
    kc_ = lax.broadcasted_iota(jnp.int32, (GR, 2 * WINDOW), 1)
    margin = jnp.where(kc_ < WINDOW, kc_ - qr, qr - kc_ + WINDOW + 1)
    half_w = lax.broadcasted_iota(jnp.int32, (WINDOW, LANES), 1) >> 6
    for i in range(T // WINDOW):
        rows = slice(i * WINDOW, (i + 1) * WINDOW)
        k_blk = kt_s[i * WINDOW:(i + 2) * WINDOW, :]
        v_blk = vt_s[i * WINDOW:(i + 2) * WINDOW, :]
        if i == 0:
            first_col = jnp.where(l == 0, WINDOW, 0)
            vis = jnp.logical_and(margin > 0, kc_ >= first_col)
        else:
            vis = margin > 0
        o_kv = []
        for j in range(SWA_KV_HEADS):
            q4 = jnp.concatenate([qh_s[j * SWA_GROUP + gq, rows, :] for gq in range(SWA_GROUP)], axis=0)
            s = jnp.where(vis, _dot_nt(q4, k_blk), -jnp.inf)
            sink = jnp.zeros((GR, 1), F32)
            for gq in range(SWA_GROUP):
                sink = jnp.where(grp_row == gq, sink_ref[j * SWA_GROUP + gq], sink)
            m = jnp.maximum(jnp.max(s, axis=-1, keepdims=True), sink)
            pr = jnp.exp(s - m)
            den = jnp.sum(pr, axis=-1, keepdims=True) + jnp.exp(sink - m)
            o_kv.append(_dot((pr / den).astype(BF16), v_blk))
        slabs = []
        for p in range(SWA_HEADS // 2):
            halves = []
            for hq in (2 * p, 2 * p + 1):
                j, gq = hq // SWA_GROUP, hq % SWA_GROUP
                piece = o_kv[j][gq * WINDOW:(gq + 1) * WINDOW, :]
                if hq % 2 != j:
                    piece = pltpu.roll(piece, HEAD_DIM, 1)
                halves.append(piece)
            slabs.append(jnp.where(half_w == 0, halves[0], halves[1]))
        o_s = jnp.concatenate(slabs, axis=1)
        cat_s[rows, GLA_V_W:GLA_V_W + SWA_Q_W] = _rmsnorm(o_s, swag).astype(BF16)

    xo_ref[...] = x + _dot(cat_s[...], wout_ref[...])

    kt_s[0:WINDOW, :] = kt_s[T:T + WINDOW, :]
    vt_s[0:WINDOW, :] = vt_s[T:T + WINDOW, :]

    @pl.when(l == n_l - 1)
    def _():
        for hd in range(GLA_HEADS):
            so_ref[hd] = sbd_s[hd * GLA_DK:(hd + 1) * GLA_DK, hd * GLA_DV:(hd + 1) * GLA_DV]


def _mix_prompt(x, sinks, nmix, win, wa2, ba, glag, swag, wout, tabs):
    B, L, _ = x.shape
    T = MIX_TILE
    const = lambda b, l: (0, 0)
    tab_spec = pl.BlockSpec((T, LANES), lambda b, l: (l, 0))
    return pl.pallas_call(
        _mix_prompt_kernel,
        grid=(B, L // T),
        in_specs=[
            pl.BlockSpec(memory_space=pltpu.SMEM),
            pl.BlockSpec((None, T, D_MODEL), lambda b, l: (b, l, 0)),
            pl.BlockSpec((1, D_MODEL), const),
            pl.BlockSpec((D_MODEL, IN_W_PAD), const),
            pl.BlockSpec((LANES, GLA_QK_W), const),
            pl.BlockSpec((1, GLA_QK_W), const),
            pl.BlockSpec((1, GLA_V_W), const),
            pl.BlockSpec((1, SWA_Q_W), const),
            pl.BlockSpec((D_MODEL, D_MODEL), const),
            tab_spec, tab_spec, tab_spec,
        ],
        out_specs=[
            pl.BlockSpec((None, T, D_MODEL), lambda b, l: (b, l, 0)),
            pl.BlockSpec((None, GLA_HEADS, GLA_DK, GLA_DV), lambda b, l: (b, 0, 0, 0)),
            pl.BlockSpec((None, WINDOW, SWA_KV_W), lambda b, l: (b, 0, 0)),
            pl.BlockSpec((None, WINDOW, SWA_KV_W), lambda b, l: (b, 0, 0)),
        ],
        out_shape=[
            jax.ShapeDtypeStruct((B, L, D_MODEL), F32),
            jax.ShapeDtypeStruct((B, GLA_HEADS, GLA_DK, GLA_DV), F32),
            jax.ShapeDtypeStruct((B, WINDOW, SWA_KV_W), F32),
            jax.ShapeDtypeStruct((B, WINDOW, SWA_KV_W), F32),
        ],
        scratch_shapes=[
            pltpu.VMEM((GLA_QK_W, GLA_V_W), F32),
            pltpu.VMEM((T, GLA_QK_W), BF16),
            pltpu.VMEM((T, GLA_QK_W), F32),
            pltpu.VMEM((GLA_QK_W, T), BF16),
            pltpu.VMEM((GLA_QK_W, LANES), F32),
            pltpu.VMEM((T, GLA_V_W), F32),
            pltpu.VMEM((T, GLA_V_W), F32),
            pltpu.VMEM((T, GLA_V_W), F32),
            pltpu.VMEM((SWA_HEADS, T, LANES), BF16),
            pltpu.VMEM((WINDOW + T, SWA_KV_W), BF16),
            pltpu.VMEM((WINDOW + T, SWA_KV_W), BF16),
            pltpu.VMEM((T, D_MODEL), BF16),
        ],
        compiler_params=pltpu.CompilerParams(
            dimension_semantics=("arbitrary", "arbitrary"), vmem_limit_bytes=VMEM_LIMIT),
        name="mix_prompt",
    )(sinks, x, nmix, win, wa2, ba, glag, swag, wout, *tabs)


def _mix_sample_kernel(sink_ref, x_ref, s_ref, kc_ref, vc_ref, nmix_ref, win_ref, wa2_ref, ba_ref,
                       glag_ref, swag_ref, wout_ref, cos_ref, sa_ref, sb_ref,
                       xo_ref, so_ref, ko_ref, vo_ref):
    R = x_ref.shape[0]
    NS = SMP_SEQS
    Q = R // NS
    q_shift = Q.bit_length() - 1

    x = x_ref[...]
    h = _rmsnorm(x, nmix_ref[...]).astype(BF16)
    proj = _dot(h, win_ref[...])
    z = _dot(proj[:, C_A:C_A + LANES].astype(BF16), wa2_ref[...]) + ba_ref[...]
    g = _log_sigmoid(z) * (1.0 / GATE_NORM)

    ri = lax.broadcasted_iota(jnp.int32, (R, R), 0)
    ci = lax.broadcasted_iota(jnp.int32, (R, R), 1)
    same = (ri >> q_shift) == (ci >> q_shift)
    causal = jnp.logical_and(same, ci <= ri)
    after = jnp.logical_and(same, ci > ri)
    b = _split_dot(jnp.where(causal, 1.0, 0.0).astype(BF16), g)
    rest = _split_dot(jnp.where(after, 1.0, 0.0).astype(BF16), g)

    qg = proj[:, C_QG:C_QG + GLA_QK_W]
    kg = proj[:, C_KG:C_KG + GLA_QK_W]
    qb = qg * QK_SCALE * jnp.exp(b)
    kd = (kg * jnp.exp(-b)).astype(BF16)
    kl_t = (kg * jnp.exp(rest)).T
    g_t = g.T

    own_tok = ((lax.broadcasted_iota(jnp.int32, (NS * GLA_DK, R), 0) >> 6)
               == (lax.broadcasted_iota(jnp.int32, (NS * GLA_DK, R), 1) >> q_shift))
    own_seq = ((lax.broadcasted_iota(jnp.int32, (R, NS * GLA_DK), 1) >> 6)
               == (lax.broadcasted_iota(jnp.int32, (R, NS * GLA_DK), 0) >> q_shift))
    lane_half = lax.broadcasted_iota(jnp.int32, (R, LANES), 1) >> 6
    ones = jnp.ones((R, GLA_DV), BF16)

    glag = glag_ref[...]
    og = []
    for hd in range(GLA_HEADS):
        ks_ = slice(hd * GLA_DK, (hd + 1) * GLA_DK)
        v = proj[:, C_VG + hd * GLA_DV:C_VG + (hd + 1) * GLA_DV].astype(BF16)
        att = jnp.where(causal, _dot_nt(qb[:, ks_].astype(BF16), kd[:, ks_]), 0.0).astype(BF16)
        o = _dot(att, v)
        s_old = s_ref[:, hd].reshape(NS * GLA_DK, GLA_DV)
        slab = qb[:, (hd // 2) * LANES:(hd // 2 + 1) * LANES]
        dup = jnp.where(lane_half == hd % 2, slab, pltpu.roll(slab, HEAD_DIM, 1))
        q_exp = jnp.where(own_seq, jnp.concatenate([dup] * (NS // 2), axis=1), 0.0).astype(BF16)
        o = o + _dot(q_exp, s_old.astype(BF16))
        o = o * lax.rsqrt(jnp.mean(o * o, axis=-1, keepdims=True) + EPS)
        rg = proj[:, C_RG + hd * GLA_DV:C_RG + (hd + 1) * GLA_DV]
        og.append(o * glag[:, hd * GLA_DV:(hd + 1) * GLA_DV] * _silu(rg))
        kl_exp = jnp.where(own_tok, jnp.concatenate([kl_t[ks_, :]] * NS, axis=0), 0.0).astype(BF16)
        g_exp = jnp.where(own_tok, jnp.concatenate([g_t[ks_, :]] * NS, axis=0), 0.0)
        b_last = _split_dot_rhs(g_exp, ones)
        s_new = jnp.exp(b_last) * s_old + _dot(kl_exp, v)
        so_ref[:, hd] = s_new.reshape(NS, GLA_DK, GLA_DV)

    cos, sa, sb = cos_ref[...], sa_ref[...], sb_ref[...]
    q_rot = _rope(proj[:, C_QS:C_QS + SWA_Q_W], cos, sa, sb) * QK_SCALE
    k_rot = _rope(proj[:, C_KS:C_KS + SWA_KV_W], cos, sa, sb)
    v_new = proj[:, C_VS:C_VS + SWA_KV_W]

    W = kc_ref.shape[1]
    SR = SMP_SUB * Q
    GR = SWA_GROUP * SR
    sub_shift = SR.bit_length() - 1
    w_shift = W.bit_length() - 1
    r_c = lax.broadcasted_iota(jnp.int32, (GR, SMP_SUB * W), 0)
    c_c = lax.broadcasted_iota(jnp.int32, (GR, SMP_SUB * W), 1)
    vis_c = jnp.logical_and(((r_c & (SR - 1)) >> q_shift) == (c_c >> w_shift),
                            (c_c & (W - 1)) > (r_c & (Q - 1)))
    r_n = lax.broadcasted_iota(jnp.int32, (GR, SR), 0)
    c_n = lax.broadcasted_iota(jnp.int32, (GR, SR), 1)
    vis_n = jnp.logical_and(((r_n & (SR - 1)) >> q_shift) == (c_n >> q_shift),
                            (c_n & (Q - 1)) <= (r_n & (Q - 1)))
    grp_row = lax.broadcasted_iota(jnp.int32, (GR, 1), 0) >> sub_shift
    half_sr = lax.broadcasted_iota(jnp.int32, (SR, LANES), 1) >> 6
    half_gr = lax.broadcasted_iota(jnp.int32, (GR, LANES), 1) >> 6

    o_rows = []
    for sg in range(NS // SMP_SUB):
        rows = slice(sg * SR, (sg + 1) * SR)
        k_c = kc_ref[sg * SMP_SUB:(sg + 1) * SMP_SUB].reshape(SMP_SUB * W, SWA_KV_W).astype(BF16)
        v_c = vc_ref[sg * SMP_SUB:(sg + 1) * SMP_SUB].reshape(SMP_SUB * W, SWA_KV_W).astype(BF16)
        k_n = k_rot[rows, :].astype(BF16)
        v_n = v_new[rows, :].astype(BF16)
        o_kv = []
        for j in range(SWA_KV_HEADS):
            parts = []
            for gq in range(SWA_GROUP):
                hq = j * SWA_GROUP + gq
                slab = q_rot[rows, (hq // 2) * LANES:(hq // 2 + 1) * LANES]
                if hq % 2 != j:
                    slab = pltpu.roll(slab, HEAD_DIM, 1)
                parts.append(jnp.where(half_sr == j, slab, 0.0))
            q8 = jnp.concatenate(parts, axis=0).astype(BF16)
            s_c = jnp.where(vis_c, _dot_nt(q8, k_c), -jnp.inf)
            s_n = jnp.where(vis_n, _dot_nt(q8, k_n), -jnp.inf)
            sink = jnp.zeros((GR, 1), F32)
            for gq in range(SWA_GROUP):
                sink = jnp.where(grp_row == gq, sink_ref[j * SWA_GROUP + gq], sink)
            m = jnp.maximum(jnp.maximum(jnp.max(s_c, axis=-1, keepdims=True),
                                        jnp.max(s_n, axis=-1, keepdims=True)), sink)
            p_c = jnp.exp(s_c - m)
            p_n = jnp.exp(s_n - m)
            den = (jnp.sum(p_c, axis=-1, keepdims=True) + jnp.sum(p_n, axis=-1, keepdims=True)
                   + jnp.exp(sink - m))
            o = _dot((p_c / den).astype(BF16), v_c) + _dot((p_n / den).astype(BF16), v_n)
            o_kv.append(o)
        slabs = []
        for p in range(SWA_HEADS // 2):
            halves = []
            for hq in (2 * p, 2 * p + 1):
                j, gq = hq // SWA_GROUP, hq % SWA_GROUP
                piece = o_kv[j][gq * SR:(gq + 1) * SR, :]
                if hq % 2 != j:
                    piece = pltpu.roll(piece, HEAD_DIM, 1)
                halves.append(piece)
            slabs.append(jnp.where(half_sr == 0, halves[0], halves[1]))
        o_rows.append(jnp.concatenate(slabs, axis=1))
    o_s = _rmsnorm(jnp.concatenate(o_rows, axis=0), swag_ref[...])

    cat = jnp.concatenate(og + [o_s], axis=1).astype(BF16)
    xo_ref[...] = x + _dot(cat, wout_ref[...])

    ko_ref[:, 0:W - Q, :] = kc_ref[:, Q:W, :]
    vo_ref[:, 0:W - Q, :] = vc_ref[:, Q:W, :]
    for s in range(NS):
        ko_ref[s, W - Q:W, :] = k_rot[s * Q:(s + 1) * Q, :]
        vo_ref[s, W - Q:W, :] = v_new[s * Q:(s + 1) * Q, :]


def _split_dot_rhs(g, m_bf16):
    hi = g.astype(BF16)
    lo = (g - hi.astype(F32)).astype(BF16)
    return _dot(hi, m_bf16) + _dot(lo, m_bf16)


def _mix_sample(x2d, state, kc, vc, sinks, nmix, win, wa2, ba, glag, swag, wout, tabs):
    NB, W = kc.shape[0], kc.shape[1]
    Q = x2d.shape[0] // NB
    R = SMP_SEQS * Q
    const = lambda i: (0, 0)
    tab_spec = pl.BlockSpec((R, LANES), const)
    st_spec = pl.BlockSpec((SMP_SEQS, GLA_HEADS, GLA_DK, GLA_DV), lambda i: (i, 0, 0, 0))
    kv_spec = pl.BlockSpec((SMP_SEQS, W, SWA_KV_W), lambda i: (i, 0, 0))
    return pl.pallas_call(
        _mix_sample_kernel,
        grid=(NB // SMP_SEQS,),
        in_specs=[
            pl.BlockSpec(memory_space=pltpu.SMEM),
            pl.BlockSpec((R, D_MODEL), lambda i: (i, 0)),
            st_spec, kv_spec, kv_spec,
            pl.BlockSpec((1, D_MODEL), const),
            pl.BlockSpec((D_MODEL, IN_W_PAD), const),
            pl.BlockSpec((LANES, GLA_QK_W), const),
            pl.BlockSpec((1, GLA_QK_W), const),
            pl.BlockSpec((1, GLA_V_W), const),
            pl.BlockSpec((1, SWA_Q_W), const),
            pl.BlockSpec((D_MODEL, D_MODEL), const),
            tab_spec, tab_spec, tab_spec,
        ],
        out_specs=[pl.BlockSpec((R, D_MODEL), lambda i: (i, 0)), st_spec, kv_spec, kv_spec],
        out_shape=[
            jax.ShapeDtypeStruct(x2d.shape, F32),
            jax.ShapeDtypeStruct(state.shape, F32),
            jax.ShapeDtypeStruct(kc.shape, F32),
            jax.ShapeDtypeStruct(vc.shape, F32),
        ],
        compiler_params=pltpu.CompilerParams(
            dimension_semantics=("arbitrary",), vmem_limit_bytes=VMEM_LIMIT),
        name="mix_sample",
    )(sinks, x2d, state, kc, vc, nmix, win, wa2, ba, glag, swag, wout, *tabs)


def _rope_tables(pos):
    half = ROT_DIM // 2
    inv = jnp.power(jnp.float32(ROPE_THETA), -jnp.arange(half, dtype=jnp.float32) * (2.0 / ROT_DIM))
    ang = pos.astype(jnp.float32)[:, None] * inv[None, :]
    cos, sin = jnp.cos(ang), jnp.sin(ang)
    n = pos.shape[0]
    pad = jnp.zeros((n, HEAD_DIM - ROT_DIM), F32)
    zero = jnp.zeros((n, half), F32)
    c64 = jnp.concatenate([cos, cos, pad + 1.0], axis=1)
    a64 = jnp.concatenate([zero, sin, pad], axis=1)
    b64 = jnp.concatenate([-sin, zero, pad], axis=1)
    rep = LANES // HEAD_DIM
    return tuple(jnp.tile(t, (1, rep)) for t in (c64, a64, b64))


def _reorder_w_in(w):
    n_a = C_RG + GLA_V_W
    pad = jnp.zeros((D_MODEL, LANES - GATE_RANK), w.dtype)
    return jnp.concatenate([w[:, :n_a], w[:, n_a + GATE_RANK:], w[:, n_a:n_a + GATE_RANK], pad], axis=1)


def kernel(x_prompt, x_sample, state_gla, cache_swa_k, cache_swa_v, norm_ffn1, w_ffn1_gu, w_ffn1_down,
           norm_mix, w_in, w_gate_up, b_gate, gla_norm, attn_sinks, swa_norm, w_out, norm_ffn2,
           w_ffn2_gu, w_ffn2_down, norm_final):
    B, L, _ = x_prompt.shape
    NB, Q, _ = x_sample.shape
    depth = w_in.shape[0]
    W = cache_swa_k.shape[2]

    tabs_p = _rope_tables(jnp.arange(L))
    tabs_s = _rope_tables(jnp.tile(PAST_LEN + jnp.arange(Q), SMP_SEQS))

    xp = x_prompt
    xs = x_sample.reshape(NB * Q, D_MODEL)
    kc_all = cache_swa_k.reshape(depth, NB, W, SWA_KV_W)
    vc_all = cache_swa_v.reshape(depth, NB, W, SWA_KV_W)
    gla_p, k_p, v_p, gla_s, k_s, v_s = [], [], [], [], [], []
    for l in range(depth):
        last = l == depth - 1
        gu1, dn1 = w_ffn1_gu[l].astype(BF16), w_ffn1_down[l].astype(BF16)
        gu2, dn2 = w_ffn2_gu[l].astype(BF16), w_ffn2_down[l].astype(BF16)
        win = _reorder_w_in(w_in[l]).astype(BF16)
        wa2 = jnp.pad(w_gate_up[l], ((0, LANES - GATE_RANK), (0, 0))).astype(BF16)
        wout = w_out[l].astype(BF16)
        mix_w = (norm_mix[l].reshape(1, -1), win, wa2, b_gate[l].reshape(1, -1),
                 gla_norm[l].reshape(1, -1), swa_norm[l].reshape(1, -1), wout)
        nf = norm_final if last else None

        xp = _ffn(xp.reshape(B * L, D_MODEL), norm_ffn1[l], gu1, dn1).reshape(B, L, D_MODEL)
        xp, s, kk, vv = _mix_prompt(xp, attn_sinks[l], *mix_w, tabs_p)
        xp = _ffn(xp.reshape(B * L, D_MODEL), norm_ffn2[l], gu2, dn2, nf).reshape(B, L, D_MODEL)
        gla_p.append(s)
        k_p.append(kk.reshape(B, WINDOW, SWA_KV_HEADS, HEAD_DIM))
        v_p.append(vv.reshape(B, WINDOW, SWA_KV_HEADS, HEAD_DIM))

        xs = _ffn(xs, norm_ffn1[l], gu1, dn1)
        xs, s, kk, vv = _mix_sample(xs, state_gla[l], kc_all[l], vc_all[l], attn_sinks[l], *mix_w, tabs_s)
        xs = _ffn(xs, norm_ffn2[l], gu2, dn2, nf)
        gla_s.append(s)
        k_s.append(kk.reshape(NB, W, SWA_KV_HEADS, HEAD_DIM))
        v_s.append(vv.reshape(NB, W, SWA_KV_HEADS, HEAD_DIM))

    return (xp, xs.reshape(NB, Q, D_MODEL), jnp.stack(gla_p), jnp.stack(k_p), jnp.stack(v_p),
            jnp.stack(gla_s), jnp.stack(k_s), jnp.stack(v_s))
```

```python
import functools

import jax
import jax.numpy as jnp
import numpy as np
from jax import lax
from jax.experimental import pallas as pl
from jax.experimental.pallas import tpu as pltpu

F32 = jnp.float32
BF16 = jnp.bfloat16

D_MODEL = 1024
D_FF = 2816
EPS = 1e-6
GLA_HEADS = 4
GLA_DK = 64
GLA_DV = 128
GLA_QK_W = GLA_HEADS * GLA_DK
GLA_V_W = GLA_HEADS * GLA_DV
GATE_RANK = 16
GATE_NORM = 16.0
GLA_CHUNK = 64
HEAD_DIM = 64
SWA_HEADS = 8
SWA_KV_HEADS = 2
SWA_GROUP = SWA_HEADS // SWA_KV_HEADS
SWA_Q_W = SWA_HEADS * HEAD_DIM
SWA_KV_W = SWA_KV_HEADS * HEAD_DIM
WINDOW = 128
ROT_DIM = 16
ROPE_THETA = 500000.0
PAST_LEN = 16384
QK_SCALE = 0.125

LANES = 128
C_QG, C_KG, C_VG, C_RG = 0, 256, 512, 1024
C_QS, C_KS, C_VS, C_A = 1536, 2048, 2176, 2304
IN_W_PAD = C_A + LANES

FFN_TILE = 512
FFN_CHUNK = 256
MIX_TILE = 512
SMP_SEQS = 32
SMP_SUB = 8
VMEM_LIMIT = 56 * 1024 * 1024

NT_DIMS = (((1,), (1,)), ((), ()))
TN_DIMS = (((0,), (0,)), ((), ()))


def _dot(a, b):
    return jnp.dot(a, b, preferred_element_type=F32)


def _dot_nt(a, b):
    return lax.dot_general(a, b, NT_DIMS, preferred_element_type=F32)


def _dot_tn(a, b):
    return lax.dot_general(a, b, TN_DIMS, preferred_element_type=F32)


def _rmsnorm(x, g):
    return x * lax.rsqrt(jnp.mean(x * x, axis=-1, keepdims=True) + EPS) * g


def _log_sigmoid(z):
    return jnp.minimum(z, 0.0) - jnp.log1p(jnp.exp(-jnp.abs(z)))


def _silu(z):
    return z * jax.nn.sigmoid(z)


def _split_dot(m_bf16, g):
    hi = g.astype(BF16)
    lo = (g - hi.astype(F32)).astype(BF16)
    return _dot(m_bf16, hi) + _dot(m_bf16, lo)


def _rope(x, cos, sa, sb):
    half = ROT_DIM // 2
    outs = []
    for p in range(x.shape[1] // LANES):
        s = x[:, p * LANES:(p + 1) * LANES]
        outs.append(s * cos + pltpu.roll(s, half, 1) * sa + pltpu.roll(s, LANES - half, 1) * sb)
    return outs[0] if len(outs) == 1 else jnp.concatenate(outs, axis=1)


def _cast_kernel(*refs):
    n = len(refs) // 2
    for src, dst in zip(refs[:n], refs[n:]):
        dst[...] = src[...].astype(BF16)


def _cast_bf16(arrs, block_rows):
    depth, rows, cols = arrs[0].shape
    spec = pl.BlockSpec((None, block_rows, cols), lambda l, r: (l, r, 0))
    n = len(arrs)
    return pl.pallas_call(
        _cast_kernel,
        grid=(depth, rows // block_rows),
        in_specs=[spec] * n,
        out_specs=[spec] * n,
        out_shape=[jax.ShapeDtypeStruct(a.shape, BF16) for a in arrs],
        compiler_params=pltpu.CompilerParams(
            dimension_semantics=("arbitrary", "arbitrary"), vmem_limit_bytes=VMEM_LIMIT),
        name="cast_weights",
    )(*arrs)


def _cast_w_in_kernel(w_ref, o_ref):
    n_a = C_RG + GLA_V_W
    w = w_ref[...]
    o_ref[:, 0:n_a] = w[:, 0:n_a].astype(BF16)
    o_ref[:, n_a:C_A] = w[:, n_a + GATE_RANK:].astype(BF16)
    pad = jnp.zeros((w.shape[0], LANES - GATE_RANK), F32)
    o_ref[:, C_A:IN_W_PAD] = jnp.concatenate([w[:, n_a:n_a + GATE_RANK], pad], axis=1).astype(BF16)


def _cast_w_in(w_in):
    depth, rows, cols = w_in.shape
    block_rows = 256
    return pl.pallas_call(
        _cast_w_in_kernel,
        grid=(depth, rows // block_rows),
        in_specs=[pl.BlockSpec((None, block_rows, cols), lambda l, r: (l, r, 0))],
        out_specs=pl.BlockSpec((None, block_rows, IN_W_PAD), lambda l, r: (l, r, 0)),
        out_shape=jax.ShapeDtypeStruct((depth, rows, IN_W_PAD), BF16),
        compiler_params=pltpu.CompilerParams(
            dimension_semantics=("arbitrary", "arbitrary"), vmem_limit_bytes=VMEM_LIMIT),
        name="cast_w_in",
    )(w_in)


def _ffn_kernel(*refs, final, n_p):
    if final:
        xp_ref, xs_ref, n_ref, wgu_ref, wd_ref, nf_ref, op_ref, os_ref, acc_ref = refs
    else:
        xp_ref, xs_ref, n_ref, wgu_ref, wd_ref, op_ref, os_ref, acc_ref = refs
    i = pl.program_id(0)
    x = jnp.where(i < n_p, xp_ref[...], xs_ref[...])
    h = _rmsnorm(x, n_ref[...]).astype(BF16)
    for c in range(D_FF // FFN_CHUNK):
        lo = c * FFN_CHUNK
        gate = _dot(h, wgu_ref[:, lo:lo + FFN_CHUNK])
        up = _dot(h, wgu_ref[:, D_FF + lo:D_FF + lo + FFN_CHUNK])
        act = (_silu(gate) * up).astype(BF16)
        part = _dot(act, wd_ref[lo:lo + FFN_CHUNK, :])
        if c == 0:
            acc_ref[...] = part
        else:
            acc_ref[...] += part
    y = x + 0.5 * acc_ref[...]
    if final:
        y = _rmsnorm(y, nf_ref[...])

    @pl.when(i < n_p)
    def _():
        op_ref[...] = y

    @pl.when(i >= n_p)
    def _():
        os_ref[...] = y


def _ffn(xp2d, xs2d, norm, w_gu, w_down, layer, norm_final=None):
    n_p = xp2d.shape[0] // FFN_TILE
    n_s = xs2d.shape[0] // FFN_TILE
    final = norm_final is not None
    at_layer = lambda i: (layer, 0, 0)
    p_map = lambda i: (jnp.minimum(i, n_p - 1), 0)
    s_map = lambda i: (jnp.maximum(i - n_p, 0), 0)
    in_specs = [
        pl.BlockSpec((FFN_TILE, D_MODEL), p_map),
        pl.BlockSpec((FFN_TILE, D_MODEL), s_map),
        pl.BlockSpec((None, 1, D_MODEL), at_layer),
        pl.BlockSpec((None, D_MODEL, 2 * D_FF), at_layer, pipeline_mode=pl.Buffered(1)),
        pl.BlockSpec((None, D_FF, D_MODEL), at_layer, pipeline_mode=pl.Buffered(1)),
    ]
    args = [xp2d, xs2d, norm, w_gu, w_down]
    if final:
        in_specs.append(pl.BlockSpec((1, D_MODEL), lambda i: (0, 0)))
        args.append(norm_final.reshape(1, D_MODEL))
    return pl.pallas_call(
        functools.partial(_ffn_kernel, final=final, n_p=n_p),
        grid=(n_p + n_s,),
        in_specs=in_specs,
        out_specs=[pl.BlockSpec((FFN_TILE, D_MODEL), p_map), pl.BlockSpec((FFN_TILE, D_MODEL), s_map)],
        out_shape=[jax.ShapeDtypeStruct(xp2d.shape, F32), jax.ShapeDtypeStruct(xs2d.shape, F32)],
        scratch_shapes=[pltpu.VMEM((FFN_TILE, D_MODEL), F32)],
        compiler_params=pltpu.CompilerParams(
            dimension_semantics=("arbitrary",), vmem_limit_bytes=VMEM_LIMIT),
        name="ffn_final" if final else "ffn",
    )(*args)


N_MIX_IN = 12


def _mix_prompt_kernel(*refs, layer):
    (sink_ref, x_ref, nmix_ref, win_ref, wa2_ref, ba_ref, glag_ref, swag_ref,
     wout_ref, cos_ref, sa_ref, sb_ref) = refs[:N_MIX_IN]
    (xo_ref, so_ref, ko_ref, vo_ref,
     s_s, qb_s, qsw_s, kd_s, klt_s, dect_s, v_s, gate_s, og_s, qh_s, kt_s, vt_s, cat_s) = refs[-17:]
    T = MIX_TILE
    C = GLA_CHUNK
    G = 256
    l = pl.program_id(1)
    n_l = pl.num_programs(1)

    @pl.when(l == 0)
    def _():
        s_s[...] = jnp.zeros_like(s_s)
        kt_s[0:WINDOW, :] = jnp.zeros((WINDOW, SWA_KV_W), BF16)
        vt_s[:, 0:WINDOW] = jnp.zeros((SWA_KV_W, WINDOW), BF16)

    x = x_ref[...]
    h = _rmsnorm(x, nmix_ref[...]).astype(BF16)

    a_low = _dot(h, win_ref[:, C_A:C_A + LANES]).astype(BF16)
    z = _dot(a_low, wa2_ref[...]) + ba_ref[...]
    g = _log_sigmoid(z) * (1.0 / GATE_NORM)
    qk = _dot(h, win_ref[:, C_QG:C_QG + 2 * GLA_QK_W])

    ri = lax.broadcasted_iota(jnp.int32, (G, G), 0)
    ci = lax.broadcasted_iota(jnp.int32, (G, G), 1)
    same_chunk = (ri >> 6) == (ci >> 6)
    incl = jnp.where(jnp.logical_and(same_chunk, ci <= ri), 1.0, 0.0).astype(BF16)
    excl = jnp.where(jnp.logical_and(same_chunk, ci > ri), 1.0, 0.0).astype(BF16)
    for r in range(T // G):
        rows = slice(r * G, (r + 1) * G)
        b = _split_dot(incl, g[rows])
        rest = _split_dot(excl, g[rows])
        q = qk[rows, 0:GLA_QK_W]
        k = qk[rows, GLA_QK_W:2 * GLA_QK_W]
        qb = q * QK_SCALE * jnp.exp(b)
        qb_s[rows, :] = qb.astype(BF16)
        qsw_s[rows, :] = jnp.concatenate(
            [pltpu.roll(qb[:, p * LANES:(p + 1) * LANES], GLA_DK, 1) for p in range(GLA_QK_W // LANES)],
            axis=1)
        kd_s[rows, :] = k * jnp.exp(-b)
        klt_s[:, rows] = (k * jnp.exp(rest)).T.astype(BF16)
    rs = lax.broadcasted_iota(jnp.int32, (LANES, T), 0)
    cs = lax.broadcasted_iota(jnp.int32, (LANES, T), 1)
    chunk_sel = jnp.where(rs == (cs >> 6), 1.0, 0.0).astype(BF16)
    dect_s[...] = jnp.exp(_split_dot(chunk_sel, g)).T

    v_s[...] = _dot(h, win_ref[:, C_VG:C_VG + GLA_V_W]).astype(BF16)
    gate_s[...] = _silu(_dot(h, win_ref[:, C_RG:C_RG + GLA_V_W])) * glag_ref[...]

    hk_r = lax.broadcasted_iota(jnp.int32, (GLA_QK_W, GLA_QK_W), 0) >> 6
    hk_c = lax.broadcasted_iota(jnp.int32, (GLA_QK_W, GLA_QK_W), 1) >> 6
    bd_k = hk_r == hk_c
    qi = lax.broadcasted_iota(jnp.int32, (C, GLA_QK_W), 0)
    kj = lax.broadcasted_iota(jnp.int32, (C, GLA_QK_W), 1) & (C - 1)
    causal = kj <= qi
    half_c = lax.broadcasted_iota(jnp.int32, (C, LANES), 1) >> 6
    zeros_v = jnp.zeros((C, GLA_DV), BF16)
    for c in range(T // C):
        rows = slice(c * C, (c + 1) * C)
        kd_bd = jnp.where(bd_k, jnp.concatenate([kd_s[rows, :]] * GLA_HEADS, axis=0), 0.0).astype(BF16)
        att = jnp.where(causal, _dot_nt(qb_s[rows, :], kd_bd), 0.0)
        s16 = s_s[...].astype(BF16)
        dcol = dect_s[:, c:c + 1]
        tok = slice((c // 2) * 2 * C, (c // 2 + 1) * 2 * C)
        for hd in range(GLA_HEADS):
            slab = slice((hd // 2) * LANES, (hd // 2 + 1) * LANES)
            hk = slice(hd * GLA_DK, (hd + 1) * GLA_DK)
            v = v_s[rows, hd * GLA_DV:(hd + 1) * GLA_DV]
            lhs = jnp.where(half_c == hd % 2, att[:, slab], qsw_s[rows, slab]).astype(BF16)
            rhs = jnp.concatenate([v, s16[hk, :]] if hd % 2 == 0 else [s16[hk, :], v], axis=0)
            og_s[rows, hd * GLA_DV:(hd + 1) * GLA_DV] = _dot(lhs, rhs)
            v_pair = jnp.concatenate([v, zeros_v] if c % 2 == 0 else [zeros_v, v], axis=0)
            s_s[hk, :] = dcol[hk, :] * s_s[hk, :] + _dot(klt_s[hk, tok], v_pair)

    for hd in range(GLA_HEADS):
        vs_ = slice(hd * GLA_DV, (hd + 1) * GLA_DV)
        o = og_s[:, vs_]
        o = o * lax.rsqrt(jnp.mean(o * o, axis=-1, keepdims=True) + EPS)
        cat_s[:, vs_] = (o * gate_s[:, vs_]).astype(BF16)

    cos, sa, sb = cos_ref[...], sa_ref[...], sb_ref[...]
    half_t = lax.broadcasted_iota(jnp.int32, (T, LANES), 1) >> 6
    q_rot = _rope(_dot(h, win_ref[:, C_QS:C_QS + SWA_Q_W]), cos, sa, sb) * QK_SCALE
    for hq in range(SWA_HEADS):
        slab = q_rot[:, (hq // 2) * LANES:(hq // 2 + 1) * LANES]
        j = hq // SWA_GROUP
        if hq % 2 != j:
            slab = pltpu.roll(slab, HEAD_DIM, 1)
        qh_s[hq] = jnp.where(half_t == j, slab, 0.0).astype(BF16)
    k_rot = _rope(_dot(h, win_ref[:, C_KS:C_KS + SWA_KV_W]), cos, sa, sb)
    v_swa = _dot(h, win_ref[:, C_VS:C_VS + SWA_KV_W])
    kt_s[WINDOW:WINDOW + T, :] = k_rot.astype(BF16)
    vt_s[:, WINDOW:WINDOW + T] = v_swa.T.astype(BF16)
    ko_ref[...] = k_rot[T - WINDOW:T, :]
    vo_ref[...] = v_swa[T - WINDOW:T, :]

    swag = swag_ref[...]
    GR = SWA_GROUP * WINDOW
    kc_ = lax.broadcasted_iota(jnp.int32, (WINDOW, GR), 0)
    qr = lax.broadcasted_iota(jnp.int32, (WINDOW, GR), 1) & (WINDOW - 1)
    from_prev = kc_ > qr
    prev_bias = jnp.where(l == 0, -jnp.inf, 0.0)
    for i in range(T // WINDOW):
        rows = slice(i * WINDOW, (i + 1) * WINDOW)
        k_blk = kt_s[i * WINDOW:(i + 2) * WINDOW, :]
        vt_blk = vt_s[:, i * WINDOW:(i + 2) * WINDOW]
        pieces = []
        for j in range(SWA_KV_HEADS):
            q4 = jnp.concatenate([qh_s[j * SWA_GROUP + gq, rows, :] for gq in range(SWA_GROUP)], axis=0)
            s2 = _dot_nt(k_blk, q4)
            s_prev = s2[0:WINDOW, :] + prev_bias if i == 0 else s2[0:WINDOW, :]
            s = jnp.where(from_prev, s_prev, s2[WINDOW:2 * WINDOW, :])
            sink = jnp.concatenate(
                [jnp.full((1, WINDOW), sink_ref[layer, j * SWA_GROUP + gq], F32) for gq in range(SWA_GROUP)],
                axis=1)
            m = jnp.maximum(jnp.max(s, axis=0, keepdims=True), sink)
            pr = jnp.exp(s - m)
            den = jnp.sum(pr, axis=0, keepdims=True) + jnp.exp(sink - m)
            p2 = jnp.concatenate([jnp.where(from_prev, pr, 0.0), jnp.where(from_prev, 0.0, pr)], axis=0)
            o_t = _dot(vt_blk, p2.astype(BF16)) * (1.0 / den)
            pieces.extend(o_t[j * HEAD_DIM:(j + 1) * HEAD_DIM, gq * WINDOW:(gq + 1) * WINDOW]
                          for gq in range(SWA_GROUP))
        o_s = jnp.concatenate(pieces, axis=0).T
        cat_s[rows, GLA_V_W:GLA_V_W + SWA_Q_W] = _rmsnorm(o_s, swag).astype(BF16)

    xo_ref[...] = (x + _dot(cat_s[:, 0:GLA_V_W], wout_ref[0:GLA_V_W, :])
                   + _dot(cat_s[:, GLA_V_W:D_MODEL], wout_ref[GLA_V_W:D_MODEL, :]))

    kt_s[0:WINDOW, :] = kt_s[T:T + WINDOW, :]
    vt_s[:, 0:WINDOW] = vt_s[:, T:T + WINDOW]

    @pl.when(l == n_l - 1)
    def _():
        for hd in range(GLA_HEADS):
            so_ref[hd] = s_s[hd * GLA_DK:(hd + 1) * GLA_DK, :]


def _mixer_weight_specs(layer, grid_rank):
    at_layer = (lambda b, l: (layer, 0, 0)) if grid_rank == 2 else (lambda i: (layer, 0, 0))
    return [
        pl.BlockSpec((None, 1, D_MODEL), at_layer),
        pl.BlockSpec((None, D_MODEL, IN_W_PAD), at_layer),
        pl.BlockSpec((None, LANES, GLA_QK_W), at_layer),
        pl.BlockSpec((None, 1, GLA_QK_W), at_layer),
        pl.BlockSpec((None, 1, GLA_V_W), at_layer),
        pl.BlockSpec((None, 1, SWA_Q_W), at_layer),
        pl.BlockSpec((None, D_MODEL, D_MODEL), at_layer),
    ]


def _mix_prompt(x, sinks, mix_w, tabs, layer, depth, prev):
    B, L, _ = x.shape
    T = MIX_TILE
    tab_spec = pl.BlockSpec((T, LANES), lambda b, l: (l, 0))
    n_prev = 0 if prev is None else len(prev)
    return pl.pallas_call(
        functools.partial(_mix_prompt_kernel, layer=layer),
        grid=(B, L // T),
        in_specs=[
            pl.BlockSpec(memory_space=pltpu.SMEM),
            pl.BlockSpec((None, T, D_MODEL), lambda b, l: (b, l, 0)),
            *_mixer_weight_specs(layer, 2),
            tab_spec, tab_spec, tab_spec,
            *([pl.BlockSpec(memory_space=pl.ANY)] * n_prev),
        ],
        out_specs=[
            pl.BlockSpec((None, T, D_MODEL), lambda b, l: (b, l, 0)),
            pl.BlockSpec((None, None, GLA_HEADS, GLA_DK, GLA_DV), lambda b, l: (layer, b, 0, 0, 0)),
            pl.BlockSpec((None, None, WINDOW, SWA_KV_W), lambda b, l: (layer, b, 0, 0)),
            pl.BlockSpec((None, None, WINDOW, SWA_KV_W), lambda b, l: (layer, b, 0, 0)),
        ],
        out_shape=[
            jax.ShapeDtypeStruct((B, L, D_MODEL), F32),
            jax.ShapeDtypeStruct((depth, B, GLA_HEADS, GLA_DK, GLA_DV), F32),
            jax.ShapeDtypeStruct((depth, B, WINDOW, SWA_KV_W), F32),
            jax.ShapeDtypeStruct((depth, B, WINDOW, SWA_KV_W), F32),
        ],
        input_output_aliases={N_MIX_IN + k: 1 + k for k in range(n_prev)},
        scratch_shapes=[
            pltpu.VMEM((GLA_QK_W, GLA_DV), F32),
            pltpu.VMEM((T, GLA_QK_W), BF16),
            pltpu.VMEM((T, GLA_QK_W), F32),
            pltpu.VMEM((T, GLA_QK_W), F32),
            pltpu.VMEM((GLA_QK_W, T), BF16),
            pltpu.VMEM((GLA_QK_W, LANES), F32),
            pltpu.VMEM((T, GLA_V_W), BF16),
            pltpu.VMEM((T, GLA_V_W), F32),
            pltpu.VMEM((T, GLA_V_W), F32),
            pltpu.VMEM((SWA_HEADS, T, LANES), BF16),
            pltpu.VMEM((WINDOW + T, SWA_KV_W), BF16),
            pltpu.VMEM((SWA_KV_W, WINDOW + T), BF16),
            pltpu.VMEM((T, D_MODEL), BF16),
        ],
        compiler_params=pltpu.CompilerParams(
            dimension_semantics=("arbitrary", "arbitrary"), vmem_limit_bytes=VMEM_LIMIT),
        name="mix_prompt",
    )(sinks, x, *mix_w, *tabs, *(prev or ()))


N_SMP_IN = 15


def _mix_sample_kernel(*refs, layer):
    (sink_ref, x_ref, s_ref, kc_ref, vc_ref, nmix_ref, win_ref, wa2_ref, ba_ref,
     glag_ref, swag_ref, wout_ref, cos_ref, sa_ref, sb_ref) = refs[:N_SMP_IN]
    xo_ref, so_ref, ko_ref, vo_ref = refs[-4:]
    R = x_ref.shape[0]
    NS = SMP_SEQS
    Q = R // NS
    q_shift = Q.bit_length() - 1

    x = x_ref[...]
    h = _rmsnorm(x, nmix_ref[...]).astype(BF16)
    proj = _dot(h, win_ref[...])
    z = _dot(proj[:, C_A:C_A + LANES].astype(BF16), wa2_ref[...]) + ba_ref[...]
    g = _log_sigmoid(z) * (1.0 / GATE_NORM)

    ri = lax.broadcasted_iota(jnp.int32, (R, R), 0)
    ci = lax.broadcasted_iota(jnp.int32, (R, R), 1)
    same = (ri >> q_shift) == (ci >> q_shift)
    causal = jnp.logical_and(same, ci <= ri)
    after = jnp.logical_and(same, ci > ri)
    b = _split_dot(jnp.where(causal, 1.0, 0.0).astype(BF16), g)
    rest = _split_dot(jnp.where(after, 1.0, 0.0).astype(BF16), g)

    qg = proj[:, C_QG:C_QG + GLA_QK_W]
    kg = proj[:, C_KG:C_KG + GLA_QK_W]
    qb = qg * QK_SCALE * jnp.exp(b)
    kd = (kg * jnp.exp(-b)).astype(BF16)
    kl_t = (kg * jnp.exp(rest)).T
    g_t = g.T

    own_tok = ((lax.broadcasted_iota(jnp.int32, (NS * GLA_DK, R), 0) >> 6)
               == (lax.broadcasted_iota(jnp.int32, (NS * GLA_DK, R), 1) >> q_shift))
    own_seq = ((lax.broadcasted_iota(jnp.int32, (R, NS * GLA_DK), 1) >> 6)
               == (lax.broadcasted_iota(jnp.int32, (R, NS * GLA_DK), 0) >> q_shift))
    lane_half = lax.broadcasted_iota(jnp.int32, (R, LANES), 1) >> 6
    ones = jnp.ones((R, GLA_DV), BF16)

    glag = glag_ref[...]
    og = []
    for hd in range(GLA_HEADS):
        ks_ = slice(hd * GLA_DK, (hd + 1) * GLA_DK)
        v = proj[:, C_VG + hd * GLA_DV:C_VG + (hd + 1) * GLA_DV].astype(BF16)
        att = jnp.where(causal, _dot_nt(qb[:, ks_].astype(BF16), kd[:, ks_]), 0.0).astype(BF16)
        o = _dot(att, v)
        s_old = s_ref[:, hd].reshape(NS * GLA_DK, GLA_DV)
        slab = qb[:, (hd // 2) * LANES:(hd // 2 + 1) * LANES]
        dup = jnp.where(lane_half == hd % 2, slab, pltpu.roll(slab, HEAD_DIM, 1))
        q_exp = jnp.where(own_seq, jnp.concatenate([dup] * (NS // 2), axis=1), 0.0).astype(BF16)
        o = o + _dot(q_exp, s_old.astype(BF16))
        o = o * lax.rsqrt(jnp.mean(o * o, axis=-1, keepdims=True) + EPS)
        rg = proj[:, C_RG + hd * GLA_DV:C_RG + (hd + 1) * GLA_DV]
        og.append(o * glag[:, hd * GLA_DV:(hd + 1) * GLA_DV] * _silu(rg))
        kl_exp = jnp.where(own_tok, jnp.concatenate([kl_t[ks_, :]] * NS, axis=0), 0.0).astype(BF16)
        g_exp = jnp.where(own_tok, jnp.concatenate([g_t[ks_, :]] * NS, axis=0), 0.0)
        b_last = _split_dot_rhs(g_exp, ones)
        s_new = jnp.exp(b_last) * s_old + _dot(kl_exp, v)
        so_ref[:, hd] = s_new.reshape(NS, GLA_DK, GLA_DV)

    cos, sa, sb = cos_ref[...], sa_ref[...], sb_ref[...]
    q_rot = _rope(proj[:, C_QS:C_QS + SWA_Q_W], cos, sa, sb) * QK_SCALE
    k_rot = _rope(proj[:, C_KS:C_KS + SWA_KV_W], cos, sa, sb)
    v_new = proj[:, C_VS:C_VS + SWA_KV_W]

    W = kc_ref.shape[1]
    SR = SMP_SUB * Q
    GR = SWA_GROUP * SR
    sub_shift = SR.bit_length() - 1
    w_shift = W.bit_length() - 1
    r_c = lax.broadcasted_iota(jnp.int32, (GR, SMP_SUB * W), 0)
    c_c = lax.broadcasted_iota(jnp.int32, (GR, SMP_SUB * W), 1)
    vis_c = jnp.logical_and(((r_c & (SR - 1)) >> q_shift) == (c_c >> w_shift),
                            (c_c & (W - 1)) > (r_c & (Q - 1)))
    r_n = lax.broadcasted_iota(jnp.int32, (GR, SR), 0)
    c_n = lax.broadcasted_iota(jnp.int32, (GR, SR), 1)
    vis_n = jnp.logical_and(((r_n & (SR - 1)) >> q_shift) == (c_n >> q_shift),
                            (c_n & (Q - 1)) <= (r_n & (Q - 1)))
    grp_row = lax.broadcasted_iota(jnp.int32, (GR, 1), 0) >> sub_shift
    half_sr = lax.broadcasted_iota(jnp.int32, (SR, LANES), 1) >> 6
    half_gr = lax.broadcasted_iota(jnp.int32, (GR, LANES), 1) >> 6

    o_rows = []
    for sg in range(NS // SMP_SUB):
        rows = slice(sg * SR, (sg + 1) * SR)
        k_c = kc_ref[sg * SMP_SUB:(sg + 1) * SMP_SUB].reshape(SMP_SUB * W, SWA_KV_W).astype(BF16)
        v_c = vc_ref[sg * SMP_SUB:(sg + 1) * SMP_SUB].reshape(SMP_SUB * W, SWA_KV_W).astype(BF16)
        k_n = k_rot[rows, :].astype(BF16)
        v_n = v_new[rows, :].astype(BF16)
        o_kv = []
        for j in range(SWA_KV_HEADS):
            parts = []
            for gq in range(SWA_GROUP):
                hq = j * SWA_GROUP + gq
                slab = q_rot[rows, (hq // 2) * LANES:(hq // 2 + 1) * LANES]
                if hq % 2 != j:
                    slab = pltpu.roll(slab, HEAD_DIM, 1)
                parts.append(jnp.where(half_sr == j, slab, 0.0))
            q8 = jnp.concatenate(parts, axis=0).astype(BF16)
            s_c = jnp.where(vis_c, _dot_nt(q8, k_c), -jnp.inf)
            s_n = jnp.where(vis_n, _dot_nt(q8, k_n), -jnp.inf)
            sink = jnp.zeros((GR, 1), F32)
            for gq in range(SWA_GROUP):
                sink = jnp.where(grp_row == gq, sink_ref[layer, j * SWA_GROUP + gq], sink)
            m = jnp.maximum(jnp.maximum(jnp.max(s_c, axis=-1, keepdims=True),
                                        jnp.max(s_n, axis=-1, keepdims=True)), sink)
            p_c = jnp.exp(s_c - m)
            p_n = jnp.exp(s_n - m)
            den = (jnp.sum(p_c, axis=-1, keepdims=True) + jnp.sum(p_n, axis=-1, keepdims=True)
                   + jnp.exp(sink - m))
            o = _dot((p_c / den).astype(BF16), v_c) + _dot((p_n / den).astype(BF16), v_n)
            o_kv.append(o)
        slabs = []
        for p in range(SWA_HEADS // 2):
            halves = []
            for hq in (2 * p, 2 * p + 1):
                j, gq = hq // SWA_GROUP, hq % SWA_GROUP
                piece = o_kv[j][gq * SR:(gq + 1) * SR, :]
                if hq % 2 != j:
                    piece = pltpu.roll(piece, HEAD_DIM, 1)
                halves.append(piece)
            slabs.append(jnp.where(half_sr == 0, halves[0], halves[1]))
        o_rows.append(jnp.concatenate(slabs, axis=1))
    o_s = _rmsnorm(jnp.concatenate(o_rows, axis=0), swag_ref[...])

    cat = jnp.concatenate(og + [o_s], axis=1).astype(BF16)
    xo_ref[...] = x + _dot(cat, wout_ref[...])

    ko_ref[:, 0:W - Q, :] = kc_ref[:, Q:W, :]
    vo_ref[:, 0:W - Q, :] = vc_ref[:, Q:W, :]
    for s in range(NS):
        ko_ref[s, W - Q:W, :] = k_rot[s * Q:(s + 1) * Q, :]
        vo_ref[s, W - Q:W, :] = v_new[s * Q:(s + 1) * Q, :]


def _split_dot_rhs(g, m_bf16):
    hi = g.astype(BF16)
    lo = (g - hi.astype(F32)).astype(BF16)
    return _dot(hi, m_bf16) + _dot(lo, m_bf16)


def _mix_sample(x2d, state, kc, vc, sinks, mix_w, tabs, layer, prev):
    depth, NB, W = kc.shape[0], kc.shape[1], kc.shape[2]
    Q = x2d.shape[0] // NB
    R = SMP_SEQS * Q
    tab_spec = pl.BlockSpec((R, LANES), lambda i: (0, 0))
    st_spec = pl.BlockSpec((None, SMP_SEQS, GLA_HEADS, GLA_DK, GLA_DV), lambda i: (layer, i, 0, 0, 0))
    kv_spec = pl.BlockSpec((None, SMP_SEQS, W, SWA_KV_W), lambda i: (layer, i, 0, 0))
    n_prev = 0 if prev is None else len(prev)
    return pl.pallas_call(
        functools.partial(_mix_sample_kernel, layer=layer),
        grid=(NB // SMP_SEQS,),
        in_specs=[
            pl.BlockSpec(memory_space=pltpu.SMEM),
            pl.BlockSpec((R, D_MODEL), lambda i: (i, 0)),
            st_spec, kv_spec, kv_spec,
            *_mixer_weight_specs(layer, 1),
            tab_spec, tab_spec, tab_spec,
            *([pl.BlockSpec(memory_space=pl.ANY)] * n_prev),
        ],
        out_specs=[pl.BlockSpec((R, D_MODEL), lambda i: (i, 0)), st_spec, kv_spec, kv_spec],
        out_shape=[
            jax.ShapeDtypeStruct(x2d.shape, F32),
            jax.ShapeDtypeStruct(state.shape, F32),
            jax.ShapeDtypeStruct(kc.shape, F32),
            jax.ShapeDtypeStruct(vc.shape, F32),
        ],
        input_output_aliases={N_SMP_IN + k: 1 + k for k in range(n_prev)},
        compiler_params=pltpu.CompilerParams(
            dimension_semantics=("arbitrary",), vmem_limit_bytes=VMEM_LIMIT),
        name="mix_sample",
    )(sinks, x2d, state, kc, vc, *mix_w, *tabs, *(prev or ()))


def _rope_tables(pos):
    half = ROT_DIM // 2
    inv = jnp.power(jnp.float32(ROPE_THETA), -jnp.arange(half, dtype=jnp.float32) * (2.0 / ROT_DIM))
    ang = pos.astype(jnp.float32)[:, None] * inv[None, :]
    cos, sin = jnp.cos(ang), jnp.sin(ang)
    n = pos.shape[0]
    pad = jnp.zeros((n, HEAD_DIM - ROT_DIM), F32)
    zero = jnp.zeros((n, half), F32)
    c64 = jnp.concatenate([cos, cos, pad + 1.0], axis=1)
    a64 = jnp.concatenate([zero, sin, pad], axis=1)
    b64 = jnp.concatenate([-sin, zero, pad], axis=1)
    rep = LANES // HEAD_DIM
    return tuple(jnp.tile(t, (1, rep)) for t in (c64, a64, b64))


def kernel(x_prompt, x_sample, state_gla, cache_swa_k, cache_swa_v, norm_ffn1, w_ffn1_gu, w_ffn1_down,
           norm_mix, w_in, w_gate_up, b_gate, gla_norm, attn_sinks, swa_norm, w_out, norm_ffn2,
           w_ffn2_gu, w_ffn2_down, norm_final):
    B, L, _ = x_prompt.shape
    NB, Q, _ = x_sample.shape
    depth = w_in.shape[0]
    W = cache_swa_k.shape[2]

    tabs_p = _rope_tables(jnp.arange(L))
    tabs_s = _rope_tables(jnp.tile(PAST_LEN + jnp.arange(Q), SMP_SEQS))

    gu1, gu2 = _cast_bf16([w_ffn1_gu, w_ffn2_gu], 128)
    dn1, dn2 = _cast_bf16([w_ffn1_down, w_ffn2_down], D_FF // 4)
    (wout,) = _cast_bf16([w_out], D_MODEL // 2)
    win = _cast_w_in(w_in)
    wa2 = jnp.pad(w_gate_up, ((0, 0), (0, LANES - GATE_RANK), (0, 0))).astype(BF16)
    row = lambda p: p.reshape(depth, 1, -1)
    mix_w = (row(norm_mix), win, wa2, row(b_gate), row(gla_norm), row(swa_norm), wout)
    n1, n2 = row(norm_ffn1), row(norm_ffn2)

    xp = x_prompt.reshape(B * L, D_MODEL)
    xs = x_sample.reshape(NB * Q, D_MODEL)
    kc = cache_swa_k.reshape(depth, NB, W, SWA_KV_W)
    vc = cache_swa_v.reshape(depth, NB, W, SWA_KV_W)
    out_p = out_s = None
    for l in range(depth):
        nf = norm_final if l == depth - 1 else None
        xp, xs = _ffn(xp, xs, n1, gu1, dn1, l)
        xp3, *out_p = _mix_prompt(xp.reshape(B, L, D_MODEL), attn_sinks, mix_w, tabs_p, l, depth, out_p)
        xs, *out_s = _mix_sample(xs, state_gla, kc, vc, attn_sinks, mix_w, tabs_s, l, out_s)
        xp, xs = _ffn(xp3.reshape(B * L, D_MODEL), xs, n2, gu2, dn2, l, nf)

    gla_p, k_p, v_p = out_p
    gla_s, k_s, v_s = out_s
    kv5 = lambda t: t.reshape(*t.shape[:3], SWA_KV_HEADS, HEAD_DIM)
    return (xp.reshape(B, L, D_MODEL), xs.reshape(NB, Q, D_MODEL), gla_p, kv5(k_p), kv5(v_p),
            gla_s, kv5(k_s), kv5(v_s))
```

```python
import functools
import itertools

import jax
import jax.numpy as jnp
import numpy as np
from jax import lax
from jax.experimental import pallas as pl
from jax.experimental.pallas import tpu as pltpu

F32 = jnp.float32
BF16 = jnp.bfloat16

D_MODEL = 1024
D_FF = 2816
EPS = 1e-6
GLA_HEADS = 4
GLA_DK = 64
GLA_DV = 128
GLA_QK_W = GLA_HEADS * GLA_DK
GLA_V_W = GLA_HEADS * GLA_DV
GATE_RANK = 16
GATE_NORM = 16.0
GLA_CHUNK = 64
HEAD_DIM = 64
SWA_HEADS = 8
SWA_KV_HEADS = 2
SWA_GROUP = SWA_HEADS // SWA_KV_HEADS
SWA_Q_W = SWA_HEADS * HEAD_DIM
SWA_KV_W = SWA_KV_HEADS * HEAD_DIM
WINDOW = 128
ROT_DIM = 16
ROPE_THETA = 500000.0
PAST_LEN = 16384
QK_SCALE = 0.125

LANES = 128
C_QG, C_KG, C_VG, C_RG = 0, 256, 512, 1024
C_QS, C_KS, C_VS, C_A = 1536, 2048, 2176, 2304
IN_W_PAD = C_A + LANES

FFN_TILE = 512
FFN_CHUNK = 256
MIX_TILE = 512
SMP_SEQS = 32
SMP_SUB = 8
VMEM_LIMIT = 56 * 1024 * 1024

NT_DIMS = (((1,), (1,)), ((), ()))
TN_DIMS = (((0,), (0,)), ((), ()))


def _dot(a, b):
    return jnp.dot(a, b, preferred_element_type=F32)


def _dot_nt(a, b):
    return lax.dot_general(a, b, NT_DIMS, preferred_element_type=F32)


def _dot_tn(a, b):
    return lax.dot_general(a, b, TN_DIMS, preferred_element_type=F32)


def _rmsnorm(x, g):
    return x * lax.rsqrt(jnp.mean(x * x, axis=-1, keepdims=True) + EPS) * g


def _log_sigmoid(z):
    return jnp.minimum(z, 0.0) - jnp.log1p(jnp.exp(-jnp.abs(z)))


def _silu(z):
    return z * jax.nn.sigmoid(z)


def _split_dot(m_bf16, g):
    hi = g.astype(BF16)
    lo = (g - hi.astype(F32)).astype(BF16)
    return _dot(m_bf16, hi) + _dot(m_bf16, lo)


def _rope(x, cos, sa, sb):
    half = ROT_DIM // 2
    outs = []
    for p in range(x.shape[1] // LANES):
        s = x[:, p * LANES:(p + 1) * LANES]
        outs.append(s * cos + pltpu.roll(s, half, 1) * sa + pltpu.roll(s, LANES - half, 1) * sb)
    return outs[0] if len(outs) == 1 else jnp.concatenate(outs, axis=1)


def _cast_kernel(*refs):
    n = len(refs) // 2
    for src, dst in zip(refs[:n], refs[n:]):
        dst[...] = src[...].astype(BF16)


def _cast_bf16(arrs, block_rows):
    depth, rows, cols = arrs[0].shape
    spec = pl.BlockSpec((None, block_rows, cols), lambda l, r: (l, r, 0))
    n = len(arrs)
    return pl.pallas_call(
        _cast_kernel,
        grid=(depth, rows // block_rows),
        in_specs=[spec] * n,
        out_specs=[spec] * n,
        out_shape=[jax.ShapeDtypeStruct(a.shape, BF16) for a in arrs],
        compiler_params=pltpu.CompilerParams(
            dimension_semantics=("arbitrary", "arbitrary"), vmem_limit_bytes=VMEM_LIMIT),
        name="cast_weights",
    )(*arrs)


def _cast_w_in_kernel(w_ref, o_ref):
    n_a = C_RG + GLA_V_W
    w = w_ref[...]
    o_ref[:, 0:n_a] = w[:, 0:n_a].astype(BF16)
    o_ref[:, n_a:C_A] = w[:, n_a + GATE_RANK:].astype(BF16)
    pad = jnp.zeros((w.shape[0], LANES - GATE_RANK), F32)
    o_ref[:, C_A:IN_W_PAD] = jnp.concatenate([w[:, n_a:n_a + GATE_RANK], pad], axis=1).astype(BF16)


def _cast_w_in(w_in):
    depth, rows, cols = w_in.shape
    block_rows = 256
    return pl.pallas_call(
        _cast_w_in_kernel,
        grid=(depth, rows // block_rows),
        in_specs=[pl.BlockSpec((None, block_rows, cols), lambda l, r: (l, r, 0))],
        out_specs=pl.BlockSpec((None, block_rows, IN_W_PAD), lambda l, r: (l, r, 0)),
        out_shape=jax.ShapeDtypeStruct((depth, rows, IN_W_PAD), BF16),
        compiler_params=pltpu.CompilerParams(
            dimension_semantics=("arbitrary", "arbitrary"), vmem_limit_bytes=VMEM_LIMIT),
        name="cast_w_in",
    )(w_in)


def _ffn_kernel(*refs, final, n_tiles):
    if final:
        xn_ref, xv_ref, n_ref, wgu_ref, wd_ref, nf_ref, o_ref, acc_ref, h_ref = refs
    else:
        xn_ref, xv_ref, n_ref, wgu_ref, wd_ref, o_ref, acc_ref, h_ref = refs
    i = pl.program_id(0)

    def epilogue():
        y = xv_ref[...] + 0.5 * acc_ref[...]
        if final:
            y = _rmsnorm(y, nf_ref[...])
        o_ref[...] = y

    def step(cur, nxt):
        epilogue()
        h_ref[nxt] = _rmsnorm(xn_ref[...], n_ref[...]).astype(BF16)
        h = h_ref[cur]
        for c in range(D_FF // FFN_CHUNK):
            lo = c * FFN_CHUNK
            gate = _dot(h, wgu_ref[:, lo:lo + FFN_CHUNK])
            up = _dot(h, wgu_ref[:, D_FF + lo:D_FF + lo + FFN_CHUNK])
            act = (_silu(gate) * up).astype(BF16)
            part = _dot(act, wd_ref[lo:lo + FFN_CHUNK, :])
            if c == 0:
                acc_ref[...] = part
            else:
                acc_ref[...] += part

    @pl.when(i == 0)
    def _():
        acc_ref[...] = jnp.zeros_like(acc_ref)
        h_ref[0] = _rmsnorm(xv_ref[...], n_ref[...]).astype(BF16)

    @pl.when(jnp.logical_and(i < n_tiles, i % 2 == 0))
    def _():
        step(0, 1)

    @pl.when(jnp.logical_and(i < n_tiles, i % 2 == 1))
    def _():
        step(1, 0)

    @pl.when(i == n_tiles)
    def _():
        epilogue()


def _ffn(x2d, norm, w_gu, w_down, layer, norm_final=None):
    n_tiles = x2d.shape[0] // FFN_TILE
    final = norm_final is not None
    at_layer = lambda i: (layer, 0, 0)
    nxt_map = lambda i: (jnp.minimum(i + 1, n_tiles - 1), 0)
    prv_map = lambda i: (jnp.maximum(i - 1, 0), 0)
    in_specs = [
        pl.BlockSpec((FFN_TILE, D_MODEL), nxt_map),
        pl.BlockSpec((FFN_TILE, D_MODEL), prv_map),
        pl.BlockSpec((None, 1, D_MODEL), at_layer),
        pl.BlockSpec((None, D_MODEL, 2 * D_FF), at_layer, pipeline_mode=pl.Buffered(1)),
        pl.BlockSpec((None, D_FF, D_MODEL), at_layer, pipeline_mode=pl.Buffered(1)),
    ]
    args = [x2d, x2d, norm, w_gu, w_down]
    if final:
        in_specs.append(pl.BlockSpec((1, D_MODEL), lambda i: (0, 0)))
        args.append(norm_final.reshape(1, D_MODEL))
    return pl.pallas_call(
        functools.partial(_ffn_kernel, final=final, n_tiles=n_tiles),
        grid=(n_tiles + 1,),
        in_specs=in_specs,
        out_specs=pl.BlockSpec((FFN_TILE, D_MODEL), prv_map),
        out_shape=jax.ShapeDtypeStruct(x2d.shape, F32),
        scratch_shapes=[pltpu.VMEM((FFN_TILE, D_MODEL), F32),
                        pltpu.VMEM((2, FFN_TILE, D_MODEL), BF16)],
        compiler_params=pltpu.CompilerParams(
            dimension_semantics=("arbitrary",), vmem_limit_bytes=VMEM_LIMIT),
        name="ffn_final" if final else "ffn",
    )(*args)


N_MIX_IN = 13


def _mix_project(slot, x_ref, nmix_ref, win_ref, wa2_ref, ba_ref, glag_ref, cos_ref, sa_ref, sb_ref,
                 qb_s, qsw_s, kd_s, klt_s, dect_s, v_s, gate_s, qh_s, kt_s, vt_s, kout_s, vout_s):
    T = MIX_TILE
    G = 256
    qb_s, qsw_s, kd_s, klt_s, dect_s, v_s, gate_s, qh_s, kt_s, vt_s, kout_s, vout_s = (
        r.at[slot] for r in (qb_s, qsw_s, kd_s, klt_s, dect_s, v_s, gate_s, qh_s, kt_s, vt_s, kout_s, vout_s))

    x = x_ref[...]
    h = _rmsnorm(x, nmix_ref[...]).astype(BF16)
    yield

    a_low = _dot(h, win_ref[:, C_A:C_A + LANES]).astype(BF16)
    z = _dot(a_low, wa2_ref[...]) + ba_ref[...]
    g = _log_sigmoid(z) * (1.0 / GATE_NORM)
    yield
    qk = _dot(h, win_ref[:, C_QG:C_QG + 2 * GLA_QK_W])
    yield

    ri = lax.broadcasted_iota(jnp.int32, (G, G), 0)
    ci = lax.broadcasted_iota(jnp.int32, (G, G), 1)
    same_chunk = (ri >> 6) == (ci >> 6)
    incl = jnp.where(jnp.logical_and(same_chunk, ci <= ri), 1.0, 0.0).astype(BF16)
    C = GLA_CHUNK
    chunk_last = []
    for r in range(T // G):
        rows = slice(r * G, (r + 1) * G)
        b = _split_dot(incl, g[rows])
        lasts = [b[c * C + C - 1:(c + 1) * C, :] for c in range(G // C)]
        chunk_last.extend(lasts)
        rest = jnp.concatenate([jnp.broadcast_to(t, (C, GLA_QK_W)) for t in lasts], axis=0) - b
        q = qk[rows, 0:GLA_QK_W]
        k = qk[rows, GLA_QK_W:2 * GLA_QK_W]
        qb = q * QK_SCALE * jnp.exp(b)
        qb_s[rows, :] = qb.astype(BF16)
        qsw_s[rows, :] = jnp.concatenate(
            [pltpu.roll(qb[:, p * LANES:(p + 1) * LANES], GLA_DK, 1) for p in range(GLA_QK_W // LANES)],
            axis=1)
        kd_s[rows, :] = k * jnp.exp(-b)
        klt_s[:, rows] = (k * jnp.exp(rest)).T.astype(BF16)
        yield
    pad = jnp.zeros((LANES - len(chunk_last), GLA_QK_W), F32)
    dect_s[...] = jnp.exp(jnp.concatenate(chunk_last + [pad], axis=0)).T

    v_s[...] = _dot(h, win_ref[:, C_VG:C_VG + GLA_V_W]).astype(BF16)
    yield
    gate_s[...] = _silu(_dot(h, win_ref[:, C_RG:C_RG + GLA_V_W])) * glag_ref[...]
    yield

    cos, sa, sb = cos_ref[...], sa_ref[...], sb_ref[...]
    half_t = lax.broadcasted_iota(jnp.int32, (T, LANES), 1) >> 6
    q_rot = _rope(_dot(h, win_ref[:, C_QS:C_QS + SWA_Q_W]), cos, sa, sb) * QK_SCALE
    for hq in range(SWA_HEADS):
        slab = q_rot[:, (hq // 2) * LANES:(hq // 2 + 1) * LANES]
        j = hq // SWA_GROUP
        if hq % 2 != j:
            slab = pltpu.roll(slab, HEAD_DIM, 1)
        qh_s[hq] = jnp.where(half_t == j, slab, 0.0).astype(BF16)
        if hq % 4 == 3:
            yield
    k_rot =_rope(_dot(h, win_ref[:, C_KS:C_KS + SWA_KV_W]), cos, sa, sb)
    v_swa = _dot(h, win_ref[:, C_VS:C_VS + SWA_KV_W])
    kt_s[...] = k_rot.astype(BF16)
    vt_s[...] = v_swa.T.astype(BF16)
    kout_s[...] = k_rot[T - WINDOW:T, :]
    vout_s[...] = v_swa[T - WINDOW:T, :]


def _mix_sequential(slot, first, layer, sink_ref, x_ref, swag_ref, wout_ref, xo_ref,
                    qb_s, qsw_s, kd_s, klt_s, dect_s, v_s, gate_s, qh_s, kt_s, vt_s,
                    s_s, og_s, cat_s, kcar_s, vcar_s):
    T = MIX_TILE
    C = GLA_CHUNK
    qb_s, qsw_s, kd_s, klt_s, dect_s, v_s, gate_s, qh_s, kt_s, vt_s = (
        r.at[slot] for r in (qb_s, qsw_s, kd_s, klt_s, dect_s, v_s, gate_s, qh_s, kt_s, vt_s))

    hk_r = lax.broadcasted_iota(jnp.int32, (GLA_QK_W, GLA_QK_W), 0) >> 6
    hk_c = lax.broadcasted_iota(jnp.int32, (GLA_QK_W, GLA_QK_W), 1) >> 6
    bd_k = hk_r == hk_c
    qi = lax.broadcasted_iota(jnp.int32, (C, GLA_QK_W), 0)
    kj = lax.broadcasted_iota(jnp.int32, (C, GLA_QK_W), 1) & (C - 1)
    causal = kj <= qi
    half_c = lax.broadcasted_iota(jnp.int32, (C, LANES), 1) >> 6
    zeros_v = jnp.zeros((C, GLA_DV), BF16)
    for c in range(T // C):
        rows = slice(c * C, (c + 1) * C)
        kd_bd = jnp.where(bd_k, jnp.concatenate([kd_s[rows, :]] * GLA_HEADS, axis=0), 0.0).astype(BF16)
        att = jnp.where(causal, _dot_nt(qb_s[rows, :], kd_bd), 0.0)
        s16 = s_s[...].astype(BF16)
        dcol = dect_s[:, c:c + 1]
        tok = slice((c // 2) * 2 * C, (c // 2 + 1) * 2 * C)
        for hd in range(GLA_HEADS):
            slab = slice((hd // 2) * LANES, (hd // 2 + 1) * LANES)
            hk = slice(hd * GLA_DK, (hd + 1) * GLA_DK)
            v = v_s[rows, hd * GLA_DV:(hd + 1) * GLA_DV]
            lhs = jnp.where(half_c == hd % 2, att[:, slab], qsw_s[rows, slab]).astype(BF16)
            rhs = jnp.concatenate([v, s16[hk, :]] if hd % 2 == 0 else [s16[hk, :], v], axis=0)
            og_s[rows, hd * GLA_DV:(hd + 1) * GLA_DV] = _dot(lhs, rhs)
            v_pair = jnp.concatenate([v, zeros_v] if c % 2 == 0 else [zeros_v, v], axis=0)
            s_s[hk, :] = dcol[hk, :] * s_s[hk, :] + _dot(klt_s[hk, tok], v_pair)
        yield

    for hd in range(GLA_HEADS):
        vs_ = slice(hd * GLA_DV, (hd + 1) * GLA_DV)
        o = og_s[:, vs_]
        o = o * lax.rsqrt(jnp.mean(o * o, axis=-1, keepdims=True) + EPS)
        cat_s[:, vs_] = (o * gate_s[:, vs_]).astype(BF16)
    yield

    swag = swag_ref[...]
    GR = SWA_GROUP * WINDOW
    kc_ = lax.broadcasted_iota(jnp.int32, (WINDOW, GR), 0)
    qr = lax.broadcasted_iota(jnp.int32, (WINDOW, GR), 1) & (WINDOW - 1)
    from_prev = kc_ > qr
    prev_bias = jnp.where(first > 0.0, -jnp.inf, 0.0)
    for i in range(T // WINDOW):
        rows = slice(i * WINDOW, (i + 1) * WINDOW)
        if i == 0:
            k_blk = jnp.concatenate([kcar_s[...], kt_s[0:WINDOW, :]], axis=0)
            vt_blk = jnp.concatenate([vcar_s[...], vt_s[:, 0:WINDOW]], axis=1)
        else:
            k_blk = kt_s[(i - 1) * WINDOW:(i + 1) * WINDOW, :]
            vt_blk = vt_s[:, (i - 1) * WINDOW:(i + 1) * WINDOW]
        pieces = []
        for j in range(SWA_KV_HEADS):
            q4 = jnp.concatenate([qh_s[j * SWA_GROUP + gq, rows, :] for gq in range(SWA_GROUP)], axis=0)
            s2 = _dot_nt(k_blk, q4)
            s_prev = s2[0:WINDOW, :] + prev_bias if i == 0 else s2[0:WINDOW, :]
            s = jnp.where(from_prev, s_prev, s2[WINDOW:2 * WINDOW, :])
            sink = jnp.concatenate(
                [jnp.full((1, WINDOW), sink_ref[layer, j * SWA_GROUP + gq], F32) for gq in range(SWA_GROUP)],
                axis=1)
            m = jnp.maximum(jnp.max(s, axis=0, keepdims=True), sink)
            pr = jnp.exp(s - m)
            den = jnp.sum(pr, axis=0, keepdims=True) + jnp.exp(sink - m)
            p2 = jnp.concatenate([jnp.where(from_prev, pr, 0.0), jnp.where(from_prev, 0.0, pr)], axis=0)
            o_t = _dot(vt_blk, p2.astype(BF16)) * (1.0 / den)
            pieces.extend(o_t[j * HEAD_DIM:(j + 1) * HEAD_DIM, gq * WINDOW:(gq + 1) * WINDOW]
                          for gq in range(SWA_GROUP))
        o_s = jnp.concatenate(pieces, axis=0).T
        cat_s[rows, GLA_V_W:GLA_V_W + SWA_Q_W] = _rmsnorm(o_s, swag).astype(BF16)
        yield

    xo_ref[...] = (x_ref[...] + _dot(cat_s[:, 0:GLA_V_W], wout_ref[0:GLA_V_W, :])
                   + _dot(cat_s[:, GLA_V_W:D_MODEL], wout_ref[GLA_V_W:D_MODEL, :]))

    kcar_s[...] = kt_s[T - WINDOW:T, :]
    vcar_s[...] = vt_s[:, T - WINDOW:T]


def _mix_prompt_kernel(*refs, layer, n_l, n_tiles):
    (sink_ref, xn_ref, xv_ref, nmix_ref, win_ref, wa2_ref, ba_ref, glag_ref, swag_ref,
     wout_ref, cos_ref, sa_ref, sb_ref) = refs[:N_MIX_IN]
    (xo_ref, so_ref, ko_ref, vo_ref,
     qb_s, qsw_s, kd_s, klt_s, dect_s, v_s, gate_s, qh_s, kt_s, vt_s, kout_s, vout_s,
     s_s, og_s, cat_s, kcar_s, vcar_s) = refs[-21:]
    handover = (qb_s, qsw_s, kd_s, klt_s, dect_s, v_s, gate_s, qh_s, kt_s, vt_s)
    t = pl.program_id(0)
    l_seq = jnp.maximum(t - 1, 0) % n_l

    @pl.when(t == 0)
    def _():
        for r in handover:
            r[1] = jnp.zeros(r.shape[1:], r.dtype)
        kcar_s[...] = jnp.zeros_like(kcar_s)
        vcar_s[...] = jnp.zeros_like(vcar_s)

    @pl.when(l_seq == 0)
    def _():
        s_s[...] = jnp.zeros_like(s_s)

    first = jnp.where(l_seq == 0, 1.0, 0.0)

    def step(slot_proj, slot_seq):
        proj = _mix_project(slot_proj, xn_ref, nmix_ref, win_ref, wa2_ref, ba_ref, glag_ref, cos_ref,
                            sa_ref, sb_ref, *handover, kout_s, vout_s)
        seq = _mix_sequential(slot_seq, first, layer, sink_ref, xv_ref, swag_ref, wout_ref, xo_ref,
                              *handover, s_s, og_s, cat_s, kcar_s, vcar_s)
        for _ in itertools.zip_longest(seq, proj):
            pass

        @pl.when(jnp.logical_and(t > 0, l_seq == n_l - 1))
        def _():
            for hd in range(GLA_HEADS):
                so_ref[hd] = s_s[hd * GLA_DK:(hd + 1) * GLA_DK, :]
            ko_ref[...] = kout_s[slot_seq]
            vo_ref[...] = vout_s[slot_seq]

    @pl.when(t % 2 == 0)
    def _():
        step(0, 1)

    @pl.when(t % 2 == 1)
    def _():
        step(1, 0)


def _mixer_weight_specs(layer, grid_rank):
    at_layer = (lambda b, l: (layer, 0, 0)) if grid_rank == 2 else (lambda i: (layer, 0, 0))
    return [
        pl.BlockSpec((None, 1, D_MODEL), at_layer),
        pl.BlockSpec((None, D_MODEL, IN_W_PAD), at_layer),
        pl.BlockSpec((None, LANES, GLA_QK_W), at_layer),
        pl.BlockSpec((None, 1, GLA_QK_W), at_layer),
        pl.BlockSpec((None, 1, GLA_V_W), at_layer),
        pl.BlockSpec((None, 1, SWA_Q_W), at_layer),
        pl.BlockSpec((None, D_MODEL, D_MODEL), at_layer),
    ]


def _mix_prompt(x, sinks, mix_w, tabs, layer, depth, prev):
    B, L, _ = x.shape
    T = MIX_TILE
    n_l = L // T
    n_tiles = B * n_l
    t_proj = lambda t: jnp.minimum(t, n_tiles - 1)
    t_seq = lambda t: jnp.maximum(t - 1, 0)
    tab_spec = pl.BlockSpec((T, LANES), lambda t: (t_proj(t) % n_l, 0))
    n_prev = 0 if prev is None else len(prev)
    return pl.pallas_call(
        functools.partial(_mix_prompt_kernel, layer=layer, n_l=n_l, n_tiles=n_tiles),
        grid=(n_tiles + 1,),
        in_specs=[
            pl.BlockSpec(memory_space=pltpu.SMEM),
            pl.BlockSpec((None, T, D_MODEL), lambda t: (t_proj(t) // n_l, t_proj(t) % n_l, 0)),
            pl.BlockSpec((None, T, D_MODEL), lambda t: (t_seq(t) // n_l, t_seq(t) % n_l, 0)),
            *_mixer_weight_specs(layer, 1),
            tab_spec, tab_spec, tab_spec,
            *([pl.BlockSpec(memory_space=pl.ANY)] * n_prev),
        ],
        out_specs=[
            pl.BlockSpec((None, T, D_MODEL), lambda t: (t_seq(t) // n_l, t_seq(t) % n_l, 0)),
            pl.BlockSpec((None, None, GLA_HEADS, GLA_DK, GLA_DV), lambda t: (layer, t_seq(t) // n_l, 0, 0, 0)),
            pl.BlockSpec((None, None, WINDOW, SWA_KV_W), lambda t: (layer, t_seq(t) // n_l, 0, 0)),
            pl.BlockSpec((None, None, WINDOW, SWA_KV_W), lambda t: (layer, t_seq(t) // n_l, 0, 0)),
        ],
        out_shape=[
            jax.ShapeDtypeStruct((B, L, D_MODEL), F32),
            jax.ShapeDtypeStruct((depth, B, GLA_HEADS, GLA_DK, GLA_DV), F32),
            jax.ShapeDtypeStruct((depth, B, WINDOW, SWA_KV_W), F32),
            jax.ShapeDtypeStruct((depth, B, WINDOW, SWA_KV_W), F32),
        ],
        input_output_aliases={N_MIX_IN + k: 1 + k for k in range(n_prev)},
        scratch_shapes=[
            pltpu.VMEM((2, T, GLA_QK_W), BF16),
            pltpu.VMEM((2, T, GLA_QK_W), F32),
            pltpu.VMEM((2, T, GLA_QK_W), F32),
            pltpu.VMEM((2, GLA_QK_W, T), BF16),
            pltpu.VMEM((2, GLA_QK_W, LANES), F32),
            pltpu.VMEM((2, T, GLA_V_W), BF16),
            pltpu.VMEM((2, T, GLA_V_W), F32),
            pltpu.VMEM((2, SWA_HEADS, T, LANES), BF16),
            pltpu.VMEM((2, T, SWA_KV_W), BF16),
            pltpu.VMEM((2, SWA_KV_W, T), BF16),
            pltpu.VMEM((2, WINDOW, SWA_KV_W), F32),
            pltpu.VMEM((2, WINDOW, SWA_KV_W), F32),
            pltpu.VMEM((GLA_QK_W, GLA_DV), F32),
            pltpu.VMEM((T, GLA_V_W), F32),
            pltpu.VMEM((T, D_MODEL), BF16),
            pltpu.VMEM((WINDOW, SWA_KV_W), BF16),
            pltpu.VMEM((SWA_KV_W, WINDOW), BF16),
        ],
        compiler_params=pltpu.CompilerParams(
            dimension_semantics=("arbitrary",), vmem_limit_bytes=VMEM_LIMIT),
        name="mix_prompt",
    )(sinks, x, x, *mix_w, *tabs, *(prev or ()))


N_SMP_IN = 15


def _mix_sample_kernel(*refs, layer):
    (sink_ref, x_ref, s_ref, kc_ref, vc_ref, nmix_ref, win_ref, wa2_ref, ba_ref,
     glag_ref, swag_ref, wout_ref, cos_ref, sa_ref, sb_ref) = refs[:N_SMP_IN]
    xo_ref, so_ref, ko_ref, vo_ref = refs[-4:]
    R = x_ref.shape[0]
    NS = SMP_SEQS
    Q = R // NS
    q_shift = Q.bit_length() - 1

    x = x_ref[...]
    h = _rmsnorm(x, nmix_ref[...]).astype(BF16)
    proj = _dot(h, win_ref[...])
    z = _dot(proj[:, C_A:C_A + LANES].astype(BF16), wa2_ref[...]) + ba_ref[...]
    g = _log_sigmoid(z) * (1.0 / GATE_NORM)

    ri = lax.broadcasted_iota(jnp.int32, (R, R), 0)
    ci = lax.broadcasted_iota(jnp.int32, (R, R), 1)
    same = (ri >> q_shift) == (ci >> q_shift)
    causal = jnp.logical_and(same, ci <= ri)
    after = jnp.logical_and(same, ci > ri)
    b = _split_dot(jnp.where(causal, 1.0, 0.0).astype(BF16), g)
    rest = _split_dot(jnp.where(after, 1.0, 0.0).astype(BF16), g)

    qg = proj[:, C_QG:C_QG + GLA_QK_W]
    kg = proj[:, C_KG:C_KG + GLA_QK_W]
    qb = qg * QK_SCALE * jnp.exp(b)
    kd = (kg * jnp.exp(-b)).astype(BF16)
    kl_t = (kg * jnp.exp(rest)).T
    g_t = g.T

    own_tok = ((lax.broadcasted_iota(jnp.int32, (NS * GLA_DK, R), 0) >> 6)
               == (lax.broadcasted_iota(jnp.int32, (NS * GLA_DK, R), 1) >> q_shift))
    own_seq = ((lax.broadcasted_iota(jnp.int32, (R, NS * GLA_DK), 1) >> 6)
               == (lax.broadcasted_iota(jnp.int32, (R, NS * GLA_DK), 0) >> q_shift))
    lane_half = lax.broadcasted_iota(jnp.int32, (R, LANES), 1) >> 6
    ones = jnp.ones((R, GLA_DV), BF16)

    glag = glag_ref[...]
    og = []
    for hd in range(GLA_HEADS):
        ks_ = slice(hd * GLA_DK, (hd + 1) * GLA_DK)
        v = proj[:, C_VG + hd * GLA_DV:C_VG + (hd + 1) * GLA_DV].astype(BF16)
        att = jnp.where(causal, _dot_nt(qb[:, ks_].astype(BF16), kd[:, ks_]), 0.0).astype(BF16)
        o = _dot(att, v)
        s_old = s_ref[:, hd].reshape(NS * GLA_DK, GLA_DV)
        slab = qb[:, (hd // 2) * LANES:(hd // 2 + 1) * LANES]
        dup = jnp.where(lane_half == hd % 2, slab, pltpu.roll(slab, HEAD_DIM, 1))
        q_exp = jnp.where(own_seq, jnp.concatenate([dup] * (NS // 2), axis=1), 0.0).astype(BF16)
        o = o + _dot(q_exp, s_old.astype(BF16))
        o = o * lax.rsqrt(jnp.mean(o * o, axis=-1, keepdims=True) + EPS)
        rg = proj[:, C_RG + hd * GLA_DV:C_RG + (hd + 1) * GLA_DV]
        og.append(o * glag[:, hd * GLA_DV:(hd + 1) * GLA_DV] * _silu(rg))
        kl_exp = jnp.where(own_tok, jnp.concatenate([kl_t[ks_, :]] * NS, axis=0), 0.0).astype(BF16)
        g_exp = jnp.where(own_tok, jnp.concatenate([g_t[ks_, :]] * NS, axis=0), 0.0)
        b_last = _split_dot_rhs(g_exp, ones)
        s_new = jnp.exp(b_last) * s_old + _dot(kl_exp, v)
        so_ref[:, hd] = s_new.reshape(NS, GLA_DK, GLA_DV)

    cos, sa, sb = cos_ref[...], sa_ref[...], sb_ref[...]
    q_rot = _rope(proj[:, C_QS:C_QS + SWA_Q_W], cos, sa, sb) * QK_SCALE
    k_rot = _rope(proj[:, C_KS:C_KS + SWA_KV_W], cos, sa, sb)
    v_new = proj[:, C_VS:C_VS + SWA_KV_W]

    W = kc_ref.shape[1]
    SR = SMP_SUB * Q
    GR = SWA_GROUP * SR
    sub_shift = SR.bit_length() - 1
    w_shift = W.bit_length() - 1
    r_c = lax.broadcasted_iota(jnp.int32, (GR, SMP_SUB * W), 0)
    c_c = lax.broadcasted_iota(jnp.int32, (GR, SMP_SUB * W), 1)
    vis_c = jnp.logical_and(((r_c & (SR - 1)) >> q_shift) == (c_c >> w_shift),
                            (c_c & (W - 1)) > (r_c & (Q - 1)))
    r_n = lax.broadcasted_iota(jnp.int32, (GR, SR), 0)
    c_n = lax.broadcasted_iota(jnp.int32, (GR, SR), 1)
    vis_n = jnp.logical_and(((r_n & (SR - 1)) >> q_shift) == (c_n >> q_shift),
                            (c_n & (Q - 1)) <= (r_n & (Q - 1)))
    grp_row = lax.broadcasted_iota(jnp.int32, (GR, 1), 0) >> sub_shift
    half_sr = lax.broadcasted_iota(jnp.int32, (SR, LANES), 1) >> 6
    half_gr = lax.broadcasted_iota(jnp.int32, (GR, LANES), 1) >> 6

    o_rows = []
    for sg in range(NS // SMP_SUB):
        rows = slice(sg * SR, (sg + 1) * SR)
        k_c = kc_ref[sg * SMP_SUB:(sg + 1) * SMP_SUB].reshape(SMP_SUB * W, SWA_KV_W).astype(BF16)
        v_c = vc_ref[sg * SMP_SUB:(sg + 1) * SMP_SUB].reshape(SMP_SUB * W, SWA_KV_W).astype(BF16)
        k_n = k_rot[rows, :].astype(BF16)
        v_n = v_new[rows, :].astype(BF16)
        o_kv = []
        for j in range(SWA_KV_HEADS):
            parts = []
            for gq in range(SWA_GROUP):
                hq = j * SWA_GROUP + gq
                slab = q_rot[rows, (hq // 2) * LANES:(hq // 2 + 1) * LANES]
                if hq % 2 != j:
                    slab = pltpu.roll(slab, HEAD_DIM, 1)
                parts.append(jnp.where(half_sr == j, slab, 0.0))
            q8 = jnp.concatenate(parts, axis=0).astype(BF16)
            s_c = jnp.where(vis_c, _dot_nt(q8, k_c), -jnp.inf)
            s_n = jnp.where(vis_n, _dot_nt(q8, k_n), -jnp.inf)
            sink = jnp.zeros((GR, 1), F32)
            for gq in range(SWA_GROUP):
                sink = jnp.where(grp_row == gq, sink_ref[layer, j * SWA_GROUP + gq], sink)
            m = jnp.maximum(jnp.maximum(jnp.max(s_c, axis=-1, keepdims=True),
                                        jnp.max(s_n, axis=-1, keepdims=True)), sink)
            p_c = jnp.exp(s_c - m)
            p_n = jnp.exp(s_n - m)
            den = (jnp.sum(p_c, axis=-1, keepdims=True) + jnp.sum(p_n, axis=-1, keepdims=True)
                   + jnp.exp(sink - m))
            o = _dot((p_c / den).astype(BF16), v_c) + _dot((p_n / den).astype(BF16), v_n)
            o_kv.append(o)
        slabs = []
        for p in range(SWA_HEADS // 2):
            halves = []
            for hq in (2 * p, 2 * p + 1):
                j, gq = hq // SWA_GROUP, hq % SWA_GROUP
                piece = o_kv[j][gq * SR:(gq + 1) * SR, :]
                if hq % 2 != j:
                    piece = pltpu.roll(piece, HEAD_DIM, 1)
                halves.append(piece)
            slabs.append(jnp.where(half_sr == 0, halves[0], halves[1]))
        o_rows.append(jnp.concatenate(slabs, axis=1))
    o_s = _rmsnorm(jnp.concatenate(o_rows, axis=0), swag_ref[...])

    cat = jnp.concatenate(og + [o_s], axis=1).astype(BF16)
    xo_ref[...] = x + _dot(cat, wout_ref[...])

    ko_ref[:, 0:W - Q, :] = kc_ref[:, Q:W, :]
    vo_ref[:, 0:W - Q, :] = vc_ref[:, Q:W, :]
    for s in range(NS):
        ko_ref[s, W - Q:W, :] = k_rot[s * Q:(s + 1) * Q, :]
        vo_ref[s, W - Q:W, :] = v_new[s * Q:(s + 1) * Q, :]


def _split_dot_rhs(g, m_bf16):
    hi = g.astype(BF16)
    lo = (g - hi.astype(F32)).astype(BF16)
    return _dot(hi, m_bf16) + _dot(lo, m_bf16)


def _mix_sample(x2d, state, kc, vc, sinks, mix_w, tabs, layer, prev):
    depth, NB, W = kc.shape[0], kc.shape[1], kc.shape[2]
    Q = x2d.shape[0] // NB
    R = SMP_SEQS * Q
    tab_spec = pl.BlockSpec((R, LANES), lambda i: (0, 0))
    st_spec = pl.BlockSpec((None, SMP_SEQS, GLA_HEADS, GLA_DK, GLA_DV), lambda i: (layer, i, 0, 0, 0))
    kv_spec = pl.BlockSpec((None, SMP_SEQS, W, SWA_KV_W), lambda i: (layer, i, 0, 0))
    n_prev = 0 if prev is None else len(prev)
    return pl.pallas_call(
        functools.partial(_mix_sample_kernel, layer=layer),
        grid=(NB // SMP_SEQS,),
        in_specs=[
            pl.BlockSpec(memory_space=pltpu.SMEM),
            pl.BlockSpec((R, D_MODEL), lambda i: (i, 0)),
            st_spec, kv_spec, kv_spec,
            *_mixer_weight_specs(layer, 1),
            tab_spec, tab_spec, tab_spec,
            *([pl.BlockSpec(memory_space=pl.ANY)] * n_prev),
        ],
        out_specs=[pl.BlockSpec((R, D_MODEL), lambda i: (i, 0)), st_spec, kv_spec, kv_spec],
        out_shape=[
            jax.ShapeDtypeStruct(x2d.shape, F32),
            jax.ShapeDtypeStruct(state.shape, F32),
            jax.ShapeDtypeStruct(kc.shape, F32),
            jax.ShapeDtypeStruct(vc.shape, F32),
        ],
        input_output_aliases={N_SMP_IN + k: 1 + k for k in range(n_prev)},
        compiler_params=pltpu.CompilerParams(
            dimension_semantics=("arbitrary",), vmem_limit_bytes=VMEM_LIMIT),
        name="mix_sample",
    )(sinks, x2d, state, kc, vc, *mix_w, *tabs, *(prev or ()))


def _rope_tables(pos):
    half = ROT_DIM // 2
    inv = jnp.power(jnp.float32(ROPE_THETA), -jnp.arange(half, dtype=jnp.float32) * (2.0 / ROT_DIM))
    ang = pos.astype(jnp.float32)[:, None] * inv[None, :]
    cos, sin = jnp.cos(ang), jnp.sin(ang)
    n = pos.shape[0]
    pad = jnp.zeros((n, HEAD_DIM - ROT_DIM), F32)
    zero = jnp.zeros((n, half), F32)
    c64 = jnp.concatenate([cos, cos, pad + 1.0], axis=1)
    a64 = jnp.concatenate([zero, sin, pad], axis=1)
    b64 = jnp.concatenate([-sin, zero, pad], axis=1)
    rep = LANES // HEAD_DIM
    return tuple(jnp.tile(t, (1, rep)) for t in (c64, a64, b64))


def kernel(x_prompt, x_sample, state_gla, cache_swa_k, cache_swa_v, norm_ffn1, w_ffn1_gu, w_ffn1_down,
           norm_mix, w_in, w_gate_up, b_gate, gla_norm, attn_sinks, swa_norm, w_out, norm_ffn2,
           w_ffn2_gu, w_ffn2_down, norm_final):
    B, L, _ = x_prompt.shape
    NB, Q, _ = x_sample.shape
    depth = w_in.shape[0]
    W = cache_swa_k.shape[2]

    tabs_p = _rope_tables(jnp.arange(L))
    tabs_s = _rope_tables(jnp.tile(PAST_LEN + jnp.arange(Q), SMP_SEQS))

    gu1, gu2 = _cast_bf16([w_ffn1_gu, w_ffn2_gu], 128)
    dn1, dn2 = _cast_bf16([w_ffn1_down, w_ffn2_down], D_FF // 4)
    (wout,) = _cast_bf16([w_out], D_MODEL // 2)
    win = _cast_w_in(w_in)
    wa2 = jnp.pad(w_gate_up, ((0, 0), (0, LANES - GATE_RANK), (0, 0))).astype(BF16)
    row = lambda p: p.reshape(depth, 1, -1)
    mix_w = (row(norm_mix), win, wa2, row(b_gate), row(gla_norm), row(swa_norm), wout)
    n1, n2 = row(norm_ffn1), row(norm_ffn2)

    xp = x_prompt.reshape(B * L, D_MODEL)
    xs = x_sample.reshape(NB * Q, D_MODEL)
    kc = cache_swa_k.reshape(depth, NB, W, SWA_KV_W)
    vc = cache_swa_v.reshape(depth, NB, W, SWA_KV_W)
    out_p = out_s = None
    for l in range(depth):
        nf = norm_final if l == depth - 1 else None
        xp = _ffn(xp, n1, gu1, dn1, l)
        xs = _ffn(xs, n1, gu1, dn1, l)
        xp3, *out_p = _mix_prompt(xp.reshape(B, L, D_MODEL), attn_sinks, mix_w, tabs_p, l, depth, out_p)
        xs, *out_s = _mix_sample(xs, state_gla, kc, vc, attn_sinks, mix_w, tabs_s, l, out_s)
        xp = _ffn(xp3.reshape(B * L, D_MODEL), n2, gu2, dn2, l, nf)
        xs = _ffn(xs, n2, gu2, dn2, l, nf)

    gla_p, k_p, v_p = out_p
    gla_s, k_s, v_s = out_s
    kv5 = lambda t: t.reshape(*t.shape[:3], SWA_KV_HEADS, HEAD_DIM)
    return (xp.reshape(B, L, D_MODEL), xs.reshape(NB, Q, D_MODEL), gla_p, kv5(k_p), kv5(v_p),
            gla_s, kv5(k_s), kv5(v_s))
```

```python
import functools
import itertools

import jax
import jax.numpy as jnp
import numpy as np
from jax import lax
from jax.experimental import pallas as pl
from jax.experimental.pallas import tpu as pltpu

F32 = jnp.float32
BF16 = jnp.bfloat16

D_MODEL = 1024
D_FF = 2816
EPS = 1e-6
GLA_HEADS = 4
GLA_DK = 64
GLA_DV = 128
GLA_QK_W = GLA_HEADS * GLA_DK
GLA_V_W = GLA_HEADS * GLA_DV
GATE_RANK = 16
GATE_NORM = 16.0
GLA_CHUNK = 64
HEAD_DIM = 64
SWA_HEADS = 8
SWA_KV_HEADS = 2
SWA_GROUP = SWA_HEADS // SWA_KV_HEADS
SWA_Q_W = SWA_HEADS * HEAD_DIM
SWA_KV_W = SWA_KV_HEADS * HEAD_DIM
WINDOW = 128
ROT_DIM = 16
ROPE_THETA = 500000.0
PAST_LEN = 16384
QK_SCALE = 0.125

LANES = 128
C_QG, C_KG, C_VG, C_RG = 0, 256, 512, 1024
C_QS, C_KS, C_VS, C_A = 1536, 2048, 2176, 2304
IN_W_PAD = C_A + LANES

FFN_TILE = 512
FFN_CHUNK = 256
MIX_TILE = 512
SMP_SEQS = 32
SMP_SUB = 8
VMEM_LIMIT = 56 * 1024 * 1024

NT_DIMS = (((1,), (1,)), ((), ()))
TN_DIMS = (((0,), (0,)), ((), ()))


def _dot(a, b):
    return jnp.dot(a, b, preferred_element_type=F32)


def _dot_nt(a, b):
    return lax.dot_general(a, b, NT_DIMS, preferred_element_type=F32)


def _dot_tn(a, b):
    return lax.dot_general(a, b, TN_DIMS, preferred_element_type=F32)


def _rmsnorm(x, g):
    return x * lax.rsqrt(jnp.mean(x * x, axis=-1, keepdims=True) + EPS) * g


def _log_sigmoid(z):
    return jnp.minimum(z, 0.0) - jnp.log1p(jnp.exp(-jnp.abs(z)))


def _silu(z):
    return z * jax.nn.sigmoid(z)


def _split_dot(m_bf16, g):
    hi = g.astype(BF16)
    lo = (g - hi.astype(F32)).astype(BF16)
    return _dot(m_bf16, hi) + _dot(m_bf16, lo)


def _rope(x, cos, sa, sb):
    half = ROT_DIM // 2
    outs = []
    for p in range(x.shape[1] // LANES):
        s = x[:, p * LANES:(p + 1) * LANES]
        outs.append(s * cos + pltpu.roll(s, half, 1) * sa + pltpu.roll(s, LANES - half, 1) * sb)
    return outs[0] if len(outs) == 1 else jnp.concatenate(outs, axis=1)


def _cast_kernel(*refs):
    n = len(refs) // 2
    for src, dst in zip(refs[:n], refs[n:]):
        dst[...] = src[...].astype(BF16)


def _cast_bf16(arrs, block_rows):
    depth, rows, cols = arrs[0].shape
    spec = pl.BlockSpec((None, block_rows, cols), lambda l, r: (l, r, 0))
    n = len(arrs)
    return pl.pallas_call(
        _cast_kernel,
        grid=(depth, rows // block_rows),
        in_specs=[spec] * n,
        out_specs=[spec] * n,
        out_shape=[jax.ShapeDtypeStruct(a.shape, BF16) for a in arrs],
        compiler_params=pltpu.CompilerParams(
            dimension_semantics=("arbitrary", "arbitrary"), vmem_limit_bytes=VMEM_LIMIT),
        name="cast_weights",
    )(*arrs)


def _cast_w_in_kernel(w_ref, o_ref):
    n_a = C_RG + GLA_V_W
    w = w_ref[...]
    o_ref[:, 0:n_a] = w[:, 0:n_a].astype(BF16)
    o_ref[:, n_a:C_A] = w[:, n_a + GATE_RANK:].astype(BF16)
    pad = jnp.zeros((w.shape[0], LANES - GATE_RANK), F32)
    o_ref[:, C_A:IN_W_PAD] = jnp.concatenate([w[:, n_a:n_a + GATE_RANK], pad], axis=1).astype(BF16)


def _cast_w_in(w_in):
    depth, rows, cols = w_in.shape
    block_rows = 256
    return pl.pallas_call(
        _cast_w_in_kernel,
        grid=(depth, rows // block_rows),
        in_specs=[pl.BlockSpec((None, block_rows, cols), lambda l, r: (l, r, 0))],
        out_specs=pl.BlockSpec((None, block_rows, IN_W_PAD), lambda l, r: (l, r, 0)),
        out_shape=jax.ShapeDtypeStruct((depth, rows, IN_W_PAD), BF16),
        compiler_params=pltpu.CompilerParams(
            dimension_semantics=("arbitrary", "arbitrary"), vmem_limit_bytes=VMEM_LIMIT),
        name="cast_w_in",
    )(w_in)


def _ffn_kernel(*refs, final, n_tiles):
    if final:
        xn_ref, xv_ref, n_ref, wgu_ref, wd_ref, nf_ref, o_ref, acc_ref, h_ref = refs
    else:
        xn_ref, xv_ref, n_ref, wgu_ref, wd_ref, o_ref, acc_ref, h_ref = refs
    i = pl.program_id(0)

    def epilogue():
        y = xv_ref[...] + 0.5 * acc_ref[...]
        if final:
            y = _rmsnorm(y, nf_ref[...])
        o_ref[...] = y

    def step(cur, nxt):
        epilogue()
        h_ref[nxt] = _rmsnorm(xn_ref[...], n_ref[...]).astype(BF16)
        h = h_ref[cur]
        for c in range(D_FF // FFN_CHUNK):
            lo = c * FFN_CHUNK
            gate = _dot(h, wgu_ref[:, lo:lo + FFN_CHUNK])
            up = _dot(h, wgu_ref[:, D_FF + lo:D_FF + lo + FFN_CHUNK])
            act = (_silu(gate) * up).astype(BF16)
            part = _dot(act, wd_ref[lo:lo + FFN_CHUNK, :])
            if c == 0:
                acc_ref[...] = part
            else:
                acc_ref[...] += part

    @pl.when(i == 0)
    def _():
        acc_ref[...] = jnp.zeros_like(acc_ref)
        h_ref[0] = _rmsnorm(xv_ref[...], n_ref[...]).astype(BF16)

    @pl.when(jnp.logical_and(i < n_tiles, i % 2 == 0))
    def _():
        step(0, 1)

    @pl.when(jnp.logical_and(i < n_tiles, i % 2 == 1))
    def _():
        step(1, 0)

    @pl.when(i == n_tiles)
    def _():
        epilogue()


def _ffn(x2d, norm, w_gu, w_down, layer, norm_final=None):
    n_tiles = x2d.shape[0] // FFN_TILE
    final = norm_final is not None
    at_layer = lambda i: (layer, 0, 0)
    nxt_map = lambda i: (jnp.minimum(i + 1, n_tiles - 1), 0)
    prv_map = lambda i: (jnp.maximum(i - 1, 0), 0)
    in_specs = [
        pl.BlockSpec((FFN_TILE, D_MODEL), nxt_map),
        pl.BlockSpec((FFN_TILE, D_MODEL), prv_map),
        pl.BlockSpec((None, 1, D_MODEL), at_layer),
        pl.BlockSpec((None, D_MODEL, 2 * D_FF), at_layer, pipeline_mode=pl.Buffered(1)),
        pl.BlockSpec((None, D_FF, D_MODEL), at_layer, pipeline_mode=pl.Buffered(1)),
    ]
    args = [x2d, x2d, norm, w_gu, w_down]
    if final:
        in_specs.append(pl.BlockSpec((1, D_MODEL), lambda i: (0, 0)))
        args.append(norm_final.reshape(1, D_MODEL))
    return pl.pallas_call(
        functools.partial(_ffn_kernel, final=final, n_tiles=n_tiles),
        grid=(n_tiles + 1,),
        in_specs=in_specs,
        out_specs=pl.BlockSpec((FFN_TILE, D_MODEL), prv_map),
        out_shape=jax.ShapeDtypeStruct(x2d.shape, F32),
        scratch_shapes=[pltpu.VMEM((FFN_TILE, D_MODEL), F32),
                        pltpu.VMEM((2, FFN_TILE, D_MODEL), BF16)],
        compiler_params=pltpu.CompilerParams(
            dimension_semantics=("arbitrary",), vmem_limit_bytes=VMEM_LIMIT),
        name="ffn_final" if final else "ffn",
    )(*args)


N_MIX_IN = 13


def _mix_project(slot, x_ref, nmix_ref, win_ref, wa2_ref, ba_ref, glag_ref, cos_ref, sa_ref, sb_ref,
                 qb_s, qsw_s, kd_s, klt_s, dect_s, v_s, gate_s, qh_s, kt_s, vt_s, kout_s, vout_s):
    T = MIX_TILE
    G = 256
    qb_s, qsw_s, kd_s, klt_s, dect_s, v_s, gate_s, qh_s, kt_s, vt_s, kout_s, vout_s = (
        r.at[slot] for r in (qb_s, qsw_s, kd_s, klt_s, dect_s, v_s, gate_s, qh_s, kt_s, vt_s, kout_s, vout_s))

    x = x_ref[...]
    h = _rmsnorm(x, nmix_ref[...]).astype(BF16)
    yield

    a_low = _dot(h, win_ref[:, C_A:C_A + LANES]).astype(BF16)
    z = _dot(a_low, wa2_ref[...]) + ba_ref[...]
    g = _log_sigmoid(z) * (1.0 / GATE_NORM)
    yield
    qk = _dot(h, win_ref[:, C_QG:C_QG + 2 * GLA_QK_W])
    yield

    ri = lax.broadcasted_iota(jnp.int32, (G, G), 0)
    ci = lax.broadcasted_iota(jnp.int32, (G, G), 1)
    same_chunk = (ri >> 6) == (ci >> 6)
    incl = jnp.where(jnp.logical_and(same_chunk, ci <= ri), 1.0, 0.0).astype(BF16)
    C = GLA_CHUNK
    chunk_last = []
    for r in range(T // G):
        rows = slice(r * G, (r + 1) * G)
        b = _split_dot(incl, g[rows])
        lasts = [b[c * C + C - 1:(c + 1) * C, :] for c in range(G // C)]
        chunk_last.extend(lasts)
        rest = jnp.concatenate([jnp.broadcast_to(t, (C, GLA_QK_W)) for t in lasts], axis=0) - b
        q = qk[rows, 0:GLA_QK_W]
        k = qk[rows, GLA_QK_W:2 * GLA_QK_W]
        qb = q * QK_SCALE * jnp.exp(b)
        qb_s[rows, :] = qb.astype(BF16)
        qsw_s[rows, :] = jnp.concatenate(
            [pltpu.roll(qb[:, p * LANES:(p + 1) * LANES], GLA_DK, 1) for p in range(GLA_QK_W // LANES)],
            axis=1)
        kd_s[rows, :] = k * jnp.exp(-b)
        klt_s[:, rows] = (k * jnp.exp(rest)).T.astype(BF16)
        yield
    pad = jnp.zeros((LANES - len(chunk_last), GLA_QK_W), F32)
    dect_s[...] = jnp.exp(jnp.concatenate(chunk_last + [pad], axis=0)).T

    v_s[...] = _dot(h, win_ref[:, C_VG:C_VG + GLA_V_W]).astype(BF16)
    yield
    gate_s[...] = _silu(_dot(h, win_ref[:, C_RG:C_RG + GLA_V_W])) * glag_ref[...]
    yield

    cos, sa, sb = cos_ref[...], sa_ref[...], sb_ref[...]
    half_t = lax.broadcasted_iota(jnp.int32, (T, LANES), 1) >> 6
    q_rot = _rope(_dot(h, win_ref[:, C_QS:C_QS + SWA_Q_W]), cos, sa, sb) * QK_SCALE
    for hq in range(SWA_HEADS):
        slab = q_rot[:, (hq // 2) * LANES:(hq // 2 + 1) * LANES]
        j = hq // SWA_GROUP
        if hq % 2 != j:
            slab = pltpu.roll(slab, HEAD_DIM, 1)
        qh_s[hq] = jnp.where(half_t == j, slab, 0.0).astype(BF16)
        if hq % 4 == 3:
            yield
    k_rot =_rope(_dot(h, win_ref[:, C_KS:C_KS + SWA_KV_W]), cos, sa, sb)
    v_swa = _dot(h, win_ref[:, C_VS:C_VS + SWA_KV_W])
    kt_s[...] = k_rot.astype(BF16)
    vt_s[...] = v_swa.T.astype(BF16)
    kout_s[...] = k_rot[T - WINDOW:T, :]
    vout_s[...] = v_swa[T - WINDOW:T, :]


def _mix_sequential(slot, first, layer, sink_ref, x_ref, swag_ref, wout_ref, xo_ref,
                    qb_s, qsw_s, kd_s, klt_s, dect_s, v_s, gate_s, qh_s, kt_s, vt_s,
                    s_s, og_s, cat_s, kcar_s, vcar_s, lhs_s, upd_s, st16_s, sc_s, pr_s, rden_s):
    T = MIX_TILE
    C = GLA_CHUNK
    qb_s, qsw_s, kd_s, klt_s, dect_s, v_s, gate_s, qh_s, kt_s, vt_s = (
        r.at[slot] for r in (qb_s, qsw_s, kd_s, klt_s, dect_s, v_s, gate_s, qh_s, kt_s, vt_s))

    hk_r = lax.broadcasted_iota(jnp.int32, (GLA_QK_W, GLA_QK_W), 0) >> 6
    hk_c = lax.broadcasted_iota(jnp.int32, (GLA_QK_W, GLA_QK_W), 1) >> 6
    bd_k = hk_r == hk_c
    qi = lax.broadcasted_iota(jnp.int32, (C, GLA_QK_W), 0)
    kj = lax.broadcasted_iota(jnp.int32, (C, GLA_QK_W), 1) & (C - 1)
    causal = kj <= qi
    half_c = lax.broadcasted_iota(jnp.int32, (C, LANES), 1) >> 6
    zeros_v = jnp.zeros((C, GLA_DV), BF16)
    n_c = T // C
    for c in range(n_c):
        rows = slice(c * C, (c + 1) * C)
        kd_bd = jnp.where(bd_k, jnp.concatenate([kd_s[rows, :]] * GLA_HEADS, axis=0), 0.0).astype(BF16)
        att = jnp.where(causal, _dot_nt(qb_s[rows, :], kd_bd), 0.0)
        tok = slice((c // 2) * 2 * C, (c // 2 + 1) * 2 * C)
        for hd in range(GLA_HEADS):
            slab = slice((hd // 2) * LANES, (hd // 2 + 1) * LANES)
            hk = slice(hd * GLA_DK, (hd + 1) * GLA_DK)
            v = v_s[rows, hd * GLA_DV:(hd + 1) * GLA_DV]
            lhs_s[c, hd] = jnp.where(half_c == hd % 2, att[:, slab], qsw_s[rows, slab]).astype(BF16)
            v_pair = jnp.concatenate([v, zeros_v] if c % 2 == 0 else [zeros_v, v], axis=0)
            upd_s[c, hk, :] = _dot(klt_s[hk, tok], v_pair)
        if c % 2 == 1:
            yield
    state = s_s[...]
    for c in range(n_c):
        st16_s[c] = state.astype(BF16)
        state = dect_s[:, c:c + 1] * state + upd_s[c]
    s_s[...] = state
    yield
    for c in range(n_c):
        rows = slice(c * C, (c + 1) * C)
        for hd in range(GLA_HEADS):
            hk = slice(hd * GLA_DK, (hd + 1) * GLA_DK)
            v = v_s[rows, hd * GLA_DV:(hd + 1) * GLA_DV]
            st = st16_s[c, hk, :]
            rhs = jnp.concatenate([v, st] if hd % 2 == 0 else [st, v], axis=0)
            og_s[rows, hd * GLA_DV:(hd + 1) * GLA_DV] = _dot(lhs_s[c, hd], rhs)
        if c % 2 == 1:
            yield

    for hd in range(GLA_HEADS):
        vs_ = slice(hd * GLA_DV, (hd + 1) * GLA_DV)
        o = og_s[:, vs_]
        o = o * lax.rsqrt(jnp.mean(o * o, axis=-1, keepdims=True) + EPS)
        cat_s[:, vs_] = (o * gate_s[:, vs_]).astype(BF16)
    yield

    swag = swag_ref[...]
    GR = SWA_GROUP * WINDOW
    kc_ = lax.broadcasted_iota(jnp.int32, (WINDOW, GR), 0)
    qr = lax.broadcasted_iota(jnp.int32, (WINDOW, GR), 1) & (WINDOW - 1)
    from_prev = kc_ > qr
    prev_bias = jnp.where(first > 0.0, -jnp.inf, 0.0)
    n_b = T // WINDOW
    sinks = [jnp.concatenate(
        [jnp.full((1, WINDOW), sink_ref[layer, j * SWA_GROUP + gq], F32) for gq in range(SWA_GROUP)], axis=1)
        for j in range(SWA_KV_HEADS)]
    for i in range(n_b):
        rows = slice(i * WINDOW, (i + 1) * WINDOW)
        if i == 0:
            k_blk = jnp.concatenate([kcar_s[...], kt_s[0:WINDOW, :]], axis=0)
        else:
            k_blk = kt_s[(i - 1) * WINDOW:(i + 1) * WINDOW, :]
        for j in range(SWA_KV_HEADS):
            q4 = jnp.concatenate([qh_s[j * SWA_GROUP + gq, rows, :] for gq in range(SWA_GROUP)], axis=0)
            s2 = _dot_nt(k_blk, q4)
            s_prev = s2[0:WINDOW, :] + prev_bias if i == 0 else s2[0:WINDOW, :]
            sc_s[i, j] = jnp.where(from_prev, s_prev, s2[WINDOW:2 * WINDOW, :])
        yield
    for i in range(n_b):
        for j in range(SWA_KV_HEADS):
            s = sc_s[i, j]
            m = jnp.maximum(jnp.max(s, axis=0, keepdims=True), sinks[j])
            pr = jnp.exp(s - m)
            rden_s[i, j] = 1.0 / (jnp.sum(pr, axis=0, keepdims=True) + jnp.exp(sinks[j] - m))
            pr_s[i, j, 0:WINDOW, :] = jnp.where(from_prev, pr, 0.0).astype(BF16)
            pr_s[i, j, WINDOW:2 * WINDOW, :] = jnp.where(from_prev, 0.0, pr).astype(BF16)
        yield
    for i in range(n_b):
        rows = slice(i * WINDOW, (i + 1) * WINDOW)
        if i == 0:
            vt_blk = jnp.concatenate([vcar_s[...], vt_s[:, 0:WINDOW]], axis=1)
        else:
            vt_blk = vt_s[:, (i - 1) * WINDOW:(i + 1) * WINDOW]
        pieces = []
        for j in range(SWA_KV_HEADS):
            o_t = _dot(vt_blk, pr_s[i, j]) * rden_s[i, j]
            pieces.extend(o_t[j * HEAD_DIM:(j + 1) * HEAD_DIM, gq * WINDOW:(gq + 1) * WINDOW]
                          for gq in range(SWA_GROUP))
        o_s = jnp.concatenate(pieces, axis=0).T
        cat_s[rows, GLA_V_W:GLA_V_W + SWA_Q_W] = _rmsnorm(o_s, swag).astype(BF16)
        yield

    xo_ref[...] = (x_ref[...] + _dot(cat_s[:, 0:GLA_V_W], wout_ref[0:GLA_V_W, :])
                   + _dot(cat_s[:, GLA_V_W:D_MODEL], wout_ref[GLA_V_W:D_MODEL, :]))

    kcar_s[...] = kt_s[T - WINDOW:T, :]
    vcar_s[...] = vt_s[:, T - WINDOW:T]


def _mix_prompt_kernel(*refs, layer, n_l, n_tiles):
    (sink_ref, xn_ref, xv_ref, nmix_ref, win_ref, wa2_ref, ba_ref, glag_ref, swag_ref,
     wout_ref, cos_ref, sa_ref, sb_ref) = refs[:N_MIX_IN]
    (xo_ref, so_ref, ko_ref, vo_ref,
     qb_s, qsw_s, kd_s, klt_s, dect_s, v_s, gate_s, qh_s, kt_s, vt_s, kout_s, vout_s,
     s_s, og_s, cat_s, kcar_s, vcar_s, lhs_s, upd_s, st16_s, sc_s, pr_s, rden_s) = refs[-27:]
    handover = (qb_s, qsw_s, kd_s, klt_s, dect_s, v_s, gate_s, qh_s, kt_s, vt_s)
    t = pl.program_id(0)
    l_seq = jnp.maximum(t - 1, 0) % n_l

    @pl.when(t == 0)
    def _():
        for r in handover:
            r[1] = jnp.zeros(r.shape[1:], r.dtype)
        kcar_s[...] = jnp.zeros_like(kcar_s)
        vcar_s[...] = jnp.zeros_like(vcar_s)

    @pl.when(l_seq == 0)
    def _():
        s_s[...] = jnp.zeros_like(s_s)

    first = jnp.where(l_seq == 0, 1.0, 0.0)

    def step(slot_proj, slot_seq):
        proj = _mix_project(slot_proj, xn_ref, nmix_ref, win_ref, wa2_ref, ba_ref, glag_ref, cos_ref,
                            sa_ref, sb_ref, *handover, kout_s, vout_s)
        seq = _mix_sequential(slot_seq, first, layer, sink_ref, xv_ref, swag_ref, wout_ref, xo_ref,
                              *handover, s_s, og_s, cat_s, kcar_s, vcar_s, lhs_s, upd_s, st16_s, sc_s, pr_s, rden_s)
        for _ in itertools.zip_longest(seq, proj):
            pass

        @pl.when(jnp.logical_and(t > 0, l_seq == n_l - 1))
        def _():
            for hd in range(GLA_HEADS):
                so_ref[hd] = s_s[hd * GLA_DK:(hd + 1) * GLA_DK, :]
            ko_ref[...] = kout_s[slot_seq]
            vo_ref[...] = vout_s[slot_seq]

    @pl.when(t % 2 == 0)
    def _():
        step(0, 1)

    @pl.when(t % 2 == 1)
    def _():
        step(1, 0)


def _mixer_weight_specs(layer, grid_rank):
    at_layer = (lambda b, l: (layer, 0, 0)) if grid_rank == 2 else (lambda i: (layer, 0, 0))
    return [
        pl.BlockSpec((None, 1, D_MODEL), at_layer),
        pl.BlockSpec((None, D_MODEL, IN_W_PAD), at_layer),
        pl.BlockSpec((None, LANES, GLA_QK_W), at_layer),
        pl.BlockSpec((None, 1, GLA_QK_W), at_layer),
        pl.BlockSpec((None, 1, GLA_V_W), at_layer),
        pl.BlockSpec((None, 1, SWA_Q_W), at_layer),
        pl.BlockSpec((None, D_MODEL, D_MODEL), at_layer),
    ]


def _mix_prompt(x, sinks, mix_w, tabs, layer, depth, prev):
    B, L, _ = x.shape
    T = MIX_TILE
    n_l = L // T
    n_tiles = B * n_l
    t_proj = lambda t: jnp.minimum(t, n_tiles - 1)
    t_seq = lambda t: jnp.maximum(t - 1, 0)
    tab_spec = pl.BlockSpec((T, LANES), lambda t: (t_proj(t) % n_l, 0))
    n_prev = 0 if prev is None else len(prev)
    return pl.pallas_call(
        functools.partial(_mix_prompt_kernel, layer=layer, n_l=n_l, n_tiles=n_tiles),
        grid=(n_tiles + 1,),
        in_specs=[
            pl.BlockSpec(memory_space=pltpu.SMEM),
            pl.BlockSpec((None, T, D_MODEL), lambda t: (t_proj(t) // n_l, t_proj(t) % n_l, 0)),
            pl.BlockSpec((None, T, D_MODEL), lambda t: (t_seq(t) // n_l, t_seq(t) % n_l, 0)),
            *_mixer_weight_specs(layer, 1),
            tab_spec, tab_spec, tab_spec,
            *([pl.BlockSpec(memory_space=pl.ANY)] * n_prev),
        ],
        out_specs=[
            pl.BlockSpec((None, T, D_MODEL), lambda t: (t_seq(t) // n_l, t_seq(t) % n_l, 0)),
            pl.BlockSpec((None, None, GLA_HEADS, GLA_DK, GLA_DV), lambda t: (layer, t_seq(t) // n_l, 0, 0, 0)),
            pl.BlockSpec((None, None, WINDOW, SWA_KV_W), lambda t: (layer, t_seq(t) // n_l, 0, 0)),
            pl.BlockSpec((None, None, WINDOW, SWA_KV_W), lambda t: (layer, t_seq(t) // n_l, 0, 0)),
        ],
        out_shape=[
            jax.ShapeDtypeStruct((B, L, D_MODEL), F32),
            jax.ShapeDtypeStruct((depth, B, GLA_HEADS, GLA_DK, GLA_DV), F32),
            jax.ShapeDtypeStruct((depth, B, WINDOW, SWA_KV_W), F32),
            jax.ShapeDtypeStruct((depth, B, WINDOW, SWA_KV_W), F32),
        ],
        input_output_aliases={N_MIX_IN + k: 1 + k for k in range(n_prev)},
        scratch_shapes=[
            pltpu.VMEM((2, T, GLA_QK_W), BF16),
            pltpu.VMEM((2, T, GLA_QK_W), F32),
            pltpu.VMEM((2, T, GLA_QK_W), F32),
            pltpu.VMEM((2, GLA_QK_W, T), BF16),
            pltpu.VMEM((2, GLA_QK_W, LANES), F32),
            pltpu.VMEM((2, T, GLA_V_W), BF16),
            pltpu.VMEM((2, T, GLA_V_W), F32),
            pltpu.VMEM((2, SWA_HEADS, T, LANES), BF16),
            pltpu.VMEM((2, T, SWA_KV_W), BF16),
            pltpu.VMEM((2, SWA_KV_W, T), BF16),
            pltpu.VMEM((2, WINDOW, SWA_KV_W), F32),
            pltpu.VMEM((2, WINDOW, SWA_KV_W), F32),
            pltpu.VMEM((GLA_QK_W, GLA_DV), F32),
            pltpu.VMEM((T, GLA_V_W), F32),
            pltpu.VMEM((T, D_MODEL), BF16),
            pltpu.VMEM((WINDOW, SWA_KV_W), BF16),
            pltpu.VMEM((SWA_KV_W, WINDOW), BF16),
            pltpu.VMEM((T // GLA_CHUNK, GLA_HEADS, GLA_CHUNK, LANES), BF16),
            pltpu.VMEM((T // GLA_CHUNK, GLA_QK_W, GLA_DV), F32),
            pltpu.VMEM((T // GLA_CHUNK, GLA_QK_W, GLA_DV), BF16),
            pltpu.VMEM((T // WINDOW, SWA_KV_HEADS, WINDOW, SWA_GROUP * WINDOW), F32),
            pltpu.VMEM((T // WINDOW, SWA_KV_HEADS, 2 * WINDOW, SWA_GROUP * WINDOW), BF16),
            pltpu.VMEM((T // WINDOW, SWA_KV_HEADS, 1, SWA_GROUP * WINDOW), F32),
        ],
        compiler_params=pltpu.CompilerParams(
            dimension_semantics=("arbitrary",), vmem_limit_bytes=VMEM_LIMIT),
        name="mix_prompt",
    )(sinks, x, x, *mix_w, *tabs, *(prev or ()))


N_SMP_IN = 15


def _mix_sample_kernel(*refs, layer):
    (sink_ref, x_ref, s_ref, kc_ref, vc_ref, nmix_ref, win_ref, wa2_ref, ba_ref,
     glag_ref, swag_ref, wout_ref, cos_ref, sa_ref, sb_ref) = refs[:N_SMP_IN]
    xo_ref, so_ref, ko_ref, vo_ref = refs[-4:]
    R = x_ref.shape[0]
    NS = SMP_SEQS
    Q = R // NS
    q_shift = Q.bit_length() - 1

    x = x_ref[...]
    h = _rmsnorm(x, nmix_ref[...]).astype(BF16)
    proj = _dot(h, win_ref[...])
    z = _dot(proj[:, C_A:C_A + LANES].astype(BF16), wa2_ref[...]) + ba_ref[...]
    g = _log_sigmoid(z) * (1.0 / GATE_NORM)

    ri = lax.broadcasted_iota(jnp.int32, (R, R), 0)
    ci = lax.broadcasted_iota(jnp.int32, (R, R), 1)
    same = (ri >> q_shift) == (ci >> q_shift)
    causal = jnp.logical_and(same, ci <= ri)
    after = jnp.logical_and(same, ci > ri)
    b = _split_dot(jnp.where(causal, 1.0, 0.0).astype(BF16), g)
    rest = _split_dot(jnp.where(after, 1.0, 0.0).astype(BF16), g)

    qg = proj[:, C_QG:C_QG + GLA_QK_W]
    kg = proj[:, C_KG:C_KG + GLA_QK_W]
    qb = qg * QK_SCALE * jnp.exp(b)
    kd = (kg * jnp.exp(-b)).astype(BF16)
    kl_t = (kg * jnp.exp(rest)).T
    g_t = g.T

    own_tok = ((lax.broadcasted_iota(jnp.int32, (NS * GLA_DK, R), 0) >> 6)
               == (lax.broadcasted_iota(jnp.int32, (NS * GLA_DK, R), 1) >> q_shift))
    own_seq = ((lax.broadcasted_iota(jnp.int32, (R, NS * GLA_DK), 1) >> 6)
               == (lax.broadcasted_iota(jnp.int32, (R, NS * GLA_DK), 0) >> q_shift))
    lane_half = lax.broadcasted_iota(jnp.int32, (R, LANES), 1) >> 6
    ones = jnp.ones((R, GLA_DV), BF16)

    glag = glag_ref[...]
    og = []
    for hd in range(GLA_HEADS):
        ks_ = slice(hd * GLA_DK, (hd + 1) * GLA_DK)
        v = proj[:, C_VG + hd * GLA_DV:C_VG + (hd + 1) * GLA_DV].astype(BF16)
        att = jnp.where(causal, _dot_nt(qb[:, ks_].astype(BF16), kd[:, ks_]), 0.0).astype(BF16)
        o = _dot(att, v)
        s_old = s_ref[:, hd].reshape(NS * GLA_DK, GLA_DV)
        slab = qb[:, (hd // 2) * LANES:(hd // 2 + 1) * LANES]
        dup = jnp.where(lane_half == hd % 2, slab, pltpu.roll(slab, HEAD_DIM, 1))
        q_exp = jnp.where(own_seq, jnp.concatenate([dup] * (NS // 2), axis=1), 0.0).astype(BF16)
        o = o + _dot(q_exp, s_old.astype(BF16))
        o = o * lax.rsqrt(jnp.mean(o * o, axis=-1, keepdims=True) + EPS)
        rg = proj[:, C_RG + hd * GLA_DV:C_RG + (hd + 1) * GLA_DV]
        og.append(o * glag[:, hd * GLA_DV:(hd + 1) * GLA_DV] * _silu(rg))
        kl_exp = jnp.where(own_tok, jnp.concatenate([kl_t[ks_, :]] * NS, axis=0), 0.0).astype(BF16)
        g_exp = jnp.where(own_tok, jnp.concatenate([g_t[ks_, :]] * NS, axis=0), 0.0)
        b_last = _split_dot_rhs(g_exp, ones)
        s_new = jnp.exp(b_last) * s_old + _dot(kl_exp, v)
        so_ref[:, hd] = s_new.reshape(NS, GLA_DK, GLA_DV)

    cos, sa, sb = cos_ref[...], sa_ref[...], sb_ref[...]
    q_rot = _rope(proj[:, C_QS:C_QS + SWA_Q_W], cos, sa, sb) * QK_SCALE
    k_rot = _rope(proj[:, C_KS:C_KS + SWA_KV_W], cos, sa, sb)
    v_new = proj[:, C_VS:C_VS + SWA_KV_W]

    W = kc_ref.shape[1]
    SR = SMP_SUB * Q
    GR = SWA_GROUP * SR
    sub_shift = SR.bit_length() - 1
    w_shift = W.bit_length() - 1
    r_c = lax.broadcasted_iota(jnp.int32, (GR, SMP_SUB * W), 0)
    c_c = lax.broadcasted_iota(jnp.int32, (GR, SMP_SUB * W), 1)
    vis_c = jnp.logical_and(((r_c & (SR - 1)) >> q_shift) == (c_c >> w_shift),
                            (c_c & (W - 1)) > (r_c & (Q - 1)))
    r_n = lax.broadcasted_iota(jnp.int32, (GR, SR), 0)
    c_n = lax.broadcasted_iota(jnp.int32, (GR, SR), 1)
    vis_n = jnp.logical_and(((r_n & (SR - 1)) >> q_shift) == (c_n >> q_shift),
                            (c_n & (Q - 1)) <= (r_n & (Q - 1)))
    grp_row = lax.broadcasted_iota(jnp.int32, (GR, 1), 0) >> sub_shift
    half_sr = lax.broadcasted_iota(jnp.int32, (SR, LANES), 1) >> 6
    half_gr = lax.broadcasted_iota(jnp.int32, (GR, LANES), 1) >> 6

    o_rows = []
    for sg in range(NS // SMP_SUB):
        rows = slice(sg * SR, (sg + 1) * SR)
        k_c = kc_ref[sg * SMP_SUB:(sg + 1) * SMP_SUB].reshape(SMP_SUB * W, SWA_KV_W).astype(BF16)
        v_c = vc_ref[sg * SMP_SUB:(sg + 1) * SMP_SUB].reshape(SMP_SUB * W, SWA_KV_W).astype(BF16)
        k_n = k_rot[rows, :].astype(BF16)
        v_n = v_new[rows, :].astype(BF16)
        o_kv = []
        for j in range(SWA_KV_HEADS):
            parts = []
            for gq in range(SWA_GROUP):
                hq = j * SWA_GROUP + gq
                slab = q_rot[rows, (hq // 2) * LANES:(hq // 2 + 1) * LANES]
                if hq % 2 != j:
                    slab = pltpu.roll(slab, HEAD_DIM, 1)
                parts.append(jnp.where(half_sr == j, slab, 0.0))
            q8 = jnp.concatenate(parts, axis=0).astype(BF16)
            s_c = jnp.where(vis_c, _dot_nt(q8, k_c), -jnp.inf)
            s_n = jnp.where(vis_n, _dot_nt(q8, k_n), -jnp.inf)
            sink = jnp.zeros((GR, 1), F32)
            for gq in range(SWA_GROUP):
                sink = jnp.where(grp_row == gq, sink_ref[layer, j * SWA_GROUP + gq], sink)
            m = jnp.maximum(jnp.maximum(jnp.max(s_c, axis=-1, keepdims=True),
                                        jnp.max(s_n, axis=-1, keepdims=True)), sink)
            p_c = jnp.exp(s_c - m)
            p_n = jnp.exp(s_n - m)
            den = (jnp.sum(p_c, axis=-1, keepdims=True) + jnp.sum(p_n, axis=-1, keepdims=True)
                   + jnp.exp(sink - m))
            o = _dot((p_c / den).astype(BF16), v_c) + _dot((p_n / den).astype(BF16), v_n)
            o_kv.append(o)
        slabs = []
        for p in range(SWA_HEADS // 2):
            halves = []
            for hq in (2 * p, 2 * p + 1):
                j, gq = hq // SWA_GROUP, hq % SWA_GROUP
                piece = o_kv[j][gq * SR:(gq + 1) * SR, :]
                if hq % 2 != j:
                    piece = pltpu.roll(piece, HEAD_DIM, 1)
                halves.append(piece)
            slabs.append(jnp.where(half_sr == 0, halves[0], halves[1]))
        o_rows.append(jnp.concatenate(slabs, axis=1))
    o_s = _rmsnorm(jnp.concatenate(o_rows, axis=0), swag_ref[...])

    cat = jnp.concatenate(og + [o_s], axis=1).astype(BF16)
    xo_ref[...] = x + _dot(cat, wout_ref[...])

    ko_ref[:, 0:W - Q, :] = kc_ref[:, Q:W, :]
    vo_ref[:, 0:W - Q, :] = vc_ref[:, Q:W, :]
    for s in range(NS):
        ko_ref[s, W - Q:W, :] = k_rot[s * Q:(s + 1) * Q, :]
        vo_ref[s, W - Q:W, :] = v_new[s * Q:(s + 1) * Q, :]


def _split_dot_rhs(g, m_bf16):
    hi = g.astype(BF16)
    lo = (g - hi.astype(F32)).astype(BF16)
    return _dot(hi, m_bf16) + _dot(lo, m_bf16)


def _mix_sample(x2d, state, kc, vc, sinks, mix_w, tabs, layer, prev):
    depth, NB, W = kc.shape[0], kc.shape[1], kc.shape[2]
    Q = x2d.shape[0] // NB
    R = SMP_SEQS * Q
    tab_spec = pl.BlockSpec((R, LANES), lambda i: (0, 0))
    st_spec = pl.BlockSpec((None, SMP_SEQS, GLA_HEADS, GLA_DK, GLA_DV), lambda i: (layer, i, 0, 0, 0))
    kv_spec = pl.BlockSpec((None, SMP_SEQS, W, SWA_KV_W), lambda i: (layer, i, 0, 0))
    n_prev = 0 if prev is None else len(prev)
    return pl.pallas_call(
        functools.partial(_mix_sample_kernel, layer=layer),
        grid=(NB // SMP_SEQS,),
        in_specs=[
            pl.BlockSpec(memory_space=pltpu.SMEM),
            pl.BlockSpec((R, D_MODEL), lambda i: (i, 0)),
            st_spec, kv_spec, kv_spec,
            *_mixer_weight_specs(layer, 1),
            tab_spec, tab_spec, tab_spec,
            *([pl.BlockSpec(memory_space=pl.ANY)] * n_prev),
        ],
        out_specs=[pl.BlockSpec((R, D_MODEL), lambda i: (i, 0)), st_spec, kv_spec, kv_spec],
        out_shape=[
            jax.ShapeDtypeStruct(x2d.shape, F32),
            jax.ShapeDtypeStruct(state.shape, F32),
            jax.ShapeDtypeStruct(kc.shape, F32),
            jax.ShapeDtypeStruct(vc.shape, F32),
        ],
        input_output_aliases={N_SMP_IN + k: 1 + k for k in range(n_prev)},
        compiler_params=pltpu.CompilerParams(
            dimension_semantics=("arbitrary",), vmem_limit_bytes=VMEM_LIMIT),
        name="mix_sample",
    )(sinks, x2d, state, kc, vc, *mix_w, *tabs, *(prev or ()))


def _rope_tables(pos):
    half = ROT_DIM // 2
    inv = jnp.power(jnp.float32(ROPE_THETA), -jnp.arange(half, dtype=jnp.float32) * (2.0 / ROT_DIM))
    ang = pos.astype(jnp.float32)[:, None] * inv[None, :]
    cos, sin = jnp.cos(ang), jnp.sin(ang)
    n = pos.shape[0]
    pad = jnp.zeros((n, HEAD_DIM - ROT_DIM), F32)
    zero = jnp.zeros((n, half), F32)
    c64 = jnp.concatenate([cos, cos, pad + 1.0], axis=1)
    a64 = jnp.concatenate([zero, sin, pad], axis=1)
    b64 = jnp.concatenate([-sin, zero, pad], axis=1)
    rep = LANES // HEAD_DIM
    return tuple(jnp.tile(t, (1, rep)) for t in (c64, a64, b64))


def kernel(x_prompt, x_sample, state_gla, cache_swa_k, cache_swa_v, norm_ffn1, w_ffn1_gu, w_ffn1_down,
           norm_mix, w_in, w_gate_up, b_gate, gla_norm, attn_sinks, swa_norm, w_out, norm_ffn2,
           w_ffn2_gu, w_ffn2_down, norm_final):
    B, L, _ = x_prompt.shape
    NB, Q, _ = x_sample.shape
    depth = w_in.shape[0]
    W = cache_swa_k.shape[2]

    tabs_p = _rope_tables(jnp.arange(L))
    tabs_s = _rope_tables(jnp.tile(PAST_LEN + jnp.arange(Q), SMP_SEQS))

    gu1, gu2 = _cast_bf16([w_ffn1_gu, w_ffn2_gu], 128)
    dn1, dn2 = _cast_bf16([w_ffn1_down, w_ffn2_down], D_FF // 4)
    (wout,) = _cast_bf16([w_out], D_MODEL // 2)
    win = _cast_w_in(w_in)
    wa2 = jnp.pad(w_gate_up, ((0, 0), (0, LANES - GATE_RANK), (0, 0))).astype(BF16)
    row = lambda p: p.reshape(depth, 1, -1)
    mix_w = (row(norm_mix), win, wa2, row(b_gate), row(gla_norm), row(swa_norm), wout)
    n1, n2 = row(norm_ffn1), row(norm_ffn2)

    xp = x_prompt.reshape(B * L, D_MODEL)
    xs = x_sample.reshape(NB * Q, D_MODEL)
    kc = cache_swa_k.reshape(depth, NB, W, SWA_KV_W)
    vc = cache_swa_v.reshape(depth, NB, W, SWA_KV_W)
    out_p = out_s = None
    for l in range(depth):
        nf = norm_final if l == depth - 1 else None
        xp = _ffn(xp, n1, gu1, dn1, l)
        xs = _ffn(xs, n1, gu1, dn1, l)
        xp3, *out_p = _mix_prompt(xp.reshape(B, L, D_MODEL), attn_sinks, mix_w, tabs_p, l, depth, out_p)
        xs, *out_s = _mix_sample(xs, state_gla, kc, vc, attn_sinks, mix_w, tabs_s, l, out_s)
        xp = _ffn(xp3.reshape(B * L, D_MODEL), n2, gu2, dn2, l, nf)
        xs = _ffn(xs, n2, gu2, dn2, l, nf)

    gla_p, k_p, v_p = out_p
    gla_s, k_s, v_s = out_s
    kv5 = lambda t: t.reshape(*t.shape[:3], SWA_KV_HEADS, HEAD_DIM)
    return (xp.reshape(B, L, D_MODEL), xs.reshape(NB, Q, D_MODEL), gla_p, kv5(k_p), kv5(v_p),
            gla_s, kv5(k_s), kv5(v_s))
```

```python
import functools
import itertools
from typing import NamedTuple

import jax
import jax.numpy as jnp
import numpy as np
from jax import lax
from jax.experimental import pallas as pl
from jax.experimental.pallas import tpu as pltpu

F32 = jnp.float32
BF16 = jnp.bfloat16

D_MODEL = 1024
D_FF = 2816
EPS = 1e-6
GLA_HEADS = 4
GLA_DK = 64
GLA_DV = 128
GLA_QK_W = GLA_HEADS * GLA_DK
GLA_V_W = GLA_HEADS * GLA_DV
GATE_RANK = 16
GATE_NORM = 16.0
GLA_CHUNK = 64
HEAD_DIM = 64
SWA_HEADS = 8
SWA_KV_HEADS = 2
SWA_GROUP = SWA_HEADS // SWA_KV_HEADS
SWA_Q_W = SWA_HEADS * HEAD_DIM
SWA_KV_W = SWA_KV_HEADS * HEAD_DIM
WINDOW = 128
ROT_DIM = 16
ROPE_THETA = 500000.0
PAST_LEN = 16384
QK_SCALE = 0.125

LANES = 128
C_QG, C_KG, C_VG, C_RG = 0, 256, 512, 1024
C_QS, C_KS, C_VS, C_A = 1536, 2048, 2176, 2304
IN_W_PAD = C_A + LANES

FFN_TILE = 512
FFN_CHUNK = 256
MIX_TILE = 512
SMP_SEQS = 32
SMP_SUB = 8
VMEM_LIMIT = 56 * 1024 * 1024

NT_DIMS = (((1,), (1,)), ((), ()))
TN_DIMS = (((0,), (0,)), ((), ()))


def _dot(a, b):
    return jnp.dot(a, b, preferred_element_type=F32)


def _dot_nt(a, b):
    return lax.dot_general(a, b, NT_DIMS, preferred_element_type=F32)


def _dot_tn(a, b):
    return lax.dot_general(a, b, TN_DIMS, preferred_element_type=F32)


def _rmsnorm(x, g):
    return x * lax.rsqrt(jnp.mean(x * x, axis=-1, keepdims=True) + EPS) * g


def _log_sigmoid(z):
    return jnp.minimum(z, 0.0) - jnp.log1p(jnp.exp(-jnp.abs(z)))


def _silu(z):
    return z * jax.nn.sigmoid(z)


def _split_dot(m_bf16, g):
    hi = g.astype(BF16)
    lo = (g - hi.astype(F32)).astype(BF16)
    return _dot(m_bf16, hi) + _dot(m_bf16, lo)


def _rope(x, cos, sa, sb):
    half = ROT_DIM // 2
    outs = []
    for p in range(x.shape[1] // LANES):
        s = x[:, p * LANES:(p + 1) * LANES]
        outs.append(s * cos + pltpu.roll(s, half, 1) * sa + pltpu.roll(s, LANES - half, 1) * sb)
    return outs[0] if len(outs) == 1 else jnp.concatenate(outs, axis=1)


class SideCast(NamedTuple):
    src: jax.Array
    layer: int
    block_rows: int
    reorder_w_in: bool = False

    @property
    def n_blocks(self):
        return self.src.shape[1] // self.block_rows

    @property
    def out_cols(self):
        return IN_W_PAD if self.reorder_w_in else self.src.shape[2]


def _cast_block(src_ref, dst_ref, reorder_w_in):
    w = src_ref[...]
    if not reorder_w_in:
        dst_ref[...] = w.astype(BF16)
        return
    n_a = C_RG + GLA_V_W
    dst_ref[:, 0:n_a] = w[:, 0:n_a].astype(BF16)
    dst_ref[:, n_a:C_A] = w[:, n_a + GATE_RANK:].astype(BF16)
    pad = jnp.zeros((w.shape[0], LANES - GATE_RANK), F32)
    dst_ref[:, C_A:IN_W_PAD] = jnp.concatenate([w[:, n_a:n_a + GATE_RANK], pad], axis=1).astype(BF16)


def _side_cast_specs(sides):
    in_specs, out_specs, out_shapes = [], [], []
    for s in sides:
        last = s.n_blocks - 1
        in_specs.append(pl.BlockSpec((None, s.block_rows, s.src.shape[2]),
                                     lambda i, layer=s.layer, last=last: (layer, jnp.minimum(i, last), 0)))
        out_specs.append(pl.BlockSpec((s.block_rows, s.out_cols), lambda i, last=last: (jnp.minimum(i, last), 0)))
        out_shapes.append(jax.ShapeDtypeStruct((s.src.shape[1], s.out_cols), BF16))
    return in_specs, out_specs, out_shapes


def _run_side_casts(step, sides, src_refs, dst_refs):
    for (n_blocks, reorder), src, dst in zip(sides, src_refs, dst_refs):
        @pl.when(step < n_blocks)
        def _():
            _cast_block(src, dst, reorder)


def _cast_kernel(*refs, sides):
    n = len(sides)
    _run_side_casts(pl.program_id(0), sides, refs[:n], refs[n:])


def _cast_layers(sides):
    in_specs, out_specs, out_shapes = _side_cast_specs(sides)
    return pl.pallas_call(
        functools.partial(_cast_kernel, sides=tuple((s.n_blocks, s.reorder_w_in) for s in sides)),
        grid=(max(s.n_blocks for s in sides),),
        in_specs=in_specs,
        out_specs=out_specs,
        out_shape=out_shapes,
        compiler_params=pltpu.CompilerParams(
            dimension_semantics=("arbitrary",), vmem_limit_bytes=VMEM_LIMIT),
        name="cast_weights",
    )(*(s.src for s in sides))


def _ffn_kernel(*refs, final, n_tiles, sides):
    n_in = 6 if final else 5
    n_side = len(sides)
    xn_ref, xv_ref, n_ref, wgu_ref, wd_ref = refs[:5]
    nf_ref = refs[5] if final else None
    o_ref = refs[n_in + n_side]
    acc_ref, h_ref = refs[-2:]
    i = pl.program_id(0)
    _run_side_casts(i, sides, refs[n_in:n_in + n_side], refs[n_in + n_side + 1:n_in + 2 * n_side + 1])

    def epilogue():
        y = xv_ref[...] + 0.5 * acc_ref[...]
        if final:
            y = _rmsnorm(y, nf_ref[...])
        o_ref[...] = y

    def step(cur, nxt):
        epilogue()
        h_ref[nxt] = _rmsnorm(xn_ref[...], n_ref[...]).astype(BF16)
        h = h_ref[cur]
        for c in range(D_FF // FFN_CHUNK):
            lo = c * FFN_CHUNK
            gate = _dot(h, wgu_ref[:, lo:lo + FFN_CHUNK])
            up = _dot(h, wgu_ref[:, D_FF + lo:D_FF + lo + FFN_CHUNK])
            act = (_silu(gate) * up).astype(BF16)
            part = _dot(act, wd_ref[lo:lo + FFN_CHUNK, :])
            if c == 0:
                acc_ref[...] = part
            else:
                acc_ref[...] += part

    @pl.when(i == 0)
    def _():
        acc_ref[...] = jnp.zeros_like(acc_ref)
        h_ref[0] = _rmsnorm(xv_ref[...], n_ref[...]).astype(BF16)

    @pl.when(jnp.logical_and(i < n_tiles, i % 2 == 0))
    def _():
        step(0, 1)

    @pl.when(jnp.logical_and(i < n_tiles, i % 2 == 1))
    def _():
        step(1, 0)

    @pl.when(i == n_tiles)
    def _():
        epilogue()


def _ffn(x2d, norm, w_gu, w_down, layer, norm_final=None, sides=()):
    n_tiles = x2d.shape[0] // FFN_TILE
    final = norm_final is not None
    const = lambda i: (0, 0)
    nxt_map = lambda i: (jnp.minimum(i + 1, n_tiles - 1), 0)
    prv_map = lambda i: (jnp.maximum(i - 1, 0), 0)
    in_specs = [
        pl.BlockSpec((FFN_TILE, D_MODEL), nxt_map),
        pl.BlockSpec((FFN_TILE, D_MODEL), prv_map),
        pl.BlockSpec((None, 1, D_MODEL), lambda i: (layer, 0, 0)),
        pl.BlockSpec((D_MODEL, 2 * D_FF), const, pipeline_mode=pl.Buffered(1)),
        pl.BlockSpec((D_FF, D_MODEL), const, pipeline_mode=pl.Buffered(1)),
    ]
    args = [x2d, x2d, norm, w_gu, w_down]
    if final:
        in_specs.append(pl.BlockSpec((1, D_MODEL), const))
        args.append(norm_final.reshape(1, D_MODEL))
    side_in, side_out, side_shapes = _side_cast_specs(sides)
    outs = pl.pallas_call(
        functools.partial(_ffn_kernel, final=final, n_tiles=n_tiles,
                          sides=tuple((s.n_blocks, s.reorder_w_in) for s in sides)),
        grid=(n_tiles + 1,),
        in_specs=in_specs + side_in,
        out_specs=[pl.BlockSpec((FFN_TILE, D_MODEL), prv_map)] + side_out,
        out_shape=[jax.ShapeDtypeStruct(x2d.shape, F32)] + side_shapes,
        scratch_shapes=[pltpu.VMEM((FFN_TILE, D_MODEL), F32),
                        pltpu.VMEM((2, FFN_TILE, D_MODEL), BF16)],
        compiler_params=pltpu.CompilerParams(
            dimension_semantics=("arbitrary",), vmem_limit_bytes=VMEM_LIMIT),
        name="ffn_final" if final else "ffn",
    )(*args, *(s.src for s in sides))
    return outs if sides else outs[0]


N_MIX_IN = 13


def _mix_project(slot, x_ref, nmix_ref, win_ref, wa2_ref, ba_ref, glag_ref, cos_ref, sa_ref, sb_ref,
                 qb_s, qsw_s, kd_s, klt_s, dect_s, v_s, gate_s, qh_s, kt_s, vt_s, kout_s, vout_s):
    T = MIX_TILE
    G = 256
    qb_s, qsw_s, kd_s, klt_s, dect_s, v_s, gate_s, qh_s, kt_s, vt_s, kout_s, vout_s = (
        r.at[slot] for r in (qb_s, qsw_s, kd_s, klt_s, dect_s, v_s, gate_s, qh_s, kt_s, vt_s, kout_s, vout_s))

    x = x_ref[...]
    h = _rmsnorm(x, nmix_ref[...]).astype(BF16)
    yield

    a_low = _dot(h, win_ref[:, C_A:C_A + LANES]).astype(BF16)
    z = _dot(a_low, wa2_ref[...]) + ba_ref[...]
    g = _log_sigmoid(z) * (1.0 / GATE_NORM)
    yield
    qk = _dot(h, win_ref[:, C_QG:C_QG + 2 * GLA_QK_W])
    yield

    ri = lax.broadcasted_iota(jnp.int32, (G, G), 0)
    ci = lax.broadcasted_iota(jnp.int32, (G, G), 1)
    same_chunk = (ri >> 6) == (ci >> 6)
    incl = jnp.where(jnp.logical_and(same_chunk, ci <= ri), 1.0, 0.0).astype(BF16)
    C = GLA_CHUNK
    chunk_last = []
    for r in range(T // G):
        rows = slice(r * G, (r + 1) * G)
        b = _split_dot(incl, g[rows])
        lasts = [b[c * C + C - 1:(c + 1) * C, :] for c in range(G // C)]
        chunk_last.extend(lasts)
        rest = jnp.concatenate([jnp.broadcast_to(t, (C, GLA_QK_W)) for t in lasts], axis=0) - b
        q = qk[rows, 0:GLA_QK_W]
        k = qk[rows, GLA_QK_W:2 * GLA_QK_W]
        qb = q * QK_SCALE * jnp.exp(b)
        qb_s[rows, :] = qb.astype(BF16)
        qsw_s[rows, :] = jnp.concatenate(
            [pltpu.roll(qb[:, p * LANES:(p + 1) * LANES], GLA_DK, 1) for p in range(GLA_QK_W // LANES)],
            axis=1)
        kd_s[rows, :] = k * jnp.exp(-b)
        klt_s[:, rows] = (k * jnp.exp(rest)).T.astype(BF16)
        yield
    pad = jnp.zeros((LANES - len(chunk_last), GLA_QK_W), F32)
    dect_s[...] = jnp.exp(jnp.concatenate(chunk_last + [pad], axis=0)).T

    v_s[...] = _dot(h, win_ref[:, C_VG:C_VG + GLA_V_W]).astype(BF16)
    yield
    gate_s[...] = _silu(_dot(h, win_ref[:, C_RG:C_RG + GLA_V_W])) * glag_ref[...]
    yield

    cos, sa, sb = cos_ref[...], sa_ref[...], sb_ref[...]
    half_t = lax.broadcasted_iota(jnp.int32, (T, LANES), 1) >> 6
    q_rot = _rope(_dot(h, win_ref[:, C_QS:C_QS + SWA_Q_W]), cos, sa, sb) * QK_SCALE
    for hq in range(SWA_HEADS):
        slab = q_rot[:, (hq // 2) * LANES:(hq // 2 + 1) * LANES]
        j = hq // SWA_GROUP
        if hq % 2 != j:
            slab = pltpu.roll(slab, HEAD_DIM, 1)
        qh_s[hq] = jnp.where(half_t == j, slab, 0.0).astype(BF16)
        if hq % 4 == 3:
            yield
    kv = _dot(h, win_ref[:, C_KS:C_KS + 2 * SWA_KV_W])
    k_rot = _rope(kv[:, 0:SWA_KV_W], cos, sa, sb)
    v_swa = kv[:, SWA_KV_W:2 * SWA_KV_W]
    kt_s[...] = k_rot.astype(BF16)
    vt_s[...] = v_swa.T.astype(BF16)
    kout_s[...] = k_rot[T - WINDOW:T, :]
    vout_s[...] = v_swa[T - WINDOW:T, :]


def _mix_sequential(slot, first, layer, sink_ref, x_ref, swag_ref, wout_ref, xo_ref,
                    qb_s, qsw_s, kd_s, klt_s, dect_s, v_s, gate_s, qh_s, kt_s, vt_s,
                    s_s, og_s, cat_s, kcar_s, vcar_s, lhs_s, upd_s, st16_s, sc_s, pr_s, rden_s):
    T = MIX_TILE
    C = GLA_CHUNK
    qb_s, qsw_s, kd_s, klt_s, dect_s, v_s, gate_s, qh_s, kt_s, vt_s = (
        r.at[slot] for r in (qb_s, qsw_s, kd_s, klt_s, dect_s, v_s, gate_s, qh_s, kt_s, vt_s))

    hk_r = lax.broadcasted_iota(jnp.int32, (GLA_QK_W, GLA_QK_W), 0) >> 6
    hk_c = lax.broadcasted_iota(jnp.int32, (GLA_QK_W, GLA_QK_W), 1) >> 6
    bd_k = hk_r == hk_c
    qi = lax.broadcasted_iota(jnp.int32, (C, GLA_QK_W), 0)
    kj = lax.broadcasted_iota(jnp.int32, (C, GLA_QK_W), 1) & (C - 1)
    causal = kj <= qi
    half_c = lax.broadcasted_iota(jnp.int32, (C, LANES), 1) >> 6
    zeros_v = jnp.zeros((C, GLA_DV), BF16)
    n_c = T // C
    for c in range(n_c):
        rows = slice(c * C, (c + 1) * C)
        kd_bd = jnp.where(bd_k, jnp.concatenate([kd_s[rows, :]] * GLA_HEADS, axis=0), 0.0).astype(BF16)
        att = jnp.where(causal, _dot_nt(qb_s[rows, :], kd_bd), 0.0)
        tok = slice((c // 2) * 2 * C, (c // 2 + 1) * 2 * C)
        for hd in range(GLA_HEADS):
            slab = slice((hd // 2) * LANES, (hd // 2 + 1) * LANES)
            hk = slice(hd * GLA_DK, (hd + 1) * GLA_DK)
            v = v_s[rows, hd * GLA_DV:(hd + 1) * GLA_DV]
            lhs_s[c, hd] = jnp.where(half_c == hd % 2, att[:, slab], qsw_s[rows, slab]).astype(BF16)
            v_pair = jnp.concatenate([v, zeros_v] if c % 2 == 0 else [zeros_v, v], axis=0)
            upd_s[c, hk, :] = _dot(klt_s[hk, tok], v_pair)
        if c % 2 == 1:
            yield
    state = s_s[...]
    for c in range(n_c):
        st16_s[c] = state.astype(BF16)
        state = dect_s[:, c:c + 1] * state + upd_s[c]
    s_s[...] = state
    yield
    for c in range(n_c):
        rows = slice(c * C, (c + 1) * C)
        for hd in range(GLA_HEADS):
            hk = slice(hd * GLA_DK, (hd + 1) * GLA_DK)
            v = v_s[rows, hd * GLA_DV:(hd + 1) * GLA_DV]
            st = st16_s[c, hk, :]
            rhs = jnp.concatenate([v, st] if hd % 2 == 0 else [st, v], axis=0)
            og_s[rows, hd * GLA_DV:(hd + 1) * GLA_DV] = _dot(lhs_s[c, hd], rhs)
        if c % 2 == 1:
            yield

    for hd in range(GLA_HEADS):
        vs_ = slice(hd * GLA_DV, (hd + 1) * GLA_DV)
        o = og_s[:, vs_]
        o = o * lax.rsqrt(jnp.mean(o * o, axis=-1, keepdims=True) + EPS)
        cat_s[:, vs_] = (o * gate_s[:, vs_]).astype(BF16)
    yield

    swag = swag_ref[...]
    GR = SWA_GROUP * WINDOW
    kc_ = lax.broadcasted_iota(jnp.int32, (WINDOW, GR), 0)
    qr = lax.broadcasted_iota(jnp.int32, (WINDOW, GR), 1) & (WINDOW - 1)
    from_prev = kc_ > qr
    prev_bias = jnp.where(first > 0.0, -jnp.inf, 0.0)
    n_b = T // WINDOW
    sinks = [jnp.concatenate(
        [jnp.full((1, WINDOW), sink_ref[layer, j * SWA_GROUP + gq], F32) for gq in range(SWA_GROUP)], axis=1)
        for j in range(SWA_KV_HEADS)]
    for i in range(n_b):
        rows = slice(i * WINDOW, (i + 1) * WINDOW)
        if i == 0:
            k_blk = jnp.concatenate([kcar_s[...], kt_s[0:WINDOW, :]], axis=0)
        else:
            k_blk = kt_s[(i - 1) * WINDOW:(i + 1) * WINDOW, :]
        for j in range(SWA_KV_HEADS):
            q4 = jnp.concatenate([qh_s[j * SWA_GROUP + gq, rows, :] for gq in range(SWA_GROUP)], axis=0)
            s2 = _dot_nt(k_blk, q4)
            s_prev = s2[0:WINDOW, :] + prev_bias if i == 0 else s2[0:WINDOW, :]
            sc_s[i, j] = jnp.where(from_prev, s_prev, s2[WINDOW:2 * WINDOW, :])
        yield
    for i in range(n_b):
        for j in range(SWA_KV_HEADS):
            s = sc_s[i, j]
            m = jnp.maximum(jnp.max(s, axis=0, keepdims=True), sinks[j])
            pr = jnp.exp(s - m)
            rden_s[i, j] = 1.0 / (jnp.sum(pr, axis=0, keepdims=True) + jnp.exp(sinks[j] - m))
            pr_s[i, j, 0:WINDOW, :] = jnp.where(from_prev, pr, 0.0).astype(BF16)
            pr_s[i, j, WINDOW:2 * WINDOW, :] = jnp.where(from_prev, 0.0, pr).astype(BF16)
        yield
    for i in range(n_b):
        rows = slice(i * WINDOW, (i + 1) * WINDOW)
        if i == 0:
            vt_blk = jnp.concatenate([vcar_s[...], vt_s[:, 0:WINDOW]], axis=1)
        else:
            vt_blk = vt_s[:, (i - 1) * WINDOW:(i + 1) * WINDOW]
        pieces = []
        for j in range(SWA_KV_HEADS):
            vt_j = vt_blk[j * HEAD_DIM:(j + 1) * HEAD_DIM, :]
            o_t = _dot(vt_j, pr_s[i, j]) * rden_s[i, j]
            pieces.extend(o_t[:, gq * WINDOW:(gq + 1) * WINDOW] for gq in range(SWA_GROUP))
        o_s = jnp.concatenate(pieces, axis=0).T
        cat_s[rows, GLA_V_W:GLA_V_W + SWA_Q_W] = _rmsnorm(o_s, swag).astype(BF16)
        yield

    xo_ref[...] = (x_ref[...] + _dot(cat_s[:, 0:GLA_V_W], wout_ref[0:GLA_V_W, :])
                   + _dot(cat_s[:, GLA_V_W:D_MODEL], wout_ref[GLA_V_W:D_MODEL, :]))

    kcar_s[...] = kt_s[T - WINDOW:T, :]
    vcar_s[...] = vt_s[:, T - WINDOW:T]


N_MIX_OUT = 4
N_MIX_SCRATCH = 23


def _mix_prompt_kernel(*refs, layer, n_l, n_tiles, n_prev, sides):
    (sink_ref, xn_ref, xv_ref, nmix_ref, win_ref, wa2_ref, ba_ref, glag_ref, swag_ref,
     wout_ref, cos_ref, sa_ref, sb_ref) = refs[:N_MIX_IN]
    n_side = len(sides)
    side_src = refs[N_MIX_IN + n_prev:N_MIX_IN + n_prev + n_side]
    outs = refs[N_MIX_IN + n_prev + n_side:len(refs) - N_MIX_SCRATCH]
    xo_ref, so_ref, ko_ref, vo_ref = outs[:N_MIX_OUT]
    (qb_s, qsw_s, kd_s, klt_s, dect_s, v_s, gate_s, qh_s, kt_s, vt_s, kout_s, vout_s,
     s_s, og_s, cat_s, kcar_s, vcar_s, lhs_s, upd_s, st16_s, sc_s, pr_s, rden_s) = refs[-N_MIX_SCRATCH:]
    handover = (qb_s, qsw_s, kd_s, klt_s, dect_s, v_s, gate_s, qh_s, kt_s, vt_s)
    t = pl.program_id(0)
    l_seq = jnp.maximum(t - 1, 0) % n_l
    _run_side_casts(t, sides, side_src, outs[N_MIX_OUT:])

    @pl.when(t == 0)
    def _():
        for r in handover:
            r[1] = jnp.zeros(r.shape[1:], r.dtype)
        kcar_s[...] = jnp.zeros_like(kcar_s)
        vcar_s[...] = jnp.zeros_like(vcar_s)

    @pl.when(l_seq == 0)
    def _():
        s_s[...] = jnp.zeros_like(s_s)

    first = jnp.where(l_seq == 0, 1.0, 0.0)

    def step(slot_proj, slot_seq):
        proj = _mix_project(slot_proj, xn_ref, nmix_ref, win_ref, wa2_ref, ba_ref, glag_ref, cos_ref,
                            sa_ref, sb_ref, *handover, kout_s, vout_s)
        seq = _mix_sequential(slot_seq, first, layer, sink_ref, xv_ref, swag_ref, wout_ref, xo_ref,
                              *handover, s_s, og_s, cat_s, kcar_s, vcar_s, lhs_s, upd_s, st16_s, sc_s, pr_s, rden_s)
        for _ in itertools.zip_longest(seq, proj):
            pass

        @pl.when(jnp.logical_and(t > 0, l_seq == n_l - 1))
        def _():
            for hd in range(GLA_HEADS):
                so_ref[hd] = s_s[hd * GLA_DK:(hd + 1) * GLA_DK, :]
            ko_ref[...] = kout_s[slot_seq]
            vo_ref[...] = vout_s[slot_seq]

    @pl.when(t % 2 == 0)
    def _():
        step(0, 1)

    @pl.when(t % 2 == 1)
    def _():
        step(1, 0)


def _mixer_weight_specs(layer):
    at_layer = lambda i: (layer, 0, 0)
    const = lambda i: (0, 0)
    return [
        pl.BlockSpec((None, 1, D_MODEL), at_layer),
        pl.BlockSpec((D_MODEL, IN_W_PAD), const),
        pl.BlockSpec((None, LANES, GLA_QK_W), at_layer),
        pl.BlockSpec((None, 1, GLA_QK_W), at_layer),
        pl.BlockSpec((None, 1, GLA_V_W), at_layer),
        pl.BlockSpec((None, 1, SWA_Q_W), at_layer),
        pl.BlockSpec((D_MODEL, D_MODEL), const),
    ]


def _mix_prompt(x, sinks, mix_w, tabs, layer, depth, prev, sides=()):
    B, L, _ = x.shape
    T = MIX_TILE
    n_l = L // T
    n_tiles = B * n_l
    t_proj = lambda t: jnp.minimum(t, n_tiles - 1)
    t_seq = lambda t: jnp.maximum(t - 1, 0)
    tab_spec = pl.BlockSpec((T, LANES), lambda t: (t_proj(t) % n_l, 0))
    n_prev = 0 if prev is None else len(prev)
    side_in, side_out, side_shapes = _side_cast_specs(sides)
    return pl.pallas_call(
        functools.partial(_mix_prompt_kernel, layer=layer, n_l=n_l, n_tiles=n_tiles, n_prev=n_prev,
                          sides=tuple((s.n_blocks, s.reorder_w_in) for s in sides)),
        grid=(n_tiles + 1,),
        in_specs=[
            pl.BlockSpec(memory_space=pltpu.SMEM),
            pl.BlockSpec((None, T, D_MODEL), lambda t: (t_proj(t) // n_l, t_proj(t) % n_l, 0)),
            pl.BlockSpec((None, T, D_MODEL), lambda t: (t_seq(t) // n_l, t_seq(t) % n_l, 0)),
            *_mixer_weight_specs(layer),
            tab_spec, tab_spec, tab_spec,
            *([pl.BlockSpec(memory_space=pl.ANY)] * n_prev),
            *side_in,
        ],
        out_specs=[
            pl.BlockSpec((None, T, D_MODEL), lambda t: (t_seq(t) // n_l, t_seq(t) % n_l, 0)),
            pl.BlockSpec((None, None, GLA_HEADS, GLA_DK, GLA_DV), lambda t: (layer, t_seq(t) // n_l, 0, 0, 0)),
            pl.BlockSpec((None, None, WINDOW, SWA_KV_W), lambda t: (layer, t_seq(t) // n_l, 0, 0)),
            pl.BlockSpec((None, None, WINDOW, SWA_KV_W), lambda t: (layer, t_seq(t) // n_l, 0, 0)),
            *side_out,
        ],
        out_shape=[
            jax.ShapeDtypeStruct((B, L, D_MODEL), F32),
            jax.ShapeDtypeStruct((depth, B, GLA_HEADS, GLA_DK, GLA_DV), F32),
            jax.ShapeDtypeStruct((depth, B, WINDOW, SWA_KV_W), F32),
            jax.ShapeDtypeStruct((depth, B, WINDOW, SWA_KV_W), F32),
            *side_shapes,
        ],
        input_output_aliases={N_MIX_IN + k: 1 + k for k in range(n_prev)},
        scratch_shapes=[
            pltpu.VMEM((2, T, GLA_QK_W), BF16),
            pltpu.VMEM((2, T, GLA_QK_W), F32),
            pltpu.VMEM((2, T, GLA_QK_W), F32),
            pltpu.VMEM((2, GLA_QK_W, T), BF16),
            pltpu.VMEM((2, GLA_QK_W, LANES), F32),
            pltpu.VMEM((2, T, GLA_V_W), BF16),
            pltpu.VMEM((2, T, GLA_V_W), F32),
            pltpu.VMEM((2, SWA_HEADS, T, LANES), BF16),
            pltpu.VMEM((2, T, SWA_KV_W), BF16),
            pltpu.VMEM((2, SWA_KV_W, T), BF16),
            pltpu.VMEM((2, WINDOW, SWA_KV_W), F32),
            pltpu.VMEM((2, WINDOW, SWA_KV_W), F32),
            pltpu.VMEM((GLA_QK_W, GLA_DV), F32),
            pltpu.VMEM((T, GLA_V_W), F32),
            pltpu.VMEM((T, D_MODEL), BF16),
            pltpu.VMEM((WINDOW, SWA_KV_W), BF16),
            pltpu.VMEM((SWA_KV_W, WINDOW), BF16),
            pltpu.VMEM((T // GLA_CHUNK, GLA_HEADS, GLA_CHUNK, LANES), BF16),
            pltpu.VMEM((T // GLA_CHUNK, GLA_QK_W, GLA_DV), F32),
            pltpu.VMEM((T // GLA_CHUNK, GLA_QK_W, GLA_DV), BF16),
            pltpu.VMEM((T // WINDOW, SWA_KV_HEADS, WINDOW, SWA_GROUP * WINDOW), F32),
            pltpu.VMEM((T // WINDOW, SWA_KV_HEADS, 2 * WINDOW, SWA_GROUP * WINDOW), BF16),
            pltpu.VMEM((T // WINDOW, SWA_KV_HEADS, 1, SWA_GROUP * WINDOW), F32),
        ],
        compiler_params=pltpu.CompilerParams(
            dimension_semantics=("arbitrary",), vmem_limit_bytes=VMEM_LIMIT),
        name="mix_prompt",
    )(sinks, x, x, *mix_w, *tabs, *(prev or ()), *(s.src for s in sides))


N_SMP_IN = 15


def _mix_sample_kernel(*refs, layer):
    (sink_ref, x_ref, s_ref, kc_ref, vc_ref, nmix_ref, win_ref, wa2_ref, ba_ref,
     glag_ref, swag_ref, wout_ref, cos_ref, sa_ref, sb_ref) = refs[:N_SMP_IN]
    xo_ref, so_ref, ko_ref, vo_ref = refs[-4:]
    R = x_ref.shape[0]
    NS = SMP_SEQS
    Q = R // NS
    q_shift = Q.bit_length() - 1

    x = x_ref[...]
    h = _rmsnorm(x, nmix_ref[...]).astype(BF16)
    proj = _dot(h, win_ref[...])
    z = _dot(proj[:, C_A:C_A + LANES].astype(BF16), wa2_ref[...]) + ba_ref[...]
    g = _log_sigmoid(z) * (1.0 / GATE_NORM)

    ri = lax.broadcasted_iota(jnp.int32, (R, R), 0)
    ci = lax.broadcasted_iota(jnp.int32, (R, R), 1)
    same = (ri >> q_shift) == (ci >> q_shift)
    causal = jnp.logical_and(same, ci <= ri)
    after = jnp.logical_and(same, ci > ri)
    b = _split_dot(jnp.where(causal, 1.0, 0.0).astype(BF16), g)
    rest = _split_dot(jnp.where(after, 1.0, 0.0).astype(BF16), g)

    qg = proj[:, C_QG:C_QG + GLA_QK_W]
    kg = proj[:, C_KG:C_KG + GLA_QK_W]
    qb = qg * QK_SCALE * jnp.exp(b)
    kd = (kg * jnp.exp(-b)).astype(BF16)
    kl_t = (kg * jnp.exp(rest)).T
    g_t = g.T

    own_tok = ((lax.broadcasted_iota(jnp.int32, (NS * GLA_DK, R), 0) >> 6)
               == (lax.broadcasted_iota(jnp.int32, (NS * GLA_DK, R), 1) >> q_shift))
    own_seq = ((lax.broadcasted_iota(jnp.int32, (R, NS * GLA_DK), 1) >> 6)
               == (lax.broadcasted_iota(jnp.int32, (R, NS * GLA_DK), 0) >> q_shift))
    lane_half = lax.broadcasted_iota(jnp.int32, (R, LANES), 1) >> 6
    ones = jnp.ones((R, GLA_DV), BF16)

    glag = glag_ref[...]
    og = []
    for hd in range(GLA_HEADS):
        ks_ = slice(hd * GLA_DK, (hd + 1) * GLA_DK)
        v = proj[:, C_VG + hd * GLA_DV:C_VG + (hd + 1) * GLA_DV].astype(BF16)
        att = jnp.where(causal, _dot_nt(qb[:, ks_].astype(BF16), kd[:, ks_]), 0.0).astype(BF16)
        o = _dot(att, v)
        s_old = s_ref[:, hd].reshape(NS * GLA_DK, GLA_DV)
        slab = qb[:, (hd // 2) * LANES:(hd // 2 + 1) * LANES]
        dup = jnp.where(lane_half == hd % 2, slab, pltpu.roll(slab, HEAD_DIM, 1))
        q_exp = jnp.where(own_seq, jnp.concatenate([dup] * (NS // 2), axis=1), 0.0).astype(BF16)
        o = o + _dot(q_exp, s_old.astype(BF16))
        o = o * lax.rsqrt(jnp.mean(o * o, axis=-1, keepdims=True) + EPS)
        rg = proj[:, C_RG + hd * GLA_DV:C_RG + (hd + 1) * GLA_DV]
        og.append(o * glag[:, hd * GLA_DV:(hd + 1) * GLA_DV] * _silu(rg))
        kl_exp = jnp.where(own_tok, jnp.concatenate([kl_t[ks_, :]] * NS, axis=0), 0.0).astype(BF16)
        g_exp = jnp.where(own_tok, jnp.concatenate([g_t[ks_, :]] * NS, axis=0), 0.0)
        b_last = _split_dot_rhs(g_exp, ones)
        s_new = jnp.exp(b_last) * s_old + _dot(kl_exp, v)
        so_ref[:, hd] = s_new.reshape(NS, GLA_DK, GLA_DV)

    cos, sa, sb = cos_ref[...], sa_ref[...], sb_ref[...]
    q_rot = _rope(proj[:, C_QS:C_QS + SWA_Q_W], cos, sa, sb) * QK_SCALE
    k_rot = _rope(proj[:, C_KS:C_KS + SWA_KV_W], cos, sa, sb)
    v_new = proj[:, C_VS:C_VS + SWA_KV_W]

    W = kc_ref.shape[1]
    SR = SMP_SUB * Q
    GR = SWA_GROUP * SR
    sub_shift = SR.bit_length() - 1
    w_shift = W.bit_length() - 1
    r_c = lax.broadcasted_iota(jnp.int32, (GR, SMP_SUB * W), 0)
    c_c = lax.broadcasted_iota(jnp.int32, (GR, SMP_SUB * W), 1)
    vis_c = jnp.logical_and(((r_c & (SR - 1)) >> q_shift) == (c_c >> w_shift),
                            (c_c & (W - 1)) > (r_c & (Q - 1)))
    r_n = lax.broadcasted_iota(jnp.int32, (GR, SR), 0)
    c_n = lax.broadcasted_iota(jnp.int32, (GR, SR), 1)
    vis_n = jnp.logical_and(((r_n & (SR - 1)) >> q_shift) == (c_n >> q_shift),
                            (c_n & (Q - 1)) <= (r_n & (Q - 1)))
    grp_row = lax.broadcasted_iota(jnp.int32, (GR, 1), 0) >> sub_shift
    half_sr = lax.broadcasted_iota(jnp.int32, (SR, LANES), 1) >> 6
    half_gr = lax.broadcasted_iota(jnp.int32, (GR, LANES), 1) >> 6

    o_rows = []
    for sg in range(NS // SMP_SUB):
        rows = slice(sg * SR, (sg + 1) * SR)
        k_c = kc_ref[sg * SMP_SUB:(sg + 1) * SMP_SUB].reshape(SMP_SUB * W, SWA_KV_W).astype(BF16)
        v_c = vc_ref[sg * SMP_SUB:(sg + 1) * SMP_SUB].reshape(SMP_SUB * W, SWA_KV_W).astype(BF16)
        k_n = k_rot[rows, :].astype(BF16)
        v_n = v_new[rows, :].astype(BF16)
        o_kv = []
        for j in range(SWA_KV_HEADS):
            parts = []
            for gq in range(SWA_GROUP):
                hq = j * SWA_GROUP + gq
                slab = q_rot[rows, (hq // 2) * LANES:(hq // 2 + 1) * LANES]
                if hq % 2 != j:
                    slab = pltpu.roll(slab, HEAD_DIM, 1)
                parts.append(jnp.where(half_sr == j, slab, 0.0))
            q8 = jnp.concatenate(parts, axis=0).astype(BF16)
            s_c = jnp.where(vis_c, _dot_nt(q8, k_c), -jnp.inf)
            s_n = jnp.where(vis_n, _dot_nt(q8, k_n), -jnp.inf)
            sink = jnp.zeros((GR, 1), F32)
            for gq in range(SWA_GROUP):
                sink = jnp.where(grp_row == gq, sink_ref[layer, j * SWA_GROUP + gq], sink)
            m = jnp.maximum(jnp.maximum(jnp.max(s_c, axis=-1, keepdims=True),
                                        jnp.max(s_n, axis=-1, keepdims=True)), sink)
            p_c = jnp.exp(s_c - m)
            p_n = jnp.exp(s_n - m)
            den = (jnp.sum(p_c, axis=-1, keepdims=True) + jnp.sum(p_n, axis=-1, keepdims=True)
                   + jnp.exp(sink - m))
            o = _dot((p_c / den).astype(BF16), v_c) + _dot((p_n / den).astype(BF16), v_n)
            o_kv.append(o)
        slabs = []
        for p in range(SWA_HEADS // 2):
            halves = []
            for hq in (2 * p, 2 * p + 1):
                j, gq = hq // SWA_GROUP, hq % SWA_GROUP
                piece = o_kv[j][gq * SR:(gq + 1) * SR, :]
                if hq % 2 != j:
                    piece = pltpu.roll(piece, HEAD_DIM, 1)
                halves.append(piece)
            slabs.append(jnp.where(half_sr == 0, halves[0], halves[1]))
        o_rows.append(jnp.concatenate(slabs, axis=1))
    o_s = _rmsnorm(jnp.concatenate(o_rows, axis=0), swag_ref[...])

    cat = jnp.concatenate(og + [o_s], axis=1).astype(BF16)
    xo_ref[...] = x + _dot(cat, wout_ref[...])

    ko_ref[:, 0:W - Q, :] = kc_ref[:, Q:W, :]
    vo_ref[:, 0:W - Q, :] = vc_ref[:, Q:W, :]
    for s in range(NS):
        ko_ref[s, W - Q:W, :] = k_rot[s * Q:(s + 1) * Q, :]
        vo_ref[s, W - Q:W, :] = v_new[s * Q:(s + 1) * Q, :]


def _split_dot_rhs(g, m_bf16):
    hi = g.astype(BF16)
    lo = (g - hi.astype(F32)).astype(BF16)
    return _dot(hi, m_bf16) + _dot(lo, m_bf16)


def _mix_sample(x2d, state, kc, vc, sinks, mix_w, tabs, layer, prev):
    depth, NB, W = kc.shape[0], kc.shape[1], kc.shape[2]
    Q = x2d.shape[0] // NB
    R = SMP_SEQS * Q
    tab_spec = pl.BlockSpec((R, LANES), lambda i: (0, 0))
    st_spec = pl.BlockSpec((None, SMP_SEQS, GLA_HEADS, GLA_DK, GLA_DV), lambda i: (layer, i, 0, 0, 0))
    kv_spec = pl.BlockSpec((None, SMP_SEQS, W, SWA_KV_W), lambda i: (layer, i, 0, 0))
    n_prev = 0 if prev is None else len(prev)
    return pl.pallas_call(
        functools.partial(_mix_sample_kernel, layer=layer),
        grid=(NB // SMP_SEQS,),
        in_specs=[
            pl.BlockSpec(memory_space=pltpu.SMEM),
            pl.BlockSpec((R, D_MODEL), lambda i: (i, 0)),
            st_spec, kv_spec, kv_spec,
            *_mixer_weight_specs(layer),
            tab_spec, tab_spec, tab_spec,
            *([pl.BlockSpec(memory_space=pl.ANY)] * n_prev),
        ],
        out_specs=[pl.BlockSpec((R, D_MODEL), lambda i: (i, 0)), st_spec, kv_spec, kv_spec],
        out_shape=[
            jax.ShapeDtypeStruct(x2d.shape, F32),
            jax.ShapeDtypeStruct(state.shape, F32),
            jax.ShapeDtypeStruct(kc.shape, F32),
            jax.ShapeDtypeStruct(vc.shape, F32),
        ],
        input_output_aliases={N_SMP_IN + k: 1 + k for k in range(n_prev)},
        compiler_params=pltpu.CompilerParams(
            dimension_semantics=("arbitrary",), vmem_limit_bytes=VMEM_LIMIT),
        name="mix_sample",
    )(sinks, x2d, state, kc, vc, *mix_w, *tabs, *(prev or ()))


def _rope_tables(pos):
    half = ROT_DIM // 2
    inv = jnp.power(jnp.float32(ROPE_THETA), -jnp.arange(half, dtype=jnp.float32) * (2.0 / ROT_DIM))
    ang = pos.astype(jnp.float32)[:, None] * inv[None, :]
    cos, sin = jnp.cos(ang), jnp.sin(ang)
    n = pos.shape[0]
    pad = jnp.zeros((n, HEAD_DIM - ROT_DIM), F32)
    zero = jnp.zeros((n, half), F32)
    c64 = jnp.concatenate([cos, cos, pad + 1.0], axis=1)
    a64 = jnp.concatenate([zero, sin, pad], axis=1)
    b64 = jnp.concatenate([-sin, zero, pad], axis=1)
    rep = LANES // HEAD_DIM
    return tuple(jnp.tile(t, (1, rep)) for t in (c64, a64, b64))


def kernel(x_prompt, x_sample, state_gla, cache_swa_k, cache_swa_v, norm_ffn1, w_ffn1_gu, w_ffn1_down,
           norm_mix, w_in, w_gate_up, b_gate, gla_norm, attn_sinks, swa_norm, w_out, norm_ffn2,
           w_ffn2_gu, w_ffn2_down, norm_final):
    B, L, _ = x_prompt.shape
    NB, Q, _ = x_sample.shape
    depth = w_in.shape[0]
    W = cache_swa_k.shape[2]

    tabs_p = _rope_tables(jnp.arange(L))
    tabs_s = _rope_tables(jnp.tile(PAST_LEN + jnp.arange(Q), SMP_SEQS))

    wa2 = jnp.pad(w_gate_up, ((0, 0), (0, LANES - GATE_RANK), (0, 0))).astype(BF16)
    row = lambda p: p.reshape(depth, 1, -1)
    n1, n2 = row(norm_ffn1), row(norm_ffn2)
    gu_rows = D_MODEL // (B * L // FFN_TILE)
    dn_rows = D_FF // 11

    def ffn_sides(w_gu, w_down, layer):
        return (SideCast(w_gu, layer, gu_rows), SideCast(w_down, layer, dn_rows))

    xp = x_prompt.reshape(B * L, D_MODEL)
    xs = x_sample.reshape(NB * Q, D_MODEL)
    kc = cache_swa_k.reshape(depth, NB, W, SWA_KV_W)
    vc = cache_swa_v.reshape(depth, NB, W, SWA_KV_W)
    out_p = out_s = None
    gu1, dn1 = _cast_layers((SideCast(w_ffn1_gu, 0, D_MODEL // 8), SideCast(w_ffn1_down, 0, D_FF // 4)))
    for l in range(depth):
        nf = norm_final if l == depth - 1 else None
        xp, win, wout = _ffn(xp, n1, gu1, dn1, l, sides=(SideCast(w_in, l, gu_rows, True),
                                                        SideCast(w_out, l, gu_rows)))
        xs = _ffn(xs, n1, gu1, dn1, l)
        mix_w = (row(norm_mix), win, wa2, row(b_gate), row(gla_norm), row(swa_norm), wout)
        xp3, *out_p = _mix_prompt(xp.reshape(B, L, D_MODEL), attn_sinks, mix_w, tabs_p, l, depth, out_p,
                                  sides=ffn_sides(w_ffn2_gu, w_ffn2_down, l))
        gu2, dn2 = out_p[3:]
        out_p = out_p[:3]
        xs, *out_s = _mix_sample(xs, state_gla, kc, vc, attn_sinks, mix_w, tabs_s, l, out_s)
        if l + 1 < depth:
            xp, gu1_next, dn1_next = _ffn(xp3.reshape(B * L, D_MODEL), n2, gu2, dn2, l, nf,
                                          sides=ffn_sides(w_ffn1_gu, w_ffn1_down, l + 1))
        else:
            xp = _ffn(xp3.reshape(B * L, D_MODEL), n2, gu2, dn2, l, nf)
        xs = _ffn(xs, n2, gu2, dn2, l, nf)
        if l + 1 < depth:
            gu1, dn1 = gu1_next, dn1_next

    gla_p, k_p, v_p = out_p
    gla_s, k_s, v_s = out_s
    kv5 = lambda t: t.reshape(*t.shape[:3], SWA_KV_HEADS, HEAD_DIM)
    return (xp.reshape(B, L, D_MODEL), xs.reshape(NB, Q, D_MODEL), gla_p, kv5(k_p), kv5(v_p),
            gla_s, kv5(k_s), kv5(v_s))
```

```python
import functools
import itertools
from typing import NamedTuple

import jax
import jax.numpy as jnp
import numpy as np
from jax import lax
from jax.experimental import pallas as pl
from jax.experimental.pallas import tpu as pltpu

F32 = jnp.float32
BF16 = jnp.bfloat16

D_MODEL = 1024
D_FF = 2816
EPS = 1e-6
GLA_HEADS = 4
GLA_DK = 64
GLA_DV = 128
GLA_QK_W = GLA_HEADS * GLA_DK
GLA_V_W = GLA_HEADS * GLA_DV
GATE_RANK = 16
GATE_NORM = 16.0
GLA_CHUNK = 64
HEAD_DIM = 64
SWA_HEADS = 8
SWA_KV_HEADS = 2
SWA_GROUP = SWA_HEADS // SWA_KV_HEADS
SWA_Q_W = SWA_HEADS * HEAD_DIM
SWA_KV_W = SWA_KV_HEADS * HEAD_DIM
WINDOW = 128
ROT_DIM = 16
ROPE_THETA = 500000.0
PAST_LEN = 16384
QK_SCALE = 0.125

LANES = 128
C_QG, C_KG, C_VG, C_RG = 0, 256, 512, 1024
C_QS, C_KS, C_VS, C_A = 1536, 2048, 2176, 2304
IN_W_PAD = C_A + LANES

FFN_TILE = 512
FFN_CHUNK = 256
MIX_TILE = 512
SMP_SEQS = 32
SMP_SUB = 8
VMEM_LIMIT = 56 * 1024 * 1024

NT_DIMS = (((1,), (1,)), ((), ()))
TN_DIMS = (((0,), (0,)), ((), ()))


def _dot(a, b):
    return jnp.dot(a, b, preferred_element_type=F32)


def _dot_nt(a, b):
    return lax.dot_general(a, b, NT_DIMS, preferred_element_type=F32)


def _dot_tn(a, b):
    return lax.dot_general(a, b, TN_DIMS, preferred_element_type=F32)


def _rmsnorm(x, g):
    return x * lax.rsqrt(jnp.mean(x * x, axis=-1, keepdims=True) + EPS) * g


def _log_sigmoid(z):
    return jnp.minimum(z, 0.0) - jnp.log1p(jnp.exp(-jnp.abs(z)))


def _silu(z):
    return z * jax.nn.sigmoid(z)


def _split_dot(m_bf16, g):
    hi = g.astype(BF16)
    lo = (g - hi.astype(F32)).astype(BF16)
    return _dot(m_bf16, hi) + _dot(m_bf16, lo)


def _rope(x, cos, sa, sb):
    half = ROT_DIM // 2
    outs = []
    for p in range(x.shape[1] // LANES):
        s = x[:, p * LANES:(p + 1) * LANES]
        outs.append(s * cos + pltpu.roll(s, half, 1) * sa + pltpu.roll(s, LANES - half, 1) * sb)
    return outs[0] if len(outs) == 1 else jnp.concatenate(outs, axis=1)


class SideCast(NamedTuple):
    src: jax.Array
    layer: int
    block_rows: int
    reorder_w_in: bool = False

    @property
    def n_blocks(self):
        return self.src.shape[1] // self.block_rows

    @property
    def out_cols(self):
        return IN_W_PAD if self.reorder_w_in else self.src.shape[2]


def _cast_block(src_ref, dst_ref, reorder_w_in):
    w = src_ref[...]
    if not reorder_w_in:
        dst_ref[...] = w.astype(BF16)
        return
    n_a = C_RG + GLA_V_W
    dst_ref[:, 0:n_a] = w[:, 0:n_a].astype(BF16)
    dst_ref[:, n_a:C_A] = w[:, n_a + GATE_RANK:].astype(BF16)
    pad = jnp.zeros((w.shape[0], LANES - GATE_RANK), F32)
    dst_ref[:, C_A:IN_W_PAD] = jnp.concatenate([w[:, n_a:n_a + GATE_RANK], pad], axis=1).astype(BF16)


def _side_cast_specs(sides):
    in_specs, out_specs, out_shapes = [], [], []
    for s in sides:
        last = s.n_blocks - 1
        in_specs.append(pl.BlockSpec((None, s.block_rows, s.src.shape[2]),
                                     lambda i, layer=s.layer, last=last: (layer, jnp.minimum(i, last), 0)))
        out_specs.append(pl.BlockSpec((s.block_rows, s.out_cols), lambda i, last=last: (jnp.minimum(i, last), 0)))
        out_shapes.append(jax.ShapeDtypeStruct((s.src.shape[1], s.out_cols), BF16))
    return in_specs, out_specs, out_shapes


def _run_side_casts(step, sides, src_refs, dst_refs):
    for (n_blocks, reorder), src, dst in zip(sides, src_refs, dst_refs):
        @pl.when(step < n_blocks)
        def _():
            _cast_block(src, dst, reorder)


def _cast_kernel(*refs, sides):
    n = len(sides)
    _run_side_casts(pl.program_id(0), sides, refs[:n], refs[n:])


def _cast_layers(sides):
    in_specs, out_specs, out_shapes = _side_cast_specs(sides)
    return pl.pallas_call(
        functools.partial(_cast_kernel, sides=tuple((s.n_blocks, s.reorder_w_in) for s in sides)),
        grid=(max(s.n_blocks for s in sides),),
        in_specs=in_specs,
        out_specs=out_specs,
        out_shape=out_shapes,
        compiler_params=pltpu.CompilerParams(
            dimension_semantics=("arbitrary",), vmem_limit_bytes=VMEM_LIMIT),
        name="cast_weights",
    )(*(s.src for s in sides))


def _ffn_kernel(*refs, final, n_tiles, sides):
    n_in = 7 if final else 6
    n_side = len(sides)
    xs_ref, xn_ref, xv_ref, n_ref, wgu_ref, wd_ref = refs[:6]
    nf_ref = refs[6] if final else None
    side_src = refs[n_in:n_in + n_side]
    os_ref, op_ref = refs[n_in + n_side:n_in + n_side + 2]
    side_dst = refs[n_in + n_side + 2:n_in + 2 * n_side + 2]
    acc_ref, h_ref = refs[-2:]
    i = pl.program_id(0)
    _run_side_casts(i, sides, side_src, side_dst)

    def epilogue(x_ref, y_ref):
        y = x_ref[...] + 0.5 * acc_ref[...]
        if final:
            y = _rmsnorm(y, nf_ref[...])
        y_ref[...] = y

    def step(cur, nxt, sample_epilogue=False):
        if sample_epilogue:
            epilogue(xs_ref, os_ref)
        else:
            epilogue(xv_ref, op_ref)
        h_ref[nxt] = _rmsnorm(xn_ref[...], n_ref[...]).astype(BF16)
        h = h_ref[cur]
        for c in range(D_FF // FFN_CHUNK):
            lo = c * FFN_CHUNK
            gate = _dot(h, wgu_ref[:, lo:lo + FFN_CHUNK])
            up = _dot(h, wgu_ref[:, D_FF + lo:D_FF + lo + FFN_CHUNK])
            act = (_silu(gate) * up).astype(BF16)
            part = _dot(act, wd_ref[lo:lo + FFN_CHUNK, :])
            if c == 0:
                acc_ref[...] = part
            else:
                acc_ref[...] += part

    @pl.when(i == 0)
    def _():
        acc_ref[...] = jnp.zeros_like(acc_ref)
        h_ref[0] = _rmsnorm(xs_ref[...], n_ref[...]).astype(BF16)

    @pl.when(jnp.logical_and(i < n_tiles, i % 2 == 0))
    def _():
        step(0, 1)

    @pl.when(i == 1)
    def _():
        step(1, 0, sample_epilogue=True)

    @pl.when(jnp.logical_and(jnp.logical_and(i < n_tiles, i % 2 == 1), i != 1))
    def _():
        step(1, 0)

    @pl.when(i == n_tiles)
    def _():
        epilogue(xv_ref, op_ref)


def _ffn(xs2d, xp2d, norm, w_gu, w_down, layer, norm_final=None, sides=()):
    assert xs2d.shape[0] == FFN_TILE
    n_p = xp2d.shape[0] // FFN_TILE
    n_tiles = n_p + 1
    final = norm_final is not None
    const = lambda i: (0, 0)
    nxt_map = lambda i: (jnp.minimum(i, n_p - 1), 0)
    prv_map = lambda i: (jnp.clip(i - 2, 0, n_p - 1), 0)
    tile = (FFN_TILE, D_MODEL)
    in_specs = [
        pl.BlockSpec(tile, const),
        pl.BlockSpec(tile, nxt_map),
        pl.BlockSpec(tile, prv_map),
        pl.BlockSpec((None, 1, D_MODEL), lambda i: (layer, 0, 0)),
        pl.BlockSpec((D_MODEL, 2 * D_FF), const, pipeline_mode=pl.Buffered(1)),
        pl.BlockSpec((D_FF, D_MODEL), const, pipeline_mode=pl.Buffered(1)),
    ]
    args = [xs2d, xp2d, xp2d, norm, w_gu, w_down]
    if final:
        in_specs.append(pl.BlockSpec((1, D_MODEL), const))
        args.append(norm_final.reshape(1, D_MODEL))
    side_in, side_out, side_shapes = _side_cast_specs(sides)
    return pl.pallas_call(
        functools.partial(_ffn_kernel, final=final, n_tiles=n_tiles,
                          sides=tuple((s.n_blocks, s.reorder_w_in) for s in sides)),
        grid=(n_tiles + 1,),
        in_specs=in_specs + side_in,
        out_specs=[pl.BlockSpec(tile, const), pl.BlockSpec(tile, prv_map)] + side_out,
        out_shape=[jax.ShapeDtypeStruct(xs2d.shape, F32), jax.ShapeDtypeStruct(xp2d.shape, F32)] + side_shapes,
        scratch_shapes=[pltpu.VMEM(tile, F32), pltpu.VMEM((2,) + tile, BF16)],
        compiler_params=pltpu.CompilerParams(
            dimension_semantics=("arbitrary",), vmem_limit_bytes=VMEM_LIMIT),
        name="ffn_final" if final else "ffn",
    )(*args, *(s.src for s in sides))


N_MIX_IN = 13


def _mix_project(slot, x_ref, nmix_ref, win_ref, wa2_ref, ba_ref, glag_ref, cos_ref, sa_ref, sb_ref,
                 qb_s, qsw_s, kd_s, klt_s, dect_s, v_s, gate_s, qh_s, kt_s, vt_s, kout_s, vout_s):
    T = MIX_TILE
    G = 256
    qb_s, qsw_s, kd_s, klt_s, dect_s, v_s, gate_s, qh_s, kt_s, vt_s, kout_s, vout_s = (
        r.at[slot] for r in (qb_s, qsw_s, kd_s, klt_s, dect_s, v_s, gate_s, qh_s, kt_s, vt_s, kout_s, vout_s))

    x = x_ref[...]
    h = _rmsnorm(x, nmix_ref[...]).astype(BF16)
    yield

    a_low = _dot(h, win_ref[:, C_A:C_A + LANES]).astype(BF16)
    z = _dot(a_low, wa2_ref[...]) + ba_ref[...]
    g = _log_sigmoid(z) * (1.0 / GATE_NORM)
    yield
    qk = _dot(h, win_ref[:, C_QG:C_QG + 2 * GLA_QK_W])
    yield

    ri = lax.broadcasted_iota(jnp.int32, (G, G), 0)
    ci = lax.broadcasted_iota(jnp.int32, (G, G), 1)
    same_chunk = (ri >> 6) == (ci >> 6)
    incl = jnp.where(jnp.logical_and(same_chunk, ci <= ri), 1.0, 0.0).astype(BF16)
    C = GLA_CHUNK
    chunk_last = []
    for r in range(T // G):
        rows = slice(r * G, (r + 1) * G)
        b = _split_dot(incl, g[rows])
        lasts = [b[c * C + C - 1:(c + 1) * C, :] for c in range(G // C)]
        chunk_last.extend(lasts)
        rest = jnp.concatenate([jnp.broadcast_to(t, (C, GLA_QK_W)) for t in lasts], axis=0) - b
        q = qk[rows, 0:GLA_QK_W]
        k = qk[rows, GLA_QK_W:2 * GLA_QK_W]
        qb = q * QK_SCALE * jnp.exp(b)
        qb_s[rows, :] = qb.astype(BF16)
        qsw_s[rows, :] = jnp.concatenate(
            [pltpu.roll(qb[:, p * LANES:(p + 1) * LANES], GLA_DK, 1) for p in range(GLA_QK_W // LANES)],
            axis=1)
        kd_s[rows, :] = k * jnp.exp(-b)
        klt_s[:, rows] = (k * jnp.exp(rest)).T.astype(BF16)
        yield
    pad = jnp.zeros((LANES - len(chunk_last), GLA_QK_W), F32)
    dect_s[...] = jnp.exp(jnp.concatenate(chunk_last + [pad], axis=0)).T

    v_s[...] = _dot(h, win_ref[:, C_VG:C_VG + GLA_V_W]).astype(BF16)
    yield
    gate_s[...] = _silu(_dot(h, win_ref[:, C_RG:C_RG + GLA_V_W])) * glag_ref[...]
    yield

    cos, sa, sb = cos_ref[...], sa_ref[...], sb_ref[...]
    half_t = lax.broadcasted_iota(jnp.int32, (T, LANES), 1) >> 6
    q_rot = _rope(_dot(h, win_ref[:, C_QS:C_QS + SWA_Q_W]), cos, sa, sb) * QK_SCALE
    for hq in range(SWA_HEADS):
        slab = q_rot[:, (hq // 2) * LANES:(hq // 2 + 1) * LANES]
        j = hq // SWA_GROUP
        if hq % 2 != j:
            slab = pltpu.roll(slab, HEAD_DIM, 1)
        qh_s[hq] = jnp.where(half_t == j, slab, 0.0).astype(BF16)
        if hq % 4 == 3:
            yield
    kv = _dot(h, win_ref[:, C_KS:C_KS + 2 * SWA_KV_W])
    k_rot = _rope(kv[:, 0:SWA_KV_W], cos, sa, sb)
    v_swa = kv[:, SWA_KV_W:2 * SWA_KV_W]
    kt_s[...] = k_rot.astype(BF16)
    vt_s[...] = v_swa.T.astype(BF16)
    kout_s[...] = k_rot[T - WINDOW:T, :]
    vout_s[...] = v_swa[T - WINDOW:T, :]


def _mix_sequential(slot, first, layer, sink_ref, x_ref, swag_ref, wout_ref, xo_ref,
                    qb_s, qsw_s, kd_s, klt_s, dect_s, v_s, gate_s, qh_s, kt_s, vt_s,
                    s_s, og_s, cat_s, kcar_s, vcar_s, lhs_s, upd_s, st16_s, sc_s, pr_s, rden_s):
    T = MIX_TILE
    C = GLA_CHUNK
    qb_s, qsw_s, kd_s, klt_s, dect_s, v_s, gate_s, qh_s, kt_s, vt_s = (
        r.at[slot] for r in (qb_s, qsw_s, kd_s, klt_s, dect_s, v_s, gate_s, qh_s, kt_s, vt_s))

    hk_r = lax.broadcasted_iota(jnp.int32, (GLA_QK_W, GLA_QK_W), 0) >> 6
    hk_c = lax.broadcasted_iota(jnp.int32, (GLA_QK_W, GLA_QK_W), 1) >> 6
    bd_k = hk_r == hk_c
    qi = lax.broadcasted_iota(jnp.int32, (C, GLA_QK_W), 0)
    kj = lax.broadcasted_iota(jnp.int32, (C, GLA_QK_W), 1) & (C - 1)
    causal = kj <= qi
    half_c = lax.broadcasted_iota(jnp.int32, (C, LANES), 1) >> 6
    zeros_v = jnp.zeros((C, GLA_DV), BF16)
    n_c = T // C
    for c in range(n_c):
        rows = slice(c * C, (c + 1) * C)
        kd_bd = jnp.where(bd_k, jnp.concatenate([kd_s[rows, :]] * GLA_HEADS, axis=0), 0.0).astype(BF16)
        att = jnp.where(causal, _dot_nt(qb_s[rows, :], kd_bd), 0.0)
        tok = slice((c // 2) * 2 * C, (c // 2 + 1) * 2 * C)
        for hd in range(GLA_HEADS):
            slab = slice((hd // 2) * LANES, (hd // 2 + 1) * LANES)
            hk = slice(hd * GLA_DK, (hd + 1) * GLA_DK)
            v = v_s[rows, hd * GLA_DV:(hd + 1) * GLA_DV]
            lhs_s[c, hd] = jnp.where(half_c == hd % 2, att[:, slab], qsw_s[rows, slab]).astype(BF16)
            v_pair = jnp.concatenate([v, zeros_v] if c % 2 == 0 else [zeros_v, v], axis=0)
            upd_s[c, hk, :] = _dot(klt_s[hk, tok], v_pair)
        if c % 2 == 1:
            yield
    state = s_s[...]
    for c in range(n_c):
        st16_s[c] = state.astype(BF16)
        state = dect_s[:, c:c + 1] * state + upd_s[c]
    s_s[...] = state
    yield
    for c in range(n_c):
        rows = slice(c * C, (c + 1) * C)
        for hd in range(GLA_HEADS):
            hk = slice(hd * GLA_DK, (hd + 1) * GLA_DK)
            v = v_s[rows, hd * GLA_DV:(hd + 1) * GLA_DV]
            st = st16_s[c, hk, :]
            rhs = jnp.concatenate([v, st] if hd % 2 == 0 else [st, v], axis=0)
            og_s[rows, hd * GLA_DV:(hd + 1) * GLA_DV] = _dot(lhs_s[c, hd], rhs)
        if c % 2 == 1:
            yield

    for hd in range(GLA_HEADS):
        vs_ = slice(hd * GLA_DV, (hd + 1) * GLA_DV)
        o = og_s[:, vs_]
        o = o * lax.rsqrt(jnp.mean(o * o, axis=-1, keepdims=True) + EPS)
        cat_s[:, vs_] = (o * gate_s[:, vs_]).astype(BF16)
    yield

    swag = swag_ref[...]
    GR = SWA_GROUP * WINDOW
    kc_ = lax.broadcasted_iota(jnp.int32, (WINDOW, GR), 0)
    qr = lax.broadcasted_iota(jnp.int32, (WINDOW, GR), 1) & (WINDOW - 1)
    from_prev = kc_ > qr
    prev_bias = jnp.where(first > 0.0, -jnp.inf, 0.0)
    n_b = T // WINDOW
    sinks = [jnp.concatenate(
        [jnp.full((1, WINDOW), sink_ref[layer, j * SWA_GROUP + gq], F32) for gq in range(SWA_GROUP)], axis=1)
        for j in range(SWA_KV_HEADS)]
    for i in range(n_b):
        rows = slice(i * WINDOW, (i + 1) * WINDOW)
        if i == 0:
            k_blk = jnp.concatenate([kcar_s[...], kt_s[0:WINDOW, :]], axis=0)
        else:
            k_blk = kt_s[(i - 1) * WINDOW:(i + 1) * WINDOW, :]
        for j in range(SWA_KV_HEADS):
            q4 = jnp.concatenate([qh_s[j * SWA_GROUP + gq, rows, :] for gq in range(SWA_GROUP)], axis=0)
            s2 = _dot_nt(k_blk, q4)
            s_prev = s2[0:WINDOW, :] + prev_bias if i == 0 else s2[0:WINDOW, :]
            sc_s[i, j] = jnp.where(from_prev, s_prev, s2[WINDOW:2 * WINDOW, :])
        yield
    for i in range(n_b):
        for j in range(SWA_KV_HEADS):
            s = sc_s[i, j]
            m = jnp.maximum(jnp.max(s, axis=0, keepdims=True), sinks[j])
            pr = jnp.exp(s - m)
            rden_s[i, j] = 1.0 / (jnp.sum(pr, axis=0, keepdims=True) + jnp.exp(sinks[j] - m))
            pr_s[i, j, 0:WINDOW, :] = jnp.where(from_prev, pr, 0.0).astype(BF16)
            pr_s[i, j, WINDOW:2 * WINDOW, :] = jnp.where(from_prev, 0.0, pr).astype(BF16)
        yield
    for i in range(n_b):
        rows = slice(i * WINDOW, (i + 1) * WINDOW)
        if i == 0:
            vt_blk = jnp.concatenate([vcar_s[...], vt_s[:, 0:WINDOW]], axis=1)
        else:
            vt_blk = vt_s[:, (i - 1) * WINDOW:(i + 1) * WINDOW]
        pieces = []
        for j in range(SWA_KV_HEADS):
            vt_j = vt_blk[j * HEAD_DIM:(j + 1) * HEAD_DIM, :]
            o_t = _dot(vt_j, pr_s[i, j]) * rden_s[i, j]
            pieces.extend(o_t[:, gq * WINDOW:(gq + 1) * WINDOW] for gq in range(SWA_GROUP))
        o_s = jnp.concatenate(pieces, axis=0).T
        cat_s[rows, GLA_V_W:GLA_V_W + SWA_Q_W] = _rmsnorm(o_s, swag).astype(BF16)
        yield

    xo_ref[...] = (x_ref[...] + _dot(cat_s[:, 0:GLA_V_W], wout_ref[0:GLA_V_W, :])
                   + _dot(cat_s[:, GLA_V_W:D_MODEL], wout_ref[GLA_V_W:D_MODEL, :]))

    kcar_s[...] = kt_s[T - WINDOW:T, :]
    vcar_s[...] = vt_s[:, T - WINDOW:T]


N_MIX_OUT = 4
N_MIX_SCRATCH = 23


def _mix_prompt_kernel(*refs, layer, n_l, n_tiles, n_prev, sides):
    (sink_ref, xn_ref, xv_ref, nmix_ref, win_ref, wa2_ref, ba_ref, glag_ref, swag_ref,
     wout_ref, cos_ref, sa_ref, sb_ref) = refs[:N_MIX_IN]
    n_side = len(sides)
    side_src = refs[N_MIX_IN + n_prev:N_MIX_IN + n_prev + n_side]
    outs = refs[N_MIX_IN + n_prev + n_side:len(refs) - N_MIX_SCRATCH]
    xo_ref, so_ref, ko_ref, vo_ref = outs[:N_MIX_OUT]
    (qb_s, qsw_s, kd_s, klt_s, dect_s, v_s, gate_s, qh_s, kt_s, vt_s, kout_s, vout_s,
     s_s, og_s, cat_s, kcar_s, vcar_s, lhs_s, upd_s, st16_s, sc_s, pr_s, rden_s) = refs[-N_MIX_SCRATCH:]
    handover = (qb_s, qsw_s, kd_s, klt_s, dect_s, v_s, gate_s, qh_s, kt_s, vt_s)
    t = pl.program_id(0)
    l_seq = jnp.maximum(t - 1, 0) % n_l
    _run_side_casts(t, sides, side_src, outs[N_MIX_OUT:])

    @pl.when(t == 0)
    def _():
        for r in handover:
            r[1] = jnp.zeros(r.shape[1:], r.dtype)
        kcar_s[...] = jnp.zeros_like(kcar_s)
        vcar_s[...] = jnp.zeros_like(vcar_s)

    @pl.when(l_seq == 0)
    def _():
        s_s[...] = jnp.zeros_like(s_s)

    first = jnp.where(l_seq == 0, 1.0, 0.0)

    def step(slot_proj, slot_seq):
        proj = _mix_project(slot_proj, xn_ref, nmix_ref, win_ref, wa2_ref, ba_ref, glag_ref, cos_ref,
                            sa_ref, sb_ref, *handover, kout_s, vout_s)
        seq = _mix_sequential(slot_seq, first, layer, sink_ref, xv_ref, swag_ref, wout_ref, xo_ref,
                              *handover, s_s, og_s, cat_s, kcar_s, vcar_s, lhs_s, upd_s, st16_s, sc_s, pr_s, rden_s)
        for _ in itertools.zip_longest(seq, proj):
            pass

        @pl.when(jnp.logical_and(t > 0, l_seq == n_l - 1))
        def _():
            for hd in range(GLA_HEADS):
                so_ref[hd] = s_s[hd * GLA_DK:(hd + 1) * GLA_DK, :]
            ko_ref[...] = kout_s[slot_seq]
            vo_ref[...] = vout_s[slot_seq]

    @pl.when(t % 2 == 0)
    def _():
        step(0, 1)

    @pl.when(t % 2 == 1)
    def _():
        step(1, 0)


def _mixer_weight_specs(layer):
    at_layer = lambda i: (layer, 0, 0)
    const = lambda i: (0, 0)
    return [
        pl.BlockSpec((None, 1, D_MODEL), at_layer),
        pl.BlockSpec((D_MODEL, IN_W_PAD), const),
        pl.BlockSpec((None, LANES, GLA_QK_W), at_layer),
        pl.BlockSpec((None, 1, GLA_QK_W), at_layer),
        pl.BlockSpec((None, 1, GLA_V_W), at_layer),
        pl.BlockSpec((None, 1, SWA_Q_W), at_layer),
        pl.BlockSpec((D_MODEL, D_MODEL), const),
    ]


def _mix_prompt(x, sinks, mix_w, tabs, layer, depth, prev, sides=()):
    B, L, _ = x.shape
    T = MIX_TILE
    n_l = L // T
    n_tiles = B * n_l
    t_proj = lambda t: jnp.minimum(t, n_tiles - 1)
    t_seq = lambda t: jnp.maximum(t - 1, 0)
    tab_spec = pl.BlockSpec((T, LANES), lambda t: (t_proj(t) % n_l, 0))
    n_prev = 0 if prev is None else len(prev)
    side_in, side_out, side_shapes = _side_cast_specs(sides)
    return pl.pallas_call(
        functools.partial(_mix_prompt_kernel, layer=layer, n_l=n_l, n_tiles=n_tiles, n_prev=n_prev,
                          sides=tuple((s.n_blocks, s.reorder_w_in) for s in sides)),
        grid=(n_tiles + 1,),
        in_specs=[
            pl.BlockSpec(memory_space=pltpu.SMEM),
            pl.BlockSpec((None, T, D_MODEL), lambda t: (t_proj(t) // n_l, t_proj(t) % n_l, 0)),
            pl.BlockSpec((None, T, D_MODEL), lambda t: (t_seq(t) // n_l, t_seq(t) % n_l, 0)),
            *_mixer_weight_specs(layer),
            tab_spec, tab_spec, tab_spec,
            *([pl.BlockSpec(memory_space=pl.ANY)] * n_prev),
            *side_in,
        ],
        out_specs=[
            pl.BlockSpec((None, T, D_MODEL), lambda t: (t_seq(t) // n_l, t_seq(t) % n_l, 0)),
            pl.BlockSpec((None, None, GLA_HEADS, GLA_DK, GLA_DV), lambda t: (layer, t_seq(t) // n_l, 0, 0, 0)),
            pl.BlockSpec((None, None, WINDOW, SWA_KV_W), lambda t: (layer, t_seq(t) // n_l, 0, 0)),
            pl.BlockSpec((None, None, WINDOW, SWA_KV_W), lambda t: (layer, t_seq(t) // n_l, 0, 0)),
            *side_out,
        ],
        out_shape=[
            jax.ShapeDtypeStruct((B, L, D_MODEL), F32),
            jax.ShapeDtypeStruct((depth, B, GLA_HEADS, GLA_DK, GLA_DV), F32),
            jax.ShapeDtypeStruct((depth, B, WINDOW, SWA_KV_W), F32),
            jax.ShapeDtypeStruct((depth, B, WINDOW, SWA_KV_W), F32),
            *side_shapes,
        ],
        input_output_aliases={N_MIX_IN + k: 1 + k for k in range(n_prev)},
        scratch_shapes=[
            pltpu.VMEM((2, T, GLA_QK_W), BF16),
            pltpu.VMEM((2, T, GLA_QK_W), F32),
            pltpu.VMEM((2, T, GLA_QK_W), F32),
            pltpu.VMEM((2, GLA_QK_W, T), BF16),
            pltpu.VMEM((2, GLA_QK_W, LANES), F32),
            pltpu.VMEM((2, T, GLA_V_W), BF16),
            pltpu.VMEM((2, T, GLA_V_W), F32),
            pltpu.VMEM((2, SWA_HEADS, T, LANES), BF16),
            pltpu.VMEM((2, T, SWA_KV_W), BF16),
            pltpu.VMEM((2, SWA_KV_W, T), BF16),
            pltpu.VMEM((2, WINDOW, SWA_KV_W), F32),
            pltpu.VMEM((2, WINDOW, SWA_KV_W), F32),
            pltpu.VMEM((GLA_QK_W, GLA_DV), F32),
            pltpu.VMEM((T, GLA_V_W), F32),
            pltpu.VMEM((T, D_MODEL), BF16),
            pltpu.VMEM((WINDOW, SWA_KV_W), BF16),
            pltpu.VMEM((SWA_KV_W, WINDOW), BF16),
            pltpu.VMEM((T // GLA_CHUNK, GLA_HEADS, GLA_CHUNK, LANES), BF16),
            pltpu.VMEM((T // GLA_CHUNK, GLA_QK_W, GLA_DV), F32),
            pltpu.VMEM((T // GLA_CHUNK, GLA_QK_W, GLA_DV), BF16),
            pltpu.VMEM((T // WINDOW, SWA_KV_HEADS, WINDOW, SWA_GROUP * WINDOW), F32),
            pltpu.VMEM((T // WINDOW, SWA_KV_HEADS, 2 * WINDOW, SWA_GROUP * WINDOW), BF16),
            pltpu.VMEM((T // WINDOW, SWA_KV_HEADS, 1, SWA_GROUP * WINDOW), F32),
        ],
        compiler_params=pltpu.CompilerParams(
            dimension_semantics=("arbitrary",), vmem_limit_bytes=VMEM_LIMIT),
        name="mix_prompt",
    )(sinks, x, x, *mix_w, *tabs, *(prev or ()), *(s.src for s in sides))


N_SMP_IN = 15


def _mix_sample_kernel(*refs, layer):
    (sink_ref, x_ref, s_ref, kc_ref, vc_ref, nmix_ref, win_ref, wa2_ref, ba_ref,
     glag_ref, swag_ref, wout_ref, cos_ref, sa_ref, sb_ref) = refs[:N_SMP_IN]
    xo_ref, so_ref, ko_ref, vo_ref = refs[-4:]
    R = x_ref.shape[0]
    NS = SMP_SEQS
    Q = R // NS
    q_shift = Q.bit_length() - 1

    x = x_ref[...]
    h = _rmsnorm(x, nmix_ref[...]).astype(BF16)
    proj = _dot(h, win_ref[...])
    z = _dot(proj[:, C_A:C_A + LANES].astype(BF16), wa2_ref[...]) + ba_ref[...]
    g = _log_sigmoid(z) * (1.0 / GATE_NORM)

    ri = lax.broadcasted_iota(jnp.int32, (R, R), 0)
    ci = lax.broadcasted_iota(jnp.int32, (R, R), 1)
    same = (ri >> q_shift) == (ci >> q_shift)
    causal = jnp.logical_and(same, ci <= ri)
    after = jnp.logical_and(same, ci > ri)
    b = _split_dot(jnp.where(causal, 1.0, 0.0).astype(BF16), g)
    rest = _split_dot(jnp.where(after, 1.0, 0.0).astype(BF16), g)

    qg = proj[:, C_QG:C_QG + GLA_QK_W]
    kg = proj[:, C_KG:C_KG + GLA_QK_W]
    qb = qg * QK_SCALE * jnp.exp(b)
    kd = (kg * jnp.exp(-b)).astype(BF16)
    kl_t = (kg * jnp.exp(rest)).T
    g_t = g.T

    own_tok = ((lax.broadcasted_iota(jnp.int32, (NS * GLA_DK, R), 0) >> 6)
               == (lax.broadcasted_iota(jnp.int32, (NS * GLA_DK, R), 1) >> q_shift))
    own_seq = ((lax.broadcasted_iota(jnp.int32, (R, NS * GLA_DK), 1) >> 6)
               == (lax.broadcasted_iota(jnp.int32, (R, NS * GLA_DK), 0) >> q_shift))
    lane_half = lax.broadcasted_iota(jnp.int32, (R, LANES), 1) >> 6
    ones = jnp.ones((R, GLA_DV), BF16)

    glag = glag_ref[...]
    og = []
    for hd in range(GLA_HEADS):
        ks_ = slice(hd * GLA_DK, (hd + 1) * GLA_DK)
        v = proj[:, C_VG + hd * GLA_DV:C_VG + (hd + 1) * GLA_DV].astype(BF16)
        att = jnp.where(causal, _dot_nt(qb[:, ks_].astype(BF16), kd[:, ks_]), 0.0).astype(BF16)
        o = _dot(att, v)
        s_old = s_ref[:, hd].reshape(NS * GLA_DK, GLA_DV)
        slab = qb[:, (hd // 2) * LANES:(hd // 2 + 1) * LANES]
        dup = jnp.where(lane_half == hd % 2, slab, pltpu.roll(slab, HEAD_DIM, 1))
        q_exp = jnp.where(own_seq, jnp.concatenate([dup] * (NS // 2), axis=1), 0.0).astype(BF16)
        o = o + _dot(q_exp, s_old.astype(BF16))
        o = o * lax.rsqrt(jnp.mean(o * o, axis=-1, keepdims=True) + EPS)
        rg = proj[:, C_RG + hd * GLA_DV:C_RG + (hd + 1) * GLA_DV]
        og.append(o * glag[:, hd * GLA_DV:(hd + 1) * GLA_DV] * _silu(rg))
        kl_exp = jnp.where(own_tok, jnp.concatenate([kl_t[ks_, :]] * NS, axis=0), 0.0).astype(BF16)
        g_exp = jnp.where(own_tok, jnp.concatenate([g_t[ks_, :]] * NS, axis=0), 0.0)
        b_last = _split_dot_rhs(g_exp, ones)
        s_new = jnp.exp(b_last) * s_old + _dot(kl_exp, v)
        so_ref[:, hd] = s_new.reshape(NS, GLA_DK, GLA_DV)

    cos, sa, sb = cos_ref[...], sa_ref[...], sb_ref[...]
    q_rot = _rope(proj[:, C_QS:C_QS + SWA_Q_W], cos, sa, sb) * QK_SCALE
    k_rot = _rope(proj[:, C_KS:C_KS + SWA_KV_W], cos, sa, sb)
    v_new = proj[:, C_VS:C_VS + SWA_KV_W]

    W = kc_ref.shape[1]
    SR = SMP_SUB * Q
    GR = SWA_GROUP * SR
    sub_shift = SR.bit_length() - 1
    w_shift = W.bit_length() - 1
    r_c = lax.broadcasted_iota(jnp.int32, (GR, SMP_SUB * W), 0)
    c_c = lax.broadcasted_iota(jnp.int32, (GR, SMP_SUB * W), 1)
    vis_c = jnp.logical_and(((r_c & (SR - 1)) >> q_shift) == (c_c >> w_shift),
                            (c_c & (W - 1)) > (r_c & (Q - 1)))
    r_n = lax.broadcasted_iota(jnp.int32, (GR, SR), 0)
    c_n = lax.broadcasted_iota(jnp.int32, (GR, SR), 1)
    vis_n = jnp.logical_and(((r_n & (SR - 1)) >> q_shift) == (c_n >> q_shift),
                            (c_n & (Q - 1)) <= (r_n & (Q - 1)))
    grp_row = lax.broadcasted_iota(jnp.int32, (GR, 1), 0) >> sub_shift
    half_sr = lax.broadcasted_iota(jnp.int32, (SR, LANES), 1) >> 6
    half_gr = lax.broadcasted_iota(jnp.int32, (GR, LANES), 1) >> 6

    o_rows = []
    for sg in range(NS // SMP_SUB):
        rows = slice(sg * SR, (sg + 1) * SR)
        k_c = kc_ref[sg * SMP_SUB:(sg + 1) * SMP_SUB].reshape(SMP_SUB * W, SWA_KV_W).astype(BF16)
        v_c = vc_ref[sg * SMP_SUB:(sg + 1) * SMP_SUB].reshape(SMP_SUB * W, SWA_KV_W).astype(BF16)
        k_n = k_rot[rows, :].astype(BF16)
        v_n = v_new[rows, :].astype(BF16)
        o_kv = []
        for j in range(SWA_KV_HEADS):
            parts = []
            for gq in range(SWA_GROUP):
                hq = j * SWA_GROUP + gq
                slab = q_rot[rows, (hq // 2) * LANES:(hq // 2 + 1) * LANES]
                if hq % 2 != j:
                    slab = pltpu.roll(slab, HEAD_DIM, 1)
                parts.append(jnp.where(half_sr == j, slab, 0.0))
            q8 = jnp.concatenate(parts, axis=0).astype(BF16)
            s_c = jnp.where(vis_c, _dot_nt(q8, k_c), -jnp.inf)
            s_n = jnp.where(vis_n, _dot_nt(q8, k_n), -jnp.inf)
            sink = jnp.zeros((GR, 1), F32)
            for gq in range(SWA_GROUP):
                sink = jnp.where(grp_row == gq, sink_ref[layer, j * SWA_GROUP + gq], sink)
            m = jnp.maximum(jnp.maximum(jnp.max(s_c, axis=-1, keepdims=True),
                                        jnp.max(s_n, axis=-1, keepdims=True)), sink)
            p_c = jnp.exp(s_c - m)
            p_n = jnp.exp(s_n - m)
            den = (jnp.sum(p_c, axis=-1, keepdims=True) + jnp.sum(p_n, axis=-1, keepdims=True)
                   + jnp.exp(sink - m))
            o = _dot((p_c / den).astype(BF16), v_c) + _dot((p_n / den).astype(BF16), v_n)
            o_kv.append(o)
        slabs = []
        for p in range(SWA_HEADS // 2):
            halves = []
            for hq in (2 * p, 2 * p + 1):
                j, gq = hq // SWA_GROUP, hq % SWA_GROUP
                piece = o_kv[j][gq * SR:(gq + 1) * SR, :]
                if hq % 2 != j:
                    piece = pltpu.roll(piece, HEAD_DIM, 1)
                halves.append(piece)
            slabs.append(jnp.where(half_sr == 0, halves[0], halves[1]))
        o_rows.append(jnp.concatenate(slabs, axis=1))
    o_s = _rmsnorm(jnp.concatenate(o_rows, axis=0), swag_ref[...])

    cat = jnp.concatenate(og + [o_s], axis=1).astype(BF16)
    xo_ref[...] = x + _dot(cat, wout_ref[...])

    ko_ref[:, 0:W - Q, :] = kc_ref[:, Q:W, :]
    vo_ref[:, 0:W - Q, :] = vc_ref[:, Q:W, :]
    for s in range(NS):
        ko_ref[s, W - Q:W, :] = k_rot[s * Q:(s + 1) * Q, :]
        vo_ref[s, W - Q:W, :] = v_new[s * Q:(s + 1) * Q, :]


def _split_dot_rhs(g, m_bf16):
    hi = g.astype(BF16)
    lo = (g - hi.astype(F32)).astype(BF16)
    return _dot(hi, m_bf16) + _dot(lo, m_bf16)


def _mix_sample(x2d, state, kc, vc, sinks, mix_w, tabs, layer, prev):
    depth, NB, W = kc.shape[0], kc.shape[1], kc.shape[2]
    Q = x2d.shape[0] // NB
    R = SMP_SEQS * Q
    tab_spec = pl.BlockSpec((R, LANES), lambda i: (0, 0))
    st_spec = pl.BlockSpec((None, SMP_SEQS, GLA_HEADS, GLA_DK, GLA_DV), lambda i: (layer, i, 0, 0, 0))
    kv_spec = pl.BlockSpec((None, SMP_SEQS, W, SWA_KV_W), lambda i: (layer, i, 0, 0))
    n_prev = 0 if prev is None else len(prev)
    return pl.pallas_call(
        functools.partial(_mix_sample_kernel, layer=layer),
        grid=(NB // SMP_SEQS,),
        in_specs=[
            pl.BlockSpec(memory_space=pltpu.SMEM),
            pl.BlockSpec((R, D_MODEL), lambda i: (i, 0)),
            st_spec, kv_spec, kv_spec,
            *_mixer_weight_specs(layer),
            tab_spec, tab_spec, tab_spec,
            *([pl.BlockSpec(memory_space=pl.ANY)] * n_prev),
        ],
        out_specs=[pl.BlockSpec((R, D_MODEL), lambda i: (i, 0)), st_spec, kv_spec, kv_spec],
        out_shape=[
            jax.ShapeDtypeStruct(x2d.shape, F32),
            jax.ShapeDtypeStruct(state.shape, F32),
            jax.ShapeDtypeStruct(kc.shape, F32),
            jax.ShapeDtypeStruct(vc.shape, F32),
        ],
        input_output_aliases={N_SMP_IN + k: 1 + k for k in range(n_prev)},
        compiler_params=pltpu.CompilerParams(
            dimension_semantics=("arbitrary",), vmem_limit_bytes=VMEM_LIMIT),
        name="mix_sample",
    )(sinks, x2d, state, kc, vc, *mix_w, *tabs, *(prev or ()))


def _rope_tables(pos):
    half = ROT_DIM // 2
    inv = jnp.power(jnp.float32(ROPE_THETA), -jnp.arange(half, dtype=jnp.float32) * (2.0 / ROT_DIM))
    ang = pos.astype(jnp.float32)[:, None] * inv[None, :]
    cos, sin = jnp.cos(ang), jnp.sin(ang)
    n = pos.shape[0]
    pad = jnp.zeros((n, HEAD_DIM - ROT_DIM), F32)
    zero = jnp.zeros((n, half), F32)
    c64 = jnp.concatenate([cos, cos, pad + 1.0], axis=1)
    a64 = jnp.concatenate([zero, sin, pad], axis=1)
    b64 = jnp.concatenate([-sin, zero, pad], axis=1)
    rep = LANES // HEAD_DIM
    return tuple(jnp.tile(t, (1, rep)) for t in (c64, a64, b64))


def kernel(x_prompt, x_sample, state_gla, cache_swa_k, cache_swa_v, norm_ffn1, w_ffn1_gu, w_ffn1_down,
           norm_mix, w_in, w_gate_up, b_gate, gla_norm, attn_sinks, swa_norm, w_out, norm_ffn2,
           w_ffn2_gu, w_ffn2_down, norm_final):
    B, L, _ = x_prompt.shape
    NB, Q, _ = x_sample.shape
    depth = w_in.shape[0]
    W = cache_swa_k.shape[2]

    tabs_p = _rope_tables(jnp.arange(L))
    tabs_s = _rope_tables(jnp.tile(PAST_LEN + jnp.arange(Q), SMP_SEQS))

    wa2 = jnp.pad(w_gate_up, ((0, 0), (0, LANES - GATE_RANK), (0, 0))).astype(BF16)
    row = lambda p: p.reshape(depth, 1, -1)
    n1, n2 = row(norm_ffn1), row(norm_ffn2)
    gu_rows = D_MODEL // (B * L // FFN_TILE)
    dn_rows = D_FF // 11

    def ffn_sides(w_gu, w_down, layer):
        return (SideCast(w_gu, layer, gu_rows), SideCast(w_down, layer, dn_rows))

    xp = x_prompt.reshape(B * L, D_MODEL)
    xs = x_sample.reshape(NB * Q, D_MODEL)
    kc = cache_swa_k.reshape(depth, NB, W, SWA_KV_W)
    vc = cache_swa_v.reshape(depth, NB, W, SWA_KV_W)
    out_p = out_s = None
    gu1, dn1 = _cast_layers((SideCast(w_ffn1_gu, 0, D_MODEL // 8), SideCast(w_ffn1_down, 0, D_FF // 4)))
    for l in range(depth):
        nf = norm_final if l == depth - 1 else None
        xs, xp, win, wout = _ffn(xs, xp, n1, gu1, dn1, l, sides=(SideCast(w_in, l, gu_rows, True),
                                                                SideCast(w_out, l, gu_rows)))
        mix_w = (row(norm_mix), win, wa2, row(b_gate), row(gla_norm), row(swa_norm), wout)
        xp3, *out_p = _mix_prompt(xp.reshape(B, L, D_MODEL), attn_sinks, mix_w, tabs_p, l, depth, out_p,
                                  sides=ffn_sides(w_ffn2_gu, w_ffn2_down, l))
        gu2, dn2 = out_p[3:]
        out_p = out_p[:3]
        xs, *out_s = _mix_sample(xs, state_gla, kc, vc, attn_sinks, mix_w, tabs_s, l, out_s)
        next_sides = ffn_sides(w_ffn1_gu, w_ffn1_down, l + 1) if l + 1 < depth else ()
        xs, xp, *next_w = _ffn(xs, xp3.reshape(B * L, D_MODEL), n2, gu2, dn2, l, nf, sides=next_sides)
        if next_w:
            gu1, dn1 = next_w

    gla_p, k_p, v_p = out_p
    gla_s, k_s, v_s = out_s
    kv5 = lambda t: t.reshape(*t.shape[:3], SWA_KV_HEADS, HEAD_DIM)
    return (xp.reshape(B, L, D_MODEL), xs.reshape(NB, Q, D_MODEL), gla_p, kv5(k_p), kv5(v_p),
            gla_s, kv5(k_s), kv5(v_s))
```

```python
import functools
import itertools
from typing import NamedTuple

import jax
import jax.numpy as jnp
import numpy as np
from jax import lax
from jax.experimental import pallas as pl
from jax.experimental.pallas import tpu as pltpu

F32 = jnp.float32
BF16 = jnp.bfloat16

D_MODEL = 1024
D_FF = 2816
EPS = 1e-6
GLA_HEADS = 4
GLA_DK = 64
GLA_DV = 128
GLA_QK_W = GLA_HEADS * GLA_DK
GLA_V_W = GLA_HEADS * GLA_DV
GATE_RANK = 16
GATE_NORM = 16.0
GLA_CHUNK = 64
HEAD_DIM = 64
SWA_HEADS = 8
SWA_KV_HEADS = 2
SWA_GROUP = SWA_HEADS // SWA_KV_HEADS
SWA_Q_W = SWA_HEADS * HEAD_DIM
SWA_KV_W = SWA_KV_HEADS * HEAD_DIM
WINDOW = 128
ROT_DIM = 16
ROPE_THETA = 500000.0
PAST_LEN = 16384
QK_SCALE = 0.125

LANES = 128
C_QG, C_KG, C_VG, C_RG = 0, 256, 512, 1024
C_QS, C_KS, C_VS, C_A = 1536, 2048, 2176, 2304
IN_W_PAD = C_A + LANES

FFN_TILE = 512
FFN_CHUNK = 256
MIX_TILE = 512
SMP_SEQS = 32
SMP_SUB = 8
VMEM_LIMIT = 56 * 1024 * 1024

NT_DIMS = (((1,), (1,)), ((), ()))
TN_DIMS = (((0,), (0,)), ((), ()))


def _dot(a, b):
    return jnp.dot(a, b, preferred_element_type=F32)


def _dot_nt(a, b):
    return lax.dot_general(a, b, NT_DIMS, preferred_element_type=F32)


def _dot_tn(a, b):
    return lax.dot_general(a, b, TN_DIMS, preferred_element_type=F32)


def _rmsnorm(x, g):
    return x * lax.rsqrt(jnp.mean(x * x, axis=-1, keepdims=True) + EPS) * g


def _log_sigmoid(z):
    return jnp.minimum(z, 0.0) - jnp.log1p(jnp.exp(-jnp.abs(z)))


def _silu(z):
    return z * jax.nn.sigmoid(z)


def _split_dot(m_bf16, g):
    hi = g.astype(BF16)
    lo = (g - hi.astype(F32)).astype(BF16)
    return _dot(m_bf16, hi) + _dot(m_bf16, lo)


def _rope(x, cos, sa, sb):
    half = ROT_DIM // 2
    outs = []
    for p in range(x.shape[1] // LANES):
        s = x[:, p * LANES:(p + 1) * LANES]
        outs.append(s * cos + pltpu.roll(s, half, 1) * sa + pltpu.roll(s, LANES - half, 1) * sb)
    return outs[0] if len(outs) == 1 else jnp.concatenate(outs, axis=1)


class SideCast(NamedTuple):
    src: jax.Array
    layer: int
    block_rows: int
    reorder_w_in: bool = False

    @property
    def n_blocks(self):
        return self.src.shape[1] // self.block_rows

    @property
    def out_cols(self):
        return IN_W_PAD if self.reorder_w_in else self.src.shape[2]


def _cast_block(src_ref, dst_ref, reorder_w_in):
    w = src_ref[...]
    if not reorder_w_in:
        dst_ref[...] = w.astype(BF16)
        return
    n_a = C_RG + GLA_V_W
    dst_ref[:, 0:n_a] = w[:, 0:n_a].astype(BF16)
    dst_ref[:, n_a:C_A] = w[:, n_a + GATE_RANK:].astype(BF16)
    pad = jnp.zeros((w.shape[0], LANES - GATE_RANK), F32)
    dst_ref[:, C_A:IN_W_PAD] = jnp.concatenate([w[:, n_a:n_a + GATE_RANK], pad], axis=1).astype(BF16)


def _side_cast_specs(sides):
    in_specs, out_specs, out_shapes = [], [], []
    for s in sides:
        last = s.n_blocks - 1
        in_specs.append(pl.BlockSpec((None, s.block_rows, s.src.shape[2]),
                                     lambda i, layer=s.layer, last=last: (layer, jnp.minimum(i, last), 0)))
        out_specs.append(pl.BlockSpec((s.block_rows, s.out_cols), lambda i, last=last: (jnp.minimum(i, last), 0)))
        out_shapes.append(jax.ShapeDtypeStruct((s.src.shape[1], s.out_cols), BF16))
    return in_specs, out_specs, out_shapes


def _run_side_casts(step, sides, src_refs, dst_refs):
    for (n_blocks, reorder), src, dst in zip(sides, src_refs, dst_refs):
        @pl.when(step < n_blocks)
        def _():
            _cast_block(src, dst, reorder)


def _cast_kernel(*refs, sides):
    n = len(sides)
    _run_side_casts(pl.program_id(0), sides, refs[:n], refs[n:])


def _cast_layers(sides):
    in_specs, out_specs, out_shapes = _side_cast_specs(sides)
    return pl.pallas_call(
        functools.partial(_cast_kernel, sides=tuple((s.n_blocks, s.reorder_w_in) for s in sides)),
        grid=(max(s.n_blocks for s in sides),),
        in_specs=in_specs,
        out_specs=out_specs,
        out_shape=out_shapes,
        compiler_params=pltpu.CompilerParams(
            dimension_semantics=("arbitrary",), vmem_limit_bytes=VMEM_LIMIT),
        name="cast_weights",
    )(*(s.src for s in sides))


def _ffn_kernel(*refs, final, sides):
    n_in = 5 if final else 4
    n_side = len(sides)
    x_ref, n_ref, wgu_ref, wd_ref = refs[:4]
    nf_ref = refs[4] if final else None
    o_ref = refs[n_in + n_side]
    acc_ref = refs[-1]
    _run_side_casts(pl.program_id(0), sides, refs[n_in:n_in + n_side],
                    refs[n_in + n_side + 1:n_in + 2 * n_side + 1])
    x = x_ref[...]
    h = _rmsnorm(x, n_ref[...]).astype(BF16)
    for c in range(D_FF // FFN_CHUNK):
        lo = c * FFN_CHUNK
        gate = _dot(h, wgu_ref[:, lo:lo + FFN_CHUNK])
        up = _dot(h, wgu_ref[:, D_FF + lo:D_FF + lo + FFN_CHUNK])
        act = (_silu(gate) * up).astype(BF16)
        part = _dot(act, wd_ref[lo:lo + FFN_CHUNK, :])
        if c == 0:
            acc_ref[...] = part
        else:
            acc_ref[...] += part
    y = x + 0.5 * acc_ref[...]
    if final:
        y = _rmsnorm(y, nf_ref[...])
    o_ref[...] = y


def _ffn(x2d, norm, w_gu, w_down, layer, norm_final=None, sides=()):
    final = norm_final is not None
    const = lambda i: (0, 0)
    tile = pl.BlockSpec((FFN_TILE, D_MODEL), lambda i: (i, 0))
    in_specs = [
        tile,
        pl.BlockSpec((None, 1, D_MODEL), lambda i: (layer, 0, 0)),
        pl.BlockSpec((D_MODEL, 2 * D_FF), const, pipeline_mode=pl.Buffered(1)),
        pl.BlockSpec((D_FF, D_MODEL), const, pipeline_mode=pl.Buffered(1)),
    ]
    args = [x2d, norm, w_gu, w_down]
    if final:
        in_specs.append(pl.BlockSpec((1, D_MODEL), const))
        args.append(norm_final.reshape(1, D_MODEL))
    side_in, side_out, side_shapes = _side_cast_specs(sides)
    outs = pl.pallas_call(
        functools.partial(_ffn_kernel, final=final,
                          sides=tuple((s.n_blocks, s.reorder_w_in) for s in sides)),
        grid=(x2d.shape[0] // FFN_TILE,),
        in_specs=in_specs + side_in,
        out_specs=[tile] + side_out,
        out_shape=[jax.ShapeDtypeStruct(x2d.shape, F32)] + side_shapes,
        scratch_shapes=[pltpu.VMEM((FFN_TILE, D_MODEL), F32)],
        compiler_params=pltpu.CompilerParams(
            dimension_semantics=("arbitrary",), vmem_limit_bytes=VMEM_LIMIT),
        name="ffn_final" if final else "ffn",
    )(*args, *(s.src for s in sides))
    return outs if sides else outs[0]


N_MIX_IN = 13


def _mix_project(slot, x_ref, nmix_ref, win_ref, wa2_ref, ba_ref, glag_ref, cos_ref, sa_ref, sb_ref,
                 qb_s, qsw_s, kd_s, klt_s, dect_s, v_s, gate_s, qh_s, kt_s, vt_s, kout_s, vout_s):
    T = MIX_TILE
    G = 256
    qb_s, qsw_s, kd_s, klt_s, dect_s, v_s, gate_s, qh_s, kt_s, vt_s, kout_s, vout_s = (
        r.at[slot] for r in (qb_s, qsw_s, kd_s, klt_s, dect_s, v_s, gate_s, qh_s, kt_s, vt_s, kout_s, vout_s))

    x = x_ref[...]
    h = _rmsnorm(x, nmix_ref[...]).astype(BF16)
    yield

    a_low = _dot(h, win_ref[:, C_A:C_A + LANES]).astype(BF16)
    z = _dot(a_low, wa2_ref[...]) + ba_ref[...]
    g = _log_sigmoid(z) * (1.0 / GATE_NORM)
    yield
    qk = _dot(h, win_ref[:, C_QG:C_QG + 2 * GLA_QK_W])
    yield

    ri = lax.broadcasted_iota(jnp.int32, (G, G), 0)
    ci = lax.broadcasted_iota(jnp.int32, (G, G), 1)
    same_chunk = (ri >> 6) == (ci >> 6)
    incl = jnp.where(jnp.logical_and(same_chunk, ci <= ri), 1.0, 0.0).astype(BF16)
    C = GLA_CHUNK
    chunk_last = []
    for r in range(T // G):
        rows = slice(r * G, (r + 1) * G)
        b = _split_dot(incl, g[rows])
        lasts = [b[c * C + C - 1:(c + 1) * C, :] for c in range(G // C)]
        chunk_last.extend(lasts)
        rest = jnp.concatenate([jnp.broadcast_to(t, (C, GLA_QK_W)) for t in lasts], axis=0) - b
        q = qk[rows, 0:GLA_QK_W]
        k = qk[rows, GLA_QK_W:2 * GLA_QK_W]
        qb = q * QK_SCALE * jnp.exp(b)
        qb_s[rows, :] = qb.astype(BF16)
        qsw_s[rows, :] = jnp.concatenate(
            [pltpu.roll(qb[:, p * LANES:(p + 1) * LANES], GLA_DK, 1) for p in range(GLA_QK_W // LANES)],
            axis=1)
        kd_s[rows, :] = k * jnp.exp(-b)
        klt_s[:, rows] = (k * jnp.exp(rest)).T.astype(BF16)
        yield
    pad = jnp.zeros((LANES - len(chunk_last), GLA_QK_W), F32)
    dect_s[...] = jnp.exp(jnp.concatenate(chunk_last + [pad], axis=0)).T

    v_s[...] = _dot(h, win_ref[:, C_VG:C_VG + GLA_V_W]).astype(BF16)
    yield
    gate_s[...] = _silu(_dot(h, win_ref[:, C_RG:C_RG + GLA_V_W])) * glag_ref[...]
    yield

    cos, sa, sb = cos_ref[...], sa_ref[...], sb_ref[...]
    half_t = lax.broadcasted_iota(jnp.int32, (T, LANES), 1) >> 6
    q_rot = _rope(_dot(h, win_ref[:, C_QS:C_QS + SWA_Q_W]), cos, sa, sb) * QK_SCALE
    for hq in range(SWA_HEADS):
        slab = q_rot[:, (hq // 2) * LANES:(hq // 2 + 1) * LANES]
        j = hq // SWA_GROUP
        if hq % 2 != j:
            slab = pltpu.roll(slab, HEAD_DIM, 1)
        qh_s[hq] = jnp.where(half_t == j, slab, 0.0).astype(BF16)
        if hq % 4 == 3:
            yield
    kv = _dot(h, win_ref[:, C_KS:C_KS + 2 * SWA_KV_W])
    k_rot = _rope(kv[:, 0:SWA_KV_W], cos, sa, sb)
    v_swa = kv[:, SWA_KV_W:2 * SWA_KV_W]
    kt_s[...] = k_rot.astype(BF16)
    vt_s[...] = v_swa.T.astype(BF16)
    kout_s[...] = k_rot[T - WINDOW:T, :]
    vout_s[...] = v_swa[T - WINDOW:T, :]


def _mix_sequential(slot, first, layer, sink_ref, x_ref, swag_ref, wout_ref, xo_ref,
                    qb_s, qsw_s, kd_s, klt_s, dect_s, v_s, gate_s, qh_s, kt_s, vt_s,
                    s_s, og_s, cat_s, kcar_s, vcar_s, lhs_s, upd_s, st16_s, sc_s, pr_s, rden_s):
    T = MIX_TILE
    C = GLA_CHUNK
    qb_s, qsw_s, kd_s, klt_s, dect_s, v_s, gate_s, qh_s, kt_s, vt_s = (
        r.at[slot] for r in (qb_s, qsw_s, kd_s, klt_s, dect_s, v_s, gate_s, qh_s, kt_s, vt_s))

    hk_r = lax.broadcasted_iota(jnp.int32, (GLA_QK_W, GLA_QK_W), 0) >> 6
    hk_c = lax.broadcasted_iota(jnp.int32, (GLA_QK_W, GLA_QK_W), 1) >> 6
    bd_k = hk_r == hk_c
    qi = lax.broadcasted_iota(jnp.int32, (C, GLA_QK_W), 0)
    kj = lax.broadcasted_iota(jnp.int32, (C, GLA_QK_W), 1) & (C - 1)
    causal = kj <= qi
    half_c = lax.broadcasted_iota(jnp.int32, (C, LANES), 1) >> 6
    zeros_v = jnp.zeros((C, GLA_DV), BF16)
    n_c = T // C
    for c in range(n_c):
        rows = slice(c * C, (c + 1) * C)
        kd_bd = jnp.where(bd_k, jnp.concatenate([kd_s[rows, :]] * GLA_HEADS, axis=0), 0.0).astype(BF16)
        att = jnp.where(causal, _dot_nt(qb_s[rows, :], kd_bd), 0.0)
        tok = slice((c // 2) * 2 * C, (c // 2 + 1) * 2 * C)
        for hd in range(GLA_HEADS):
            slab = slice((hd // 2) * LANES, (hd // 2 + 1) * LANES)
            hk = slice(hd * GLA_DK, (hd + 1) * GLA_DK)
            v = v_s[rows, hd * GLA_DV:(hd + 1) * GLA_DV]
            lhs_s[c, hd] = jnp.where(half_c == hd % 2, att[:, slab], qsw_s[rows, slab]).astype(BF16)
            v_pair = jnp.concatenate([v, zeros_v] if c % 2 == 0 else [zeros_v, v], axis=0)
            upd_s[c, hk, :] = _dot(klt_s[hk, tok], v_pair)
        if c % 2 == 1:
            yield
    state = s_s[...]
    for c in range(n_c):
        st16_s[c] = state.astype(BF16)
        state = dect_s[:, c:c + 1] * state + upd_s[c]
    s_s[...] = state
    yield
    for c in range(n_c):
        rows = slice(c * C, (c + 1) * C)
        for hd in range(GLA_HEADS):
            hk = slice(hd * GLA_DK, (hd + 1) * GLA_DK)
            v = v_s[rows, hd * GLA_DV:(hd + 1) * GLA_DV]
            st = st16_s[c, hk, :]
            rhs = jnp.concatenate([v, st] if hd % 2 == 0 else [st, v], axis=0)
            og_s[rows, hd * GLA_DV:(hd + 1) * GLA_DV] = _dot(lhs_s[c, hd], rhs)
        if c % 2 == 1:
            yield

    for hd in range(GLA_HEADS):
        vs_ = slice(hd * GLA_DV, (hd + 1) * GLA_DV)
        o = og_s[:, vs_]
        o = o * lax.rsqrt(jnp.mean(o * o, axis=-1, keepdims=True) + EPS)
        cat_s[:, vs_] = (o * gate_s[:, vs_]).astype(BF16)
    yield

    swag = swag_ref[...]
    GR = SWA_GROUP * WINDOW
    kc_ = lax.broadcasted_iota(jnp.int32, (WINDOW, GR), 0)
    qr = lax.broadcasted_iota(jnp.int32, (WINDOW, GR), 1) & (WINDOW - 1)
    from_prev = kc_ > qr
    prev_bias = jnp.where(first > 0.0, -jnp.inf, 0.0)
    n_b = T // WINDOW
    sinks = [jnp.concatenate(
        [jnp.full((1, WINDOW), sink_ref[layer, j * SWA_GROUP + gq], F32) for gq in range(SWA_GROUP)], axis=1)
        for j in range(SWA_KV_HEADS)]
    for i in range(n_b):
        rows = slice(i * WINDOW, (i + 1) * WINDOW)
        if i == 0:
            k_blk = jnp.concatenate([kcar_s[...], kt_s[0:WINDOW, :]], axis=0)
        else:
            k_blk = kt_s[(i - 1) * WINDOW:(i + 1) * WINDOW, :]
        for j in range(SWA_KV_HEADS):
            q4 = jnp.concatenate([qh_s[j * SWA_GROUP + gq, rows, :] for gq in range(SWA_GROUP)], axis=0)
            s2 = _dot_nt(k_blk, q4)
            s_prev = s2[0:WINDOW, :] + prev_bias if i == 0 else s2[0:WINDOW, :]
            sc_s[i, j] = jnp.where(from_prev, s_prev, s2[WINDOW:2 * WINDOW, :])
        yield
    for i in range(n_b):
        for j in range(SWA_KV_HEADS):
            s = sc_s[i, j]
            m = jnp.maximum(jnp.max(s, axis=0, keepdims=True), sinks[j])
            pr = jnp.exp(s - m)
            rden_s[i, j] = 1.0 / (jnp.sum(pr, axis=0, keepdims=True) + jnp.exp(sinks[j] - m))
            pr_s[i, j, 0:WINDOW, :] = jnp.where(from_prev, pr, 0.0).astype(BF16)
            pr_s[i, j, WINDOW:2 * WINDOW, :] = jnp.where(from_prev, 0.0, pr).astype(BF16)
        yield
    for i in range(n_b):
        rows = slice(i * WINDOW, (i + 1) * WINDOW)
        if i == 0:
            vt_blk = jnp.concatenate([vcar_s[...], vt_s[:, 0:WINDOW]], axis=1)
        else:
            vt_blk = vt_s[:, (i - 1) * WINDOW:(i + 1) * WINDOW]
        pieces = []
        for j in range(SWA_KV_HEADS):
            vt_j = vt_blk[j * HEAD_DIM:(j + 1) * HEAD_DIM, :]
            o_t = _dot(vt_j, pr_s[i, j]) * rden_s[i, j]
            pieces.extend(o_t[:, gq * WINDOW:(gq + 1) * WINDOW] for gq in range(SWA_GROUP))
        o_s = jnp.concatenate(pieces, axis=0).T
        cat_s[rows, GLA_V_W:GLA_V_W + SWA_Q_W] = _rmsnorm(o_s, swag).astype(BF16)
        yield

    xo_ref[...] = (x_ref[...] + _dot(cat_s[:, 0:GLA_V_W], wout_ref[0:GLA_V_W, :])
                   + _dot(cat_s[:, GLA_V_W:D_MODEL], wout_ref[GLA_V_W:D_MODEL, :]))

    kcar_s[...] = kt_s[T - WINDOW:T, :]
    vcar_s[...] = vt_s[:, T - WINDOW:T]


N_MIX_OUT = 4
N_MIX_SCRATCH = 23


def _mix_prompt_kernel(*refs, layer, n_l, n_tiles, n_prev, sides):
    (sink_ref, xn_ref, xv_ref, nmix_ref, win_ref, wa2_ref, ba_ref, glag_ref, swag_ref,
     wout_ref, cos_ref, sa_ref, sb_ref) = refs[:N_MIX_IN]
    n_side = len(sides)
    side_src = refs[N_MIX_IN + n_prev:N_MIX_IN + n_prev + n_side]
    outs = refs[N_MIX_IN + n_prev + n_side:len(refs) - N_MIX_SCRATCH]
    xo_ref, so_ref, ko_ref, vo_ref = outs[:N_MIX_OUT]
    (qb_s, qsw_s, kd_s, klt_s, dect_s, v_s, gate_s, qh_s, kt_s, vt_s, kout_s, vout_s,
     s_s, og_s, cat_s, kcar_s, vcar_s, lhs_s, upd_s, st16_s, sc_s, pr_s, rden_s) = refs[-N_MIX_SCRATCH:]
    handover = (qb_s, qsw_s, kd_s, klt_s, dect_s, v_s, gate_s, qh_s, kt_s, vt_s)
    t = pl.program_id(0)
    l_seq = jnp.maximum(t - 1, 0) % n_l
    _run_side_casts(t, sides, side_src, outs[N_MIX_OUT:])

    @pl.when(t == 0)
    def _():
        for r in handover:
            r[1] = jnp.zeros(r.shape[1:], r.dtype)
        kcar_s[...] = jnp.zeros_like(kcar_s)
        vcar_s[...] = jnp.zeros_like(vcar_s)

    @pl.when(l_seq == 0)
    def _():
        s_s[...] = jnp.zeros_like(s_s)

    first = jnp.where(l_seq == 0, 1.0, 0.0)

    def step(slot_proj, slot_seq):
        proj = _mix_project(slot_proj, xn_ref, nmix_ref, win_ref, wa2_ref, ba_ref, glag_ref, cos_ref,
                            sa_ref, sb_ref, *handover, kout_s, vout_s)
        seq = _mix_sequential(slot_seq, first, layer, sink_ref, xv_ref, swag_ref, wout_ref, xo_ref,
                              *handover, s_s, og_s, cat_s, kcar_s, vcar_s, lhs_s, upd_s, st16_s, sc_s, pr_s, rden_s)
        for _ in itertools.zip_longest(seq, proj):
            pass

        @pl.when(jnp.logical_and(t > 0, l_seq == n_l - 1))
        def _():
            for hd in range(GLA_HEADS):
                so_ref[hd] = s_s[hd * GLA_DK:(hd + 1) * GLA_DK, :]
            ko_ref[...] = kout_s[slot_seq]
            vo_ref[...] = vout_s[slot_seq]

    @pl.when(t % 2 == 0)
    def _():
        step(0, 1)

    @pl.when(t % 2 == 1)
    def _():
        step(1, 0)


def _mixer_weight_specs(layer):
    at_layer = lambda i: (layer, 0, 0)
    const = lambda i: (0, 0)
    return [
        pl.BlockSpec((None, 1, D_MODEL), at_layer),
        pl.BlockSpec((D_MODEL, IN_W_PAD), const),
        pl.BlockSpec((None, LANES, GLA_QK_W), at_layer),
        pl.BlockSpec((None, 1, GLA_QK_W), at_layer),
        pl.BlockSpec((None, 1, GLA_V_W), at_layer),
        pl.BlockSpec((None, 1, SWA_Q_W), at_layer),
        pl.BlockSpec((D_MODEL, D_MODEL), const),
    ]


def _mix_prompt(x, sinks, mix_w, tabs, layer, depth, prev, sides=()):
    B, L, _ = x.shape
    T = MIX_TILE
    n_l = L // T
    n_tiles = B * n_l
    t_proj = lambda t: jnp.minimum(t, n_tiles - 1)
    t_seq = lambda t: jnp.maximum(t - 1, 0)
    tab_spec = pl.BlockSpec((T, LANES), lambda t: (t_proj(t) % n_l, 0))
    n_prev = 0 if prev is None else len(prev)
    side_in, side_out, side_shapes = _side_cast_specs(sides)
    return pl.pallas_call(
        functools.partial(_mix_prompt_kernel, layer=layer, n_l=n_l, n_tiles=n_tiles, n_prev=n_prev,
                          sides=tuple((s.n_blocks, s.reorder_w_in) for s in sides)),
        grid=(n_tiles + 1,),
        in_specs=[
            pl.BlockSpec(memory_space=pltpu.SMEM),
            pl.BlockSpec((None, T, D_MODEL), lambda t: (t_proj(t) // n_l, t_proj(t) % n_l, 0)),
            pl.BlockSpec((None, T, D_MODEL), lambda t: (t_seq(t) // n_l, t_seq(t) % n_l, 0)),
            *_mixer_weight_specs(layer),
            tab_spec, tab_spec, tab_spec,
            *([pl.BlockSpec(memory_space=pl.ANY)] * n_prev),
            *side_in,
        ],
        out_specs=[
            pl.BlockSpec((None, T, D_MODEL), lambda t: (t_seq(t) // n_l, t_seq(t) % n_l, 0)),
            pl.BlockSpec((None, None, GLA_HEADS, GLA_DK, GLA_DV), lambda t: (layer, t_seq(t) // n_l, 0, 0, 0)),
            pl.BlockSpec((None, None, WINDOW, SWA_KV_W), lambda t: (layer, t_seq(t) // n_l, 0, 0)),
            pl.BlockSpec((None, None, WINDOW, SWA_KV_W), lambda t: (layer, t_seq(t) // n_l, 0, 0)),
            *side_out,
        ],
        out_shape=[
            jax.ShapeDtypeStruct((B, L, D_MODEL), F32),
            jax.ShapeDtypeStruct((depth, B, GLA_HEADS, GLA_DK, GLA_DV), F32),
            jax.ShapeDtypeStruct((depth, B, WINDOW, SWA_KV_W), F32),
            jax.ShapeDtypeStruct((depth, B, WINDOW, SWA_KV_W), F32),
            *side_shapes,
        ],
        input_output_aliases={N_MIX_IN + k: 1 + k for k in range(n_prev)},
        scratch_shapes=[
            pltpu.VMEM((2, T, GLA_QK_W), BF16),
            pltpu.VMEM((2, T, GLA_QK_W), F32),
            pltpu.VMEM((2, T, GLA_QK_W), F32),
            pltpu.VMEM((2, GLA_QK_W, T), BF16),
            pltpu.VMEM((2, GLA_QK_W, LANES), F32),
            pltpu.VMEM((2, T, GLA_V_W), BF16),
            pltpu.VMEM((2, T, GLA_V_W), F32),
            pltpu.VMEM((2, SWA_HEADS, T, LANES), BF16),
            pltpu.VMEM((2, T, SWA_KV_W), BF16),
            pltpu.VMEM((2, SWA_KV_W, T), BF16),
            pltpu.VMEM((2, WINDOW, SWA_KV_W), F32),
            pltpu.VMEM((2, WINDOW, SWA_KV_W), F32),
            pltpu.VMEM((GLA_QK_W, GLA_DV), F32),
            pltpu.VMEM((T, GLA_V_W), F32),
            pltpu.VMEM((T, D_MODEL), BF16),
            pltpu.VMEM((WINDOW, SWA_KV_W), BF16),
            pltpu.VMEM((SWA_KV_W, WINDOW), BF16),
            pltpu.VMEM((T // GLA_CHUNK, GLA_HEADS, GLA_CHUNK, LANES), BF16),
            pltpu.VMEM((T // GLA_CHUNK, GLA_QK_W, GLA_DV), F32),
            pltpu.VMEM((T // GLA_CHUNK, GLA_QK_W, GLA_DV), BF16),
            pltpu.VMEM((T // WINDOW, SWA_KV_HEADS, WINDOW, SWA_GROUP * WINDOW), F32),
            pltpu.VMEM((T // WINDOW, SWA_KV_HEADS, 2 * WINDOW, SWA_GROUP * WINDOW), BF16),
            pltpu.VMEM((T // WINDOW, SWA_KV_HEADS, 1, SWA_GROUP * WINDOW), F32),
        ],
        compiler_params=pltpu.CompilerParams(
            dimension_semantics=("arbitrary",), vmem_limit_bytes=VMEM_LIMIT),
        name="mix_prompt",
    )(sinks, x, x, *mix_w, *tabs, *(prev or ()), *(s.src for s in sides))


N_SMP_IN = 15


def _mix_sample_kernel(*refs, layer):
    (sink_ref, x_ref, s_ref, kc_ref, vc_ref, nmix_ref, win_ref, wa2_ref, ba_ref,
     glag_ref, swag_ref, wout_ref, cos_ref, sa_ref, sb_ref) = refs[:N_SMP_IN]
    xo_ref, so_ref, ko_ref, vo_ref = refs[-4:]
    R = x_ref.shape[0]
    NS = SMP_SEQS
    Q = R // NS
    q_shift = Q.bit_length() - 1

    x = x_ref[...]
    h = _rmsnorm(x, nmix_ref[...]).astype(BF16)
    proj = _dot(h, win_ref[...])
    z = _dot(proj[:, C_A:C_A + LANES].astype(BF16), wa2_ref[...]) + ba_ref[...]
    g = _log_sigmoid(z) * (1.0 / GATE_NORM)

    ri = lax.broadcasted_iota(jnp.int32, (R, R), 0)
    ci = lax.broadcasted_iota(jnp.int32, (R, R), 1)
    same = (ri >> q_shift) == (ci >> q_shift)
    causal = jnp.logical_and(same, ci <= ri)
    after = jnp.logical_and(same, ci > ri)
    b = _split_dot(jnp.where(causal, 1.0, 0.0).astype(BF16), g)
    rest = _split_dot(jnp.where(after, 1.0, 0.0).astype(BF16), g)

    qg = proj[:, C_QG:C_QG + GLA_QK_W]
    kg = proj[:, C_KG:C_KG + GLA_QK_W]
    qb = qg * QK_SCALE * jnp.exp(b)
    kd = (kg * jnp.exp(-b)).astype(BF16)
    kl_t = (kg * jnp.exp(rest)).T
    g_t = g.T

    own_tok = ((lax.broadcasted_iota(jnp.int32, (NS * GLA_DK, R), 0) >> 6)
               == (lax.broadcasted_iota(jnp.int32, (NS * GLA_DK, R), 1) >> q_shift))
    own_seq = ((lax.broadcasted_iota(jnp.int32, (R, NS * GLA_DK), 1) >> 6)
               == (lax.broadcasted_iota(jnp.int32, (R, NS * GLA_DK), 0) >> q_shift))
    lane_half = lax.broadcasted_iota(jnp.int32, (R, LANES), 1) >> 6
    ones = jnp.ones((R, GLA_DV), BF16)

    glag = glag_ref[...]
    og = []
    for hd in range(GLA_HEADS):
        ks_ = slice(hd * GLA_DK, (hd + 1) * GLA_DK)
        v = proj[:, C_VG + hd * GLA_DV:C_VG + (hd + 1) * GLA_DV].astype(BF16)
        att = jnp.where(causal, _dot_nt(qb[:, ks_].astype(BF16), kd[:, ks_]), 0.0).astype(BF16)
        o = _dot(att, v)
        s_old = s_ref[:, hd].reshape(NS * GLA_DK, GLA_DV)
        slab = qb[:, (hd // 2) * LANES:(hd // 2 + 1) * LANES]
        dup = jnp.where(lane_half == hd % 2, slab, pltpu.roll(slab, HEAD_DIM, 1))
        q_exp = jnp.where(own_seq, jnp.concatenate([dup] * (NS // 2), axis=1), 0.0).astype(BF16)
        o = o + _dot(q_exp, s_old.astype(BF16))
        o = o * lax.rsqrt(jnp.mean(o * o, axis=-1, keepdims=True) + EPS)
        rg = proj[:, C_RG + hd * GLA_DV:C_RG + (hd + 1) * GLA_DV]
        og.append(o * glag[:, hd * GLA_DV:(hd + 1) * GLA_DV] * _silu(rg))
        kl_exp = jnp.where(own_tok, jnp.concatenate([kl_t[ks_, :]] * NS, axis=0), 0.0).astype(BF16)
        g_exp = jnp.where(own_tok, jnp.concatenate([g_t[ks_, :]] * NS, axis=0), 0.0)
        b_last = _split_dot_rhs(g_exp, ones)
        s_new = jnp.exp(b_last) * s_old + _dot(kl_exp, v)
        so_ref[:, hd] = s_new.reshape(NS, GLA_DK, GLA_DV)

    cos, sa, sb = cos_ref[...], sa_ref[...], sb_ref[...]
    q_rot = _rope(proj[:, C_QS:C_QS + SWA_Q_W], cos, sa, sb) * QK_SCALE
    k_rot = _rope(proj[:, C_KS:C_KS + SWA_KV_W], cos, sa, sb)
    v_new = proj[:, C_VS:C_VS + SWA_KV_W]

    W = kc_ref.shape[1]
    SR = SMP_SUB * Q
    GR = SWA_GROUP * SR
    sub_shift = SR.bit_length() - 1
    w_shift = W.bit_length() - 1
    r_c = lax.broadcasted_iota(jnp.int32, (GR, SMP_SUB * W), 0)
    c_c = lax.broadcasted_iota(jnp.int32, (GR, SMP_SUB * W), 1)
    vis_c = jnp.logical_and(((r_c & (SR - 1)) >> q_shift) == (c_c >> w_shift),
                            (c_c & (W - 1)) > (r_c & (Q - 1)))
    r_n = lax.broadcasted_iota(jnp.int32, (GR, SR), 0)
    c_n = lax.broadcasted_iota(jnp.int32, (GR, SR), 1)
    vis_n = jnp.logical_and(((r_n & (SR - 1)) >> q_shift) == (c_n >> q_shift),
                            (c_n & (Q - 1)) <= (r_n & (Q - 1)))
    grp_row = lax.broadcasted_iota(jnp.int32, (GR, 1), 0) >> sub_shift
    half_sr = lax.broadcasted_iota(jnp.int32, (SR, LANES), 1) >> 6
    half_gr = lax.broadcasted_iota(jnp.int32, (GR, LANES), 1) >> 6

    o_rows = []
    for sg in range(NS // SMP_SUB):
        rows = slice(sg * SR, (sg + 1) * SR)
        k_c = kc_ref[sg * SMP_SUB:(sg + 1) * SMP_SUB].reshape(SMP_SUB * W, SWA_KV_W).astype(BF16)
        v_c = vc_ref[sg * SMP_SUB:(sg + 1) * SMP_SUB].reshape(SMP_SUB * W, SWA_KV_W).astype(BF16)
        k_n = k_rot[rows, :].astype(BF16)
        v_n = v_new[rows, :].astype(BF16)
        o_kv = []
        for j in range(SWA_KV_HEADS):
            parts = []
            for gq in range(SWA_GROUP):
                hq = j * SWA_GROUP + gq
                slab = q_rot[rows, (hq // 2) * LANES:(hq // 2 + 1) * LANES]
                if hq % 2 != j:
                    slab = pltpu.roll(slab, HEAD_DIM, 1)
                parts.append(jnp.where(half_sr == j, slab, 0.0))
            q8 = jnp.concatenate(parts, axis=0).astype(BF16)
            s_c = jnp.where(vis_c, _dot_nt(q8, k_c), -jnp.inf)
            s_n = jnp.where(vis_n, _dot_nt(q8, k_n), -jnp.inf)
            sink = jnp.zeros((GR, 1), F32)
            for gq in range(SWA_GROUP):
                sink = jnp.where(grp_row == gq, sink_ref[layer, j * SWA_GROUP + gq], sink)
            m = jnp.maximum(jnp.maximum(jnp.max(s_c, axis=-1, keepdims=True),
                                        jnp.max(s_n, axis=-1, keepdims=True)), sink)
            p_c = jnp.exp(s_c - m)
            p_n = jnp.exp(s_n - m)
            den = (jnp.sum(p_c, axis=-1, keepdims=True) + jnp.sum(p_n, axis=-1, keepdims=True)
                   + jnp.exp(sink - m))
            o = _dot((p_c / den).astype(BF16), v_c) + _dot((p_n / den).astype(BF16), v_n)
            o_kv.append(o)
        slabs = []
        for p in range(SWA_HEADS // 2):
            halves = []
            for hq in (2 * p, 2 * p + 1):
                j, gq = hq // SWA_GROUP, hq % SWA_GROUP
                piece = o_kv[j][gq * SR:(gq + 1) * SR, :]
                if hq % 2 != j:
                    piece = pltpu.roll(piece, HEAD_DIM, 1)
                halves.append(piece)
            slabs.append(jnp.where(half_sr == 0, halves[0], halves[1]))
        o_rows.append(jnp.concatenate(slabs, axis=1))
    o_s = _rmsnorm(jnp.concatenate(o_rows, axis=0), swag_ref[...])

    cat = jnp.concatenate(og + [o_s], axis=1).astype(BF16)
    xo_ref[...] = x + _dot(cat, wout_ref[...])

    ko_ref[:, 0:W - Q, :] = kc_ref[:, Q:W, :]
    vo_ref[:, 0:W - Q, :] = vc_ref[:, Q:W, :]
    for s in range(NS):
        ko_ref[s, W - Q:W, :] = k_rot[s * Q:(s + 1) * Q, :]
        vo_ref[s, W - Q:W, :] = v_new[s * Q:(s + 1) * Q, :]


def _split_dot_rhs(g, m_bf16):
    hi = g.astype(BF16)
    lo = (g - hi.astype(F32)).astype(BF16)
    return _dot(hi, m_bf16) + _dot(lo, m_bf16)


def _mix_sample(x2d, state, kc, vc, sinks, mix_w, tabs, layer, prev):
    depth, NB, W = kc.shape[0], kc.shape[1], kc.shape[2]
    Q = x2d.shape[0] // NB
    R = SMP_SEQS * Q
    tab_spec = pl.BlockSpec((R, LANES), lambda i: (0, 0))
    st_spec = pl.BlockSpec((None, SMP_SEQS, GLA_HEADS, GLA_DK, GLA_DV), lambda i: (layer, i, 0, 0, 0))
    kv_spec = pl.BlockSpec((None, SMP_SEQS, W, SWA_KV_W), lambda i: (layer, i, 0, 0))
    n_prev = 0 if prev is None else len(prev)
    return pl.pallas_call(
        functools.partial(_mix_sample_kernel, layer=layer),
        grid=(NB // SMP_SEQS,),
        in_specs=[
            pl.BlockSpec(memory_space=pltpu.SMEM),
            pl.BlockSpec((R, D_MODEL), lambda i: (i, 0)),
            st_spec, kv_spec, kv_spec,
            *_mixer_weight_specs(layer),
            tab_spec, tab_spec, tab_spec,
            *([pl.BlockSpec(memory_space=pl.ANY)] * n_prev),
        ],
        out_specs=[pl.BlockSpec((R, D_MODEL), lambda i: (i, 0)), st_spec, kv_spec, kv_spec],
        out_shape=[
            jax.ShapeDtypeStruct(x2d.shape, F32),
            jax.ShapeDtypeStruct(state.shape, F32),
            jax.ShapeDtypeStruct(kc.shape, F32),
            jax.ShapeDtypeStruct(vc.shape, F32),
        ],
        input_output_aliases={N_SMP_IN + k: 1 + k for k in range(n_prev)},
        compiler_params=pltpu.CompilerParams(
            dimension_semantics=("arbitrary",), vmem_limit_bytes=VMEM_LIMIT),
        name="mix_sample",
    )(sinks, x2d, state, kc, vc, *mix_w, *tabs, *(prev or ()))


def _rope_tables(pos):
    half = ROT_DIM // 2
    inv = jnp.power(jnp.float32(ROPE_THETA), -jnp.arange(half, dtype=jnp.float32) * (2.0 / ROT_DIM))
    ang = pos.astype(jnp.float32)[:, None] * inv[None, :]
    cos, sin = jnp.cos(ang), jnp.sin(ang)
    n = pos.shape[0]
    pad = jnp.zeros((n, HEAD_DIM - ROT_DIM), F32)
    zero = jnp.zeros((n, half), F32)
    c64 = jnp.concatenate([cos, cos, pad + 1.0], axis=1)
    a64 = jnp.concatenate([zero, sin, pad], axis=1)
    b64 = jnp.concatenate([-sin, zero, pad], axis=1)
    rep = LANES // HEAD_DIM
    return tuple(jnp.tile(t, (1, rep)) for t in (c64, a64, b64))


def kernel(x_prompt, x_sample, state_gla, cache_swa_k, cache_swa_v, norm_ffn1, w_ffn1_gu, w_ffn1_down,
           norm_mix, w_in, w_gate_up, b_gate, gla_norm, attn_sinks, swa_norm, w_out, norm_ffn2,
           w_ffn2_gu, w_ffn2_down, norm_final):
    B, L, _ = x_prompt.shape
    NB, Q, _ = x_sample.shape
    depth = w_in.shape[0]
    W = cache_swa_k.shape[2]

    tabs_p = _rope_tables(jnp.arange(L))
    tabs_s = _rope_tables(jnp.tile(PAST_LEN + jnp.arange(Q), SMP_SEQS))

    wa2 = jnp.pad(w_gate_up, ((0, 0), (0, LANES - GATE_RANK), (0, 0))).astype(BF16)
    row = lambda p: p.reshape(depth, 1, -1)
    n1, n2 = row(norm_ffn1), row(norm_ffn2)
    gu_rows = D_MODEL // (B * L // FFN_TILE)
    dn_rows = D_FF // 11

    def ffn_sides(w_gu, w_down, layer):
        return (SideCast(w_gu, layer, gu_rows), SideCast(w_down, layer, dn_rows))

    xp = x_prompt.reshape(B * L, D_MODEL)
    xs = x_sample.reshape(NB * Q, D_MODEL)
    kc = cache_swa_k.reshape(depth, NB, W, SWA_KV_W)
    vc = cache_swa_v.reshape(depth, NB, W, SWA_KV_W)
    out_p = out_s = None
    gu1, dn1 = _cast_layers((SideCast(w_ffn1_gu, 0, D_MODEL // 8), SideCast(w_ffn1_down, 0, D_FF // 4)))
    for l in range(depth):
        nf = norm_final if l == depth - 1 else None
        xp, win, wout = _ffn(xp, n1, gu1, dn1, l, sides=(SideCast(w_in, l, gu_rows, True),
                                                        SideCast(w_out, l, gu_rows)))
        xs = _ffn(xs, n1, gu1, dn1, l)
        mix_w = (row(norm_mix), win, wa2, row(b_gate), row(gla_norm), row(swa_norm), wout)
        xp3, *out_p = _mix_prompt(xp.reshape(B, L, D_MODEL), attn_sinks, mix_w, tabs_p, l, depth, out_p,
                                  sides=ffn_sides(w_ffn2_gu, w_ffn2_down, l))
        gu2, dn2 = out_p[3:]
        out_p = out_p[:3]
        xs, *out_s = _mix_sample(xs, state_gla, kc, vc, attn_sinks, mix_w, tabs_s, l, out_s)
        xs = _ffn(xs, n2, gu2, dn2, l, nf)
        xp3 = xp3.reshape(B * L, D_MODEL)
        if l + 1 < depth:
            xp, gu1, dn1 = _ffn(xp3, n2, gu2, dn2, l, nf, sides=ffn_sides(w_ffn1_gu, w_ffn1_down, l + 1))
        else:
            xp = _ffn(xp3, n2, gu2, dn2, l, nf)

    gla_p, k_p, v_p = out_p
    gla_s, k_s, v_s = out_s
    kv5 = lambda t: t.reshape(*t.shape[:3], SWA_KV_HEADS, HEAD_DIM)
    return (xp.reshape(B, L, D_MODEL), xs.reshape(NB, Q, D_MODEL), gla_p, kv5(k_p), kv5(v_p),
            gla_s, kv5(k_s), kv5(v_s))
```

```python
import functools
from typing import NamedTuple

import jax
import jax.numpy as jnp
from jax import lax
from jax.experimental import pallas as pl
from jax.experimental.pallas import tpu as pltpu

F32 = jnp.float32
BF16 = jnp.bfloat16

D_MODEL = 1024
D_FF = 2816
EPS = 1e-6
GLA_HEADS = 4
GLA_DK = 64
GLA_DV = 128
GLA_QK_W = GLA_HEADS * GLA_DK
GLA_V_W = GLA_HEADS * GLA_DV
GATE_RANK = 16
GATE_NORM = 16.0
GLA_CHUNK = 64
HEAD_DIM = 64
SWA_HEADS = 8
SWA_KV_HEADS = 2
SWA_GROUP = SWA_HEADS // SWA_KV_HEADS
SWA_Q_W = SWA_HEADS * HEAD_DIM
SWA_KV_W = SWA_KV_HEADS * HEAD_DIM
WINDOW = 128
ROT_DIM = 16
ROPE_THETA = 500000.0
PAST_LEN = 16384
QK_SCALE = 0.125

LANES = 128
C_QG, C_KG, C_VG, C_RG = 0, 256, 512, 1024
C_QS, C_KS, C_VS, C_A = 1536, 2048, 2176, 2304
IN_W_PAD = C_A + LANES

FFN_TILE = 512
FFN_CHUNK = 256
MIX_TILE = 512
SMP_SEQS = 32
SMP_SUB = 8
VMEM_LIMIT = 56 * 1024 * 1024

NT_DIMS = (((1,), (1,)), ((), ()))


def _dot(a, b):
    return jnp.dot(a, b, preferred_element_type=F32)


def _dot_nt(a, b):
    return lax.dot_general(a, b, NT_DIMS, preferred_element_type=F32)


def _rmsnorm(x, g):
    return x * lax.rsqrt(jnp.mean(x * x, axis=-1, keepdims=True) + EPS) * g


def _log_sigmoid(z):
    return jnp.minimum(z, 0.0) - jnp.log1p(jnp.exp(-jnp.abs(z)))


def _silu(z):
    return z * jax.nn.sigmoid(z)


def _split_dot(m_bf16, g):
    hi = g.astype(BF16)
    lo = (g - hi.astype(F32)).astype(BF16)
    return _dot(m_bf16, hi) + _dot(m_bf16, lo)


def _rope(x, cos, sa, sb):
    half = ROT_DIM // 2
    outs = []
    for p in range(x.shape[1] // LANES):
        s = x[:, p * LANES:(p + 1) * LANES]
        outs.append(s * cos + pltpu.roll(s, half, 1) * sa + pltpu.roll(s, LANES - half, 1) * sb)
    return outs[0] if len(outs) == 1 else jnp.concatenate(outs, axis=1)


class SideCast(NamedTuple):
    src: jax.Array
    layer: int
    block_rows: int
    reorder_w_in: bool = False

    @property
    def n_blocks(self):
        return self.src.shape[1] // self.block_rows

    @property
    def out_cols(self):
        return IN_W_PAD if self.reorder_w_in else self.src.shape[2]


def _cast_block(src_ref, dst_ref, reorder_w_in):
    w = src_ref[...]
    if not reorder_w_in:
        dst_ref[...] = w.astype(BF16)
        return
    n_a = C_RG + GLA_V_W
    dst_ref[:, 0:n_a] = w[:, 0:n_a].astype(BF16)
    dst_ref[:, n_a:C_A] = w[:, n_a + GATE_RANK:].astype(BF16)
    pad = jnp.zeros((w.shape[0], LANES - GATE_RANK), F32)
    dst_ref[:, C_A:IN_W_PAD] = jnp.concatenate([w[:, n_a:n_a + GATE_RANK], pad], axis=1).astype(BF16)


def _side_cast_specs(sides):
    in_specs, out_specs, out_shapes = [], [], []
    for s in sides:
        last = s.n_blocks - 1
        in_specs.append(pl.BlockSpec((None, s.block_rows, s.src.shape[2]),
                                     lambda i, layer=s.layer, last=last: (layer, jnp.minimum(i, last), 0)))
        out_specs.append(pl.BlockSpec((s.block_rows, s.out_cols), lambda i, last=last: (jnp.minimum(i, last), 0)))
        out_shapes.append(jax.ShapeDtypeStruct((s.src.shape[1], s.out_cols), BF16))
    return in_specs, out_specs, out_shapes


def _run_side_casts(step, sides, src_refs, dst_refs):
    for (n_blocks, reorder), src, dst in zip(sides, src_refs, dst_refs):
        @pl.when(step < n_blocks)
        def _():
            _cast_block(src, dst, reorder)


def _cast_kernel(*refs, sides):
    n = len(sides)
    _run_side_casts(pl.program_id(0), sides, refs[:n], refs[n:])


def _cast_layers(sides):
    in_specs, out_specs, out_shapes = _side_cast_specs(sides)
    return pl.pallas_call(
        functools.partial(_cast_kernel, sides=tuple((s.n_blocks, s.reorder_w_in) for s in sides)),
        grid=(max(s.n_blocks for s in sides),),
        in_specs=in_specs,
        out_specs=out_specs,
        out_shape=out_shapes,
        compiler_params=pltpu.CompilerParams(
            dimension_semantics=("arbitrary",), vmem_limit_bytes=VMEM_LIMIT),
        name="cast_weights",
    )(*(s.src for s in sides))


def _ffn_kernel(*refs, final, sides):
    n_in = 5 if final else 4
    n_side = len(sides)
    x_ref, n_ref, wgu_ref, wd_ref = refs[:4]
    nf_ref = refs[4] if final else None
    o_ref = refs[n_in + n_side]
    acc_ref = refs[-1]
    _run_side_casts(pl.program_id(0), sides, refs[n_in:n_in + n_side],
                    refs[n_in + n_side + 1:n_in + 2 * n_side + 1])
    x = x_ref[...]
    h = _rmsnorm(x, n_ref[...]).astype(BF16)
    for c in range(D_FF // FFN_CHUNK):
        lo = c * FFN_CHUNK
        gate = _dot(h, wgu_ref[:, lo:lo + FFN_CHUNK])
        up = _dot(h, wgu_ref[:, D_FF + lo:D_FF + lo + FFN_CHUNK])
        act = (_silu(gate) * up).astype(BF16)
        part = _dot(act, wd_ref[lo:lo + FFN_CHUNK, :])
        if c == 0:
            acc_ref[...] = part
        else:
            acc_ref[...] += part
    y = x + 0.5 * acc_ref[...]
    if final:
        y = _rmsnorm(y, nf_ref[...])
    o_ref[...] = y


def _ffn(x2d, norm, w_gu, w_down, layer, norm_final=None, sides=()):
    final = norm_final is not None
    const = lambda i: (0, 0)
    tile = pl.BlockSpec((FFN_TILE, D_MODEL), lambda i: (i, 0))
    in_specs = [
        tile,
        pl.BlockSpec((None, 1, D_MODEL), lambda i: (layer, 0, 0)),
        pl.BlockSpec((D_MODEL, 2 * D_FF), const, pipeline_mode=pl.Buffered(1)),
        pl.BlockSpec((D_FF, D_MODEL), const, pipeline_mode=pl.Buffered(1)),
    ]
    args = [x2d, norm, w_gu, w_down]
    if final:
        in_specs.append(pl.BlockSpec((1, D_MODEL), const))
        args.append(norm_final.reshape(1, D_MODEL))
    side_in, side_out, side_shapes = _side_cast_specs(sides)
    outs = pl.pallas_call(
        functools.partial(_ffn_kernel, final=final,
                          sides=tuple((s.n_blocks, s.reorder_w_in) for s in sides)),
        grid=(x2d.shape[0] // FFN_TILE,),
        in_specs=in_specs + side_in,
        out_specs=[tile] + side_out,
        out_shape=[jax.ShapeDtypeStruct(x2d.shape, F32)] + side_shapes,
        scratch_shapes=[pltpu.VMEM((FFN_TILE, D_MODEL), F32)],
        compiler_params=pltpu.CompilerParams(
            dimension_semantics=("arbitrary",), vmem_limit_bytes=VMEM_LIMIT),
        name="ffn_final" if final else "ffn",
    )(*args, *(s.src for s in sides))
    return outs if sides else outs[0]


N_MIX_IN = 12


def _mix_project(x_ref, nmix_ref, win_ref, wa2_ref, ba_ref, glag_ref, cos_ref, sa_ref, sb_ref,
                 qb_s, qsw_s, kd_s, klt_s, dect_s, v_s, gate_s, qh_s, kt_s, vt_s, ko_ref, vo_ref):
    T = MIX_TILE
    G = 256

    x = x_ref[...]
    h = _rmsnorm(x, nmix_ref[...]).astype(BF16)

    a_low = _dot(h, win_ref[:, C_A:C_A + LANES]).astype(BF16)
    z = _dot(a_low, wa2_ref[...]) + ba_ref[...]
    g = _log_sigmoid(z) * (1.0 / GATE_NORM)
    qk = _dot(h, win_ref[:, C_QG:C_QG + 2 * GLA_QK_W])

    ri = lax.broadcasted_iota(jnp.int32, (G, G), 0)
    ci = lax.broadcasted_iota(jnp.int32, (G, G), 1)
    same_chunk = (ri >> 6) == (ci >> 6)
    incl = jnp.where(jnp.logical_and(same_chunk, ci <= ri), 1.0, 0.0).astype(BF16)
    C = GLA_CHUNK
    chunk_last = []
    for r in range(T // G):
        rows = slice(r * G, (r + 1) * G)
        b = _split_dot(incl, g[rows])
        lasts = [b[c * C + C - 1:(c + 1) * C, :] for c in range(G // C)]
        chunk_last.extend(lasts)
        rest = jnp.concatenate([jnp.broadcast_to(t, (C, GLA_QK_W)) for t in lasts], axis=0) - b
        q = qk[rows, 0:GLA_QK_W]
        k = qk[rows, GLA_QK_W:2 * GLA_QK_W]
        qb = q * QK_SCALE * jnp.exp(b)
        qb_s[rows, :] = qb.astype(BF16)
        qsw_s[rows, :] = jnp.concatenate(
            [pltpu.roll(qb[:, p * LANES:(p + 1) * LANES], GLA_DK, 1) for p in range(GLA_QK_W // LANES)],
            axis=1)
        kd_s[rows, :] = k * jnp.exp(-b)
        klt_s[:, rows] = (k * jnp.exp(rest)).T.astype(BF16)
    pad = jnp.zeros((LANES - len(chunk_last), GLA_QK_W), F32)
    dect_s[...] = jnp.exp(jnp.concatenate(chunk_last + [pad], axis=0)).T

    v_s[...] = _dot(h, win_ref[:, C_VG:C_VG + GLA_V_W]).astype(BF16)
    gate_s[...] = _silu(_dot(h, win_ref[:, C_RG:C_RG + GLA_V_W])) * glag_ref[...]

    cos, sa, sb = cos_ref[...], sa_ref[...], sb_ref[...]
    half_t = lax.broadcasted_iota(jnp.int32, (T, LANES), 1) >> 6
    q_rot = _rope(_dot(h, win_ref[:, C_QS:C_QS + SWA_Q_W]), cos, sa, sb) * QK_SCALE
    for hq in range(SWA_HEADS):
        slab = q_rot[:, (hq // 2) * LANES:(hq // 2 + 1) * LANES]
        j = hq // SWA_GROUP
        if hq % 2 != j:
            slab = pltpu.roll(slab, HEAD_DIM, 1)
        qh_s[hq] = jnp.where(half_t == j, slab, 0.0).astype(BF16)
    kv =_dot(h, win_ref[:, C_KS:C_KS + 2 * SWA_KV_W])
    k_rot = _rope(kv[:, 0:SWA_KV_W], cos, sa, sb)
    v_swa = kv[:, SWA_KV_W:2 * SWA_KV_W]
    kt_s[...] = k_rot.astype(BF16)
    vt_s[...] = v_swa.T.astype(BF16)
    ko_ref[...] = k_rot[T - WINDOW:T, :]
    vo_ref[...] = v_swa[T - WINDOW:T, :]


def _mix_sequential(first, layer, sink_ref, x_ref, swag_ref, wout_ref, xo_ref,
                    qb_s, qsw_s, kd_s, klt_s, dect_s, v_s, gate_s, qh_s, kt_s, vt_s,
                    s_s, og_s, cat_s, kcar_s, vcar_s, lhs_s, upd_s, st16_s, sc_s, pr_s, rden_s):
    T = MIX_TILE
    C = GLA_CHUNK

    hk_r = lax.broadcasted_iota(jnp.int32, (GLA_QK_W, GLA_QK_W), 0) >> 6
    hk_c = lax.broadcasted_iota(jnp.int32, (GLA_QK_W, GLA_QK_W), 1) >> 6
    bd_k = hk_r == hk_c
    qi = lax.broadcasted_iota(jnp.int32, (C, GLA_QK_W), 0)
    kj = lax.broadcasted_iota(jnp.int32, (C, GLA_QK_W), 1) & (C - 1)
    causal = kj <= qi
    half_c = lax.broadcasted_iota(jnp.int32, (C, LANES), 1) >> 6
    zeros_v = jnp.zeros((C, GLA_DV), BF16)
    n_c = T // C
    for c in range(n_c):
        rows = slice(c * C, (c + 1) * C)
        kd_bd = jnp.where(bd_k, jnp.concatenate([kd_s[rows, :]] * GLA_HEADS, axis=0), 0.0).astype(BF16)
        att = jnp.where(causal, _dot_nt(qb_s[rows, :], kd_bd), 0.0)
        tok = slice((c // 2) * 2 * C, (c // 2 + 1) * 2 * C)
        for hd in range(GLA_HEADS):
            slab = slice((hd // 2) * LANES, (hd // 2 + 1) * LANES)
            hk = slice(hd * GLA_DK, (hd + 1) * GLA_DK)
            v = v_s[rows, hd * GLA_DV:(hd + 1) * GLA_DV]
            lhs_s[c, hd] = jnp.where(half_c == hd % 2, att[:, slab], qsw_s[rows, slab]).astype(BF16)
            v_pair = jnp.concatenate([v, zeros_v] if c % 2 == 0 else [zeros_v, v], axis=0)
            upd_s[c, hk, :] = _dot(klt_s[hk, tok], v_pair)
    state = s_s[...]
    for c in range(n_c):
        st16_s[c] = state.astype(BF16)
        state = dect_s[:, c:c + 1] * state + upd_s[c]
    s_s[...] = state
    for c in range(n_c):
        rows = slice(c * C, (c + 1) * C)
        for hd in range(GLA_HEADS):
            hk = slice(hd * GLA_DK, (hd + 1) * GLA_DK)
            v = v_s[rows, hd * GLA_DV:(hd + 1) * GLA_DV]
            st = st16_s[c, hk, :]
            rhs = jnp.concatenate([v, st] if hd % 2 == 0 else [st, v], axis=0)
            og_s[rows, hd * GLA_DV:(hd + 1) * GLA_DV] = _dot(lhs_s[c, hd], rhs)

    for hd in range(GLA_HEADS):
        vs_ = slice(hd * GLA_DV, (hd + 1) * GLA_DV)
        o = og_s[:, vs_]
        o = o * lax.rsqrt(jnp.mean(o * o, axis=-1, keepdims=True) + EPS)
        cat_s[:, vs_] = (o * gate_s[:, vs_]).astype(BF16)

    swag = swag_ref[...]
    GR = SWA_GROUP * WINDOW
    kc_ = lax.broadcasted_iota(jnp.int32, (WINDOW, GR), 0)
    qr = lax.broadcasted_iota(jnp.int32, (WINDOW, GR), 1) & (WINDOW - 1)
    from_prev = kc_ > qr
    prev_bias = jnp.where(first > 0.0, -jnp.inf, 0.0)
    n_b = T // WINDOW
    sinks = [jnp.concatenate(
        [jnp.full((1, WINDOW), sink_ref[layer, j * SWA_GROUP + gq], F32) for gq in range(SWA_GROUP)], axis=1)
        for j in range(SWA_KV_HEADS)]
    for i in range(n_b):
        rows = slice(i * WINDOW, (i + 1) * WINDOW)
        if i == 0:
            k_blk = jnp.concatenate([kcar_s[...], kt_s[0:WINDOW, :]], axis=0)
        else:
            k_blk = kt_s[(i - 1) * WINDOW:(i + 1) * WINDOW, :]
        for j in range(SWA_KV_HEADS):
            q4 = jnp.concatenate([qh_s[j * SWA_GROUP + gq, rows, :] for gq in range(SWA_GROUP)], axis=0)
            s2 = _dot_nt(k_blk, q4)
            s_prev = s2[0:WINDOW, :] + prev_bias if i == 0 else s2[0:WINDOW, :]
            sc_s[i, j] = jnp.where(from_prev, s_prev, s2[WINDOW:2 * WINDOW, :])
    for i in range(n_b):
        for j in range(SWA_KV_HEADS):
            s = sc_s[i, j]
            m = jnp.maximum(jnp.max(s, axis=0, keepdims=True), sinks[j])
            pr = jnp.exp(s - m)
            rden_s[i, j] = 1.0 / (jnp.sum(pr, axis=0, keepdims=True) + jnp.exp(sinks[j] - m))
            pr_s[i, j, 0:WINDOW, :] = jnp.where(from_prev, pr, 0.0).astype(BF16)
            pr_s[i, j, WINDOW:2 * WINDOW, :] = jnp.where(from_prev, 0.0, pr).astype(BF16)
    for i in range(n_b):
        rows = slice(i * WINDOW, (i + 1) * WINDOW)
        if i == 0:
            vt_blk = jnp.concatenate([vcar_s[...], vt_s[:, 0:WINDOW]], axis=1)
        else:
            vt_blk = vt_s[:, (i - 1) * WINDOW:(i + 1) * WINDOW]
        pieces = []
        for j in range(SWA_KV_HEADS):
            vt_j = vt_blk[j * HEAD_DIM:(j + 1) * HEAD_DIM, :]
            o_t = _dot(vt_j, pr_s[i, j]) * rden_s[i, j]
            pieces.extend(o_t[:, gq * WINDOW:(gq + 1) * WINDOW] for gq in range(SWA_GROUP))
        o_s = jnp.concatenate(pieces, axis=0).T
        cat_s[rows, GLA_V_W:GLA_V_W + SWA_Q_W] = _rmsnorm(o_s, swag).astype(BF16)

    xo_ref[...] =(x_ref[...] + _dot(cat_s[:, 0:GLA_V_W], wout_ref[0:GLA_V_W, :])
                   + _dot(cat_s[:, GLA_V_W:D_MODEL], wout_ref[GLA_V_W:D_MODEL, :]))

    kcar_s[...] = kt_s[T - WINDOW:T, :]
    vcar_s[...] = vt_s[:, T - WINDOW:T]


N_MIX_OUT = 4
N_MIX_SCRATCH = 21


def _mix_prompt_kernel(*refs, layer, n_l, n_prev, sides):
    (sink_ref, x_ref, nmix_ref, win_ref, wa2_ref, ba_ref, glag_ref, swag_ref,
     wout_ref, cos_ref, sa_ref, sb_ref) = refs[:N_MIX_IN]
    n_side = len(sides)
    side_src = refs[N_MIX_IN + n_prev:N_MIX_IN + n_prev + n_side]
    outs = refs[N_MIX_IN + n_prev + n_side:len(refs) - N_MIX_SCRATCH]
    xo_ref, so_ref, ko_ref, vo_ref = outs[:N_MIX_OUT]
    (qb_s, qsw_s, kd_s, klt_s, dect_s, v_s, gate_s, qh_s, kt_s, vt_s,
     s_s, og_s, cat_s, kcar_s, vcar_s, lhs_s, upd_s, st16_s, sc_s, pr_s, rden_s) = refs[-N_MIX_SCRATCH:]
    handover = (qb_s, qsw_s, kd_s, klt_s, dect_s, v_s, gate_s, qh_s, kt_s, vt_s)
    t = pl.program_id(0)
    l_seq = t % n_l
    _run_side_casts(t, sides, side_src, outs[N_MIX_OUT:])

    @pl.when(l_seq == 0)
    def _():
        s_s[...] = jnp.zeros_like(s_s)
        kcar_s[...] = jnp.zeros_like(kcar_s)
        vcar_s[...] = jnp.zeros_like(vcar_s)

    first = jnp.where(l_seq == 0, 1.0, 0.0)
    _mix_project(x_ref, nmix_ref, win_ref, wa2_ref, ba_ref, glag_ref, cos_ref, sa_ref, sb_ref,
                 *handover, ko_ref, vo_ref)
    _mix_sequential(first, layer, sink_ref, x_ref, swag_ref, wout_ref, xo_ref,
                    *handover, s_s, og_s, cat_s, kcar_s, vcar_s, lhs_s, upd_s, st16_s, sc_s, pr_s, rden_s)

    @pl.when(l_seq == n_l - 1)
    def _():
        for hd in range(GLA_HEADS):
            so_ref[hd] = s_s[hd * GLA_DK:(hd + 1) * GLA_DK, :]


def _mixer_weight_specs(layer):
    at_layer = lambda i: (layer, 0, 0)
    const = lambda i: (0, 0)
    return [
        pl.BlockSpec((None, 1, D_MODEL), at_layer),
        pl.BlockSpec((D_MODEL, IN_W_PAD), const),
        pl.BlockSpec((None, LANES, GLA_QK_W), at_layer),
        pl.BlockSpec((None, 1, GLA_QK_W), at_layer),
        pl.BlockSpec((None, 1, GLA_V_W), at_layer),
        pl.BlockSpec((None, 1, SWA_Q_W), at_layer),
        pl.BlockSpec((D_MODEL, D_MODEL), const),
    ]


def _mix_prompt(x, sinks, mix_w, tabs, layer, depth, prev, sides=()):
    B, L, _ = x.shape
    T = MIX_TILE
    n_l = L // T
    x_spec = pl.BlockSpec((None, T, D_MODEL), lambda t: (t // n_l, t % n_l, 0))
    tab_spec = pl.BlockSpec((T, LANES), lambda t: (t % n_l, 0))
    n_prev = 0 if prev is None else len(prev)
    side_in, side_out, side_shapes = _side_cast_specs(sides)
    return pl.pallas_call(
        functools.partial(_mix_prompt_kernel, layer=layer, n_l=n_l, n_prev=n_prev,
                          sides=tuple((s.n_blocks, s.reorder_w_in) for s in sides)),
        grid=(B * n_l,),
        in_specs=[
            pl.BlockSpec(memory_space=pltpu.SMEM),
            x_spec,
            *_mixer_weight_specs(layer),
            tab_spec, tab_spec, tab_spec,
            *([pl.BlockSpec(memory_space=pl.ANY)] * n_prev),
            *side_in,
        ],
        out_specs=[
            x_spec,
            pl.BlockSpec((None, None, GLA_HEADS, GLA_DK, GLA_DV), lambda t: (layer, t // n_l, 0, 0, 0)),
            pl.BlockSpec((None, None, WINDOW, SWA_KV_W), lambda t: (layer, t // n_l, 0, 0)),
            pl.BlockSpec((None, None, WINDOW, SWA_KV_W), lambda t: (layer, t // n_l, 0, 0)),
            *side_out,
        ],
        out_shape=[
            jax.ShapeDtypeStruct((B, L, D_MODEL), F32),
            jax.ShapeDtypeStruct((depth, B, GLA_HEADS, GLA_DK, GLA_DV), F32),
            jax.ShapeDtypeStruct((depth, B, WINDOW, SWA_KV_W), F32),
            jax.ShapeDtypeStruct((depth, B, WINDOW, SWA_KV_W), F32),
            *side_shapes,
        ],
        input_output_aliases={N_MIX_IN + k: 1 + k for k in range(n_prev)},
        scratch_shapes=[
            pltpu.VMEM((T, GLA_QK_W), BF16),
            pltpu.VMEM((T, GLA_QK_W), F32),
            pltpu.VMEM((T, GLA_QK_W), F32),
            pltpu.VMEM((GLA_QK_W, T), BF16),
            pltpu.VMEM((GLA_QK_W, LANES), F32),
            pltpu.VMEM((T, GLA_V_W), BF16),
            pltpu.VMEM((T, GLA_V_W), F32),
            pltpu.VMEM((SWA_HEADS, T, LANES), BF16),
            pltpu.VMEM((T, SWA_KV_W), BF16),
            pltpu.VMEM((SWA_KV_W, T), BF16),
            pltpu.VMEM((GLA_QK_W, GLA_DV), F32),
            pltpu.VMEM((T, GLA_V_W), F32),
            pltpu.VMEM((T, D_MODEL), BF16),
            pltpu.VMEM((WINDOW, SWA_KV_W), BF16),
            pltpu.VMEM((SWA_KV_W, WINDOW), BF16),
            pltpu.VMEM((T // GLA_CHUNK, GLA_HEADS, GLA_CHUNK, LANES), BF16),
            pltpu.VMEM((T // GLA_CHUNK, GLA_QK_W, GLA_DV), F32),
            pltpu.VMEM((T // GLA_CHUNK, GLA_QK_W, GLA_DV), BF16),
            pltpu.VMEM((T // WINDOW, SWA_KV_HEADS, WINDOW, SWA_GROUP * WINDOW), F32),
            pltpu.VMEM((T // WINDOW, SWA_KV_HEADS, 2 * WINDOW, SWA_GROUP * WINDOW), BF16),
            pltpu.VMEM((T // WINDOW, SWA_KV_HEADS, 1, SWA_GROUP * WINDOW), F32),
        ],
        compiler_params=pltpu.CompilerParams(
            dimension_semantics=("arbitrary",), vmem_limit_bytes=VMEM_LIMIT),
        name="mix_prompt",
    )(sinks, x, *mix_w, *tabs, *(prev or ()), *(s.src for s in sides))


N_SMP_IN = 15


def _mix_sample_kernel(*refs, layer):
    (sink_ref, x_ref, s_ref, kc_ref, vc_ref, nmix_ref, win_ref, wa2_ref, ba_ref,
     glag_ref, swag_ref, wout_ref, cos_ref, sa_ref, sb_ref) = refs[:N_SMP_IN]
    xo_ref, so_ref, ko_ref, vo_ref = refs[-4:]
    R = x_ref.shape[0]
    NS = SMP_SEQS
    Q = R // NS
    q_shift = Q.bit_length() - 1

    x = x_ref[...]
    h = _rmsnorm(x, nmix_ref[...]).astype(BF16)
    proj = _dot(h, win_ref[...])
    z = _dot(proj[:, C_A:C_A + LANES].astype(BF16), wa2_ref[...]) + ba_ref[...]
    g = _log_sigmoid(z) * (1.0 / GATE_NORM)

    ri = lax.broadcasted_iota(jnp.int32, (R, R), 0)
    ci = lax.broadcasted_iota(jnp.int32, (R, R), 1)
    same = (ri >> q_shift) == (ci >> q_shift)
    causal = jnp.logical_and(same, ci <= ri)
    after = jnp.logical_and(same, ci > ri)
    b = _split_dot(jnp.where(causal, 1.0, 0.0).astype(BF16), g)
    rest = _split_dot(jnp.where(after, 1.0, 0.0).astype(BF16), g)

    qg = proj[:, C_QG:C_QG + GLA_QK_W]
    kg = proj[:, C_KG:C_KG + GLA_QK_W]
    qb = qg * QK_SCALE * jnp.exp(b)
    kd = (kg * jnp.exp(-b)).astype(BF16)
    kl_t = (kg * jnp.exp(rest)).T
    g_t = g.T

    own_tok = ((lax.broadcasted_iota(jnp.int32, (NS * GLA_DK, R), 0) >> 6)
               == (lax.broadcasted_iota(jnp.int32, (NS * GLA_DK, R), 1) >> q_shift))
    own_seq = ((lax.broadcasted_iota(jnp.int32, (R, NS * GLA_DK), 1) >> 6)
               == (lax.broadcasted_iota(jnp.int32, (R, NS * GLA_DK), 0) >> q_shift))
    lane_half = lax.broadcasted_iota(jnp.int32, (R, LANES), 1) >> 6
    ones = jnp.ones((R, GLA_DV), BF16)

    glag = glag_ref[...]
    og = []
    for hd in range(GLA_HEADS):
        ks_ = slice(hd * GLA_DK, (hd + 1) * GLA_DK)
        v = proj[:, C_VG + hd * GLA_DV:C_VG + (hd + 1) * GLA_DV].astype(BF16)
        att = jnp.where(causal, _dot_nt(qb[:, ks_].astype(BF16), kd[:, ks_]), 0.0).astype(BF16)
        o = _dot(att, v)
        s_old = s_ref[:, hd].reshape(NS * GLA_DK, GLA_DV)
        slab = qb[:, (hd // 2) * LANES:(hd // 2 + 1) * LANES]
        dup = jnp.where(lane_half == hd % 2, slab, pltpu.roll(slab, HEAD_DIM, 1))
        q_exp = jnp.where(own_seq, jnp.concatenate([dup] * (NS // 2), axis=1), 0.0).astype(BF16)
        o = o + _dot(q_exp, s_old.astype(BF16))
        o = o * lax.rsqrt(jnp.mean(o * o, axis=-1, keepdims=True) + EPS)
        rg = proj[:, C_RG + hd * GLA_DV:C_RG + (hd + 1) * GLA_DV]
        og.append(o * glag[:, hd * GLA_DV:(hd + 1) * GLA_DV] * _silu(rg))
        kl_exp = jnp.where(own_tok, jnp.concatenate([kl_t[ks_, :]] * NS, axis=0), 0.0).astype(BF16)
        g_exp = jnp.where(own_tok, jnp.concatenate([g_t[ks_, :]] * NS, axis=0), 0.0)
        b_last = _split_dot_rhs(g_exp, ones)
        s_new = jnp.exp(b_last) * s_old + _dot(kl_exp, v)
        so_ref[:, hd] = s_new.reshape(NS, GLA_DK, GLA_DV)

    cos, sa, sb = cos_ref[...], sa_ref[...], sb_ref[...]
    q_rot = _rope(proj[:, C_QS:C_QS + SWA_Q_W], cos, sa, sb) * QK_SCALE
    k_rot = _rope(proj[:, C_KS:C_KS + SWA_KV_W], cos, sa, sb)
    v_new = proj[:, C_VS:C_VS + SWA_KV_W]

    W = kc_ref.shape[1]
    SR = SMP_SUB * Q
    GR = SWA_GROUP * SR
    sub_shift = SR.bit_length() - 1
    w_shift = W.bit_length() - 1
    r_c = lax.broadcasted_iota(jnp.int32, (GR, SMP_SUB * W), 0)
    c_c = lax.broadcasted_iota(jnp.int32, (GR, SMP_SUB * W), 1)
    vis_c = jnp.logical_and(((r_c & (SR - 1)) >> q_shift) == (c_c >> w_shift),
                            (c_c & (W - 1)) > (r_c & (Q - 1)))
    r_n = lax.broadcasted_iota(jnp.int32, (GR, SR), 0)
    c_n = lax.broadcasted_iota(jnp.int32, (GR, SR), 1)
    vis_n = jnp.logical_and(((r_n & (SR - 1)) >> q_shift) == (c_n >> q_shift),
                            (c_n & (Q - 1)) <= (r_n & (Q - 1)))
    grp_row = lax.broadcasted_iota(jnp.int32, (GR, 1), 0) >> sub_shift
    half_sr = lax.broadcasted_iota(jnp.int32, (SR, LANES), 1) >> 6

    o_rows = []
    for sg in range(NS // SMP_SUB):
        rows = slice(sg * SR, (sg + 1) * SR)
        k_c = kc_ref[sg * SMP_SUB:(sg + 1) * SMP_SUB].reshape(SMP_SUB * W, SWA_KV_W).astype(BF16)
        v_c = vc_ref[sg * SMP_SUB:(sg + 1) * SMP_SUB].reshape(SMP_SUB * W, SWA_KV_W).astype(BF16)
        k_n = k_rot[rows, :].astype(BF16)
        v_n = v_new[rows, :].astype(BF16)
        o_kv = []
        for j in range(SWA_KV_HEADS):
            parts = []
            for gq in range(SWA_GROUP):
                hq = j * SWA_GROUP + gq
                slab = q_rot[rows, (hq // 2) * LANES:(hq // 2 + 1) * LANES]
                if hq % 2 != j:
                    slab = pltpu.roll(slab, HEAD_DIM, 1)
                parts.append(jnp.where(half_sr == j, slab, 0.0))
            q8 = jnp.concatenate(parts, axis=0).astype(BF16)
            s_c = jnp.where(vis_c, _dot_nt(q8, k_c), -jnp.inf)
            s_n = jnp.where(vis_n, _dot_nt(q8, k_n), -jnp.inf)
            sink = jnp.zeros((GR, 1), F32)
            for gq in range(SWA_GROUP):
                sink = jnp.where(grp_row == gq, sink_ref[layer, j * SWA_GROUP + gq], sink)
            m = jnp.maximum(jnp.maximum(jnp.max(s_c, axis=-1, keepdims=True),
                                        jnp.max(s_n, axis=-1, keepdims=True)), sink)
            p_c = jnp.exp(s_c - m)
            p_n = jnp.exp(s_n - m)
            den = (jnp.sum(p_c, axis=-1, keepdims=True) + jnp.sum(p_n, axis=-1, keepdims=True)
                   + jnp.exp(sink - m))
            o = _dot((p_c / den).astype(BF16), v_c) + _dot((p_n / den).astype(BF16), v_n)
            o_kv.append(o)
        slabs = []
        for p in range(SWA_HEADS // 2):
            halves = []
            for hq in (2 * p, 2 * p + 1):
                j, gq = hq // SWA_GROUP, hq % SWA_GROUP
                piece = o_kv[j][gq * SR:(gq + 1) * SR, :]
                if hq % 2 != j:
                    piece = pltpu.roll(piece, HEAD_DIM, 1)
                halves.append(piece)
            slabs.append(jnp.where(half_sr == 0, halves[0], halves[1]))
        o_rows.append(jnp.concatenate(slabs, axis=1))
    o_s = _rmsnorm(jnp.concatenate(o_rows, axis=0), swag_ref[...])

    cat = jnp.concatenate(og + [o_s], axis=1).astype(BF16)
    xo_ref[...] = x + _dot(cat, wout_ref[...])

    ko_ref[:, 0:W - Q, :] = kc_ref[:, Q:W, :]
    vo_ref[:, 0:W - Q, :] = vc_ref[:, Q:W, :]
    for s in range(NS):
        ko_ref[s, W - Q:W, :] = k_rot[s * Q:(s + 1) * Q, :]
        vo_ref[s, W - Q:W, :] = v_new[s * Q:(s + 1) * Q, :]


def _split_dot_rhs(g, m_bf16):
    hi = g.astype(BF16)
    lo = (g - hi.astype(F32)).astype(BF16)
    return _dot(hi, m_bf16) + _dot(lo, m_bf16)


def _mix_sample(x2d, state, kc, vc, sinks, mix_w, tabs, layer, prev):
    depth, NB, W = kc.shape[0], kc.shape[1], kc.shape[2]
    Q = x2d.shape[0] // NB
    R = SMP_SEQS * Q
    tab_spec = pl.BlockSpec((R, LANES), lambda i: (0, 0))
    st_spec = pl.BlockSpec((None, SMP_SEQS, GLA_HEADS, GLA_DK, GLA_DV), lambda i: (layer, i, 0, 0, 0))
    kv_spec = pl.BlockSpec((None, SMP_SEQS, W, SWA_KV_W), lambda i: (layer, i, 0, 0))
    n_prev = 0 if prev is None else len(prev)
    return pl.pallas_call(
        functools.partial(_mix_sample_kernel, layer=layer),
        grid=(NB // SMP_SEQS,),
        in_specs=[
            pl.BlockSpec(memory_space=pltpu.SMEM),
            pl.BlockSpec((R, D_MODEL), lambda i: (i, 0)),
            st_spec, kv_spec, kv_spec,
            *_mixer_weight_specs(layer),
            tab_spec, tab_spec, tab_spec,
            *([pl.BlockSpec(memory_space=pl.ANY)] * n_prev),
        ],
        out_specs=[pl.BlockSpec((R, D_MODEL), lambda i: (i, 0)), st_spec, kv_spec, kv_spec],
        out_shape=[
            jax.ShapeDtypeStruct(x2d.shape, F32),
            jax.ShapeDtypeStruct(state.shape, F32),
            jax.ShapeDtypeStruct(kc.shape, F32),
            jax.ShapeDtypeStruct(vc.shape, F32),
        ],
        input_output_aliases={N_SMP_IN + k: 1 + k for k in range(n_prev)},
        compiler_params=pltpu.CompilerParams(
            dimension_semantics=("arbitrary",), vmem_limit_bytes=VMEM_LIMIT),
        name="mix_sample",
    )(sinks, x2d, state, kc, vc, *mix_w, *tabs, *(prev or ()))


def _rope_tables(pos):
    half = ROT_DIM // 2
    inv = jnp.power(jnp.float32(ROPE_THETA), -jnp.arange(half, dtype=jnp.float32) * (2.0 / ROT_DIM))
    ang = pos.astype(jnp.float32)[:, None] * inv[None, :]
    cos, sin = jnp.cos(ang), jnp.sin(ang)
    n = pos.shape[0]
    pad = jnp.zeros((n, HEAD_DIM - ROT_DIM), F32)
    zero = jnp.zeros((n, half), F32)
    c64 = jnp.concatenate([cos, cos, pad + 1.0], axis=1)
    a64 = jnp.concatenate([zero, sin, pad], axis=1)
    b64 = jnp.concatenate([-sin, zero, pad], axis=1)
    rep = LANES // HEAD_DIM
    return tuple(jnp.tile(t, (1, rep)) for t in (c64, a64, b64))


def kernel(x_prompt, x_sample, state_gla, cache_swa_k, cache_swa_v, norm_ffn1, w_ffn1_gu, w_ffn1_down,
           norm_mix, w_in, w_gate_up, b_gate, gla_norm, attn_sinks, swa_norm, w_out, norm_ffn2,
           w_ffn2_gu, w_ffn2_down, norm_final):
    B, L, _ = x_prompt.shape
    NB, Q, _ = x_sample.shape
    depth = w_in.shape[0]
    W = cache_swa_k.shape[2]

    tabs_p = _rope_tables(jnp.arange(L))
    tabs_s = _rope_tables(jnp.tile(PAST_LEN + jnp.arange(Q), SMP_SEQS))

    wa2 = jnp.pad(w_gate_up, ((0, 0), (0, LANES - GATE_RANK), (0, 0))).astype(BF16)
    row = lambda p: p.reshape(depth, 1, -1)
    n1, n2 = row(norm_ffn1), row(norm_ffn2)
    gu_rows = D_MODEL // (B * L // FFN_TILE)
    dn_rows = D_FF // 11

    def ffn_sides(w_gu, w_down, layer):
        return (SideCast(w_gu, layer, gu_rows), SideCast(w_down, layer, dn_rows))

    xp = x_prompt.reshape(B * L, D_MODEL)
    xs = x_sample.reshape(NB * Q, D_MODEL)
    kc = cache_swa_k.reshape(depth, NB, W, SWA_KV_W)
    vc = cache_swa_v.reshape(depth, NB, W, SWA_KV_W)
    out_p = out_s = None
    gu1, dn1 = _cast_layers((SideCast(w_ffn1_gu, 0, D_MODEL // 8), SideCast(w_ffn1_down, 0, D_FF // 4)))
    for l in range(depth):
        nf = norm_final if l == depth - 1 else None
        xp, win, wout = _ffn(xp, n1, gu1, dn1, l, sides=(SideCast(w_in, l, gu_rows, True),
                                                        SideCast(w_out, l, gu_rows)))
        xs = _ffn(xs, n1, gu1, dn1, l)
        mix_w = (row(norm_mix), win, wa2, row(b_gate), row(gla_norm), row(swa_norm), wout)
        xp3, *out_p = _mix_prompt(xp.reshape(B, L, D_MODEL), attn_sinks, mix_w, tabs_p, l, depth, out_p,
                                  sides=ffn_sides(w_ffn2_gu, w_ffn2_down, l))
        gu2, dn2 = out_p[3:]
        out_p = out_p[:3]
        xs, *out_s = _mix_sample(xs, state_gla, kc, vc, attn_sinks, mix_w, tabs_s, l, out_s)
        xs = _ffn(xs, n2, gu2, dn2, l, nf)
        xp3 = xp3.reshape(B * L, D_MODEL)
        if l + 1 < depth:
            xp, gu1, dn1 = _ffn(xp3, n2, gu2, dn2, l, nf, sides=ffn_sides(w_ffn1_gu, w_ffn1_down, l + 1))
        else:
            xp = _ffn(xp3, n2, gu2, dn2, l, nf)

    gla_p, k_p, v_p = out_p
    gla_s, k_s, v_s = out_s
    kv5 = lambda t: t.reshape(*t.shape[:3], SWA_KV_HEADS, HEAD_DIM)
    return (xp.reshape(B, L, D_MODEL), xs.reshape(NB, Q, D_MODEL), gla_p, kv5(k_p), kv5(v_p),
            gla_s, kv5(k_s), kv5(v_s))
```

```python
import functools
from typing import NamedTuple

import jax
import jax.numpy as jnp
from jax import lax
from jax.experimental import pallas as pl
from jax.experimental.pallas import tpu as pltpu

F32 = jnp.float32
BF16 = jnp.bfloat16

D_MODEL = 1024
D_FF = 2816
EPS = 1e-6
GLA_HEADS = 4
GLA_DK = 64
GLA_DV = 128
GLA_QK_W = GLA_HEADS * GLA_DK
GLA_V_W = GLA_HEADS * GLA_DV
GATE_RANK = 16
GATE_NORM = 16.0
GLA_CHUNK = 64
HEAD_DIM = 64
SWA_HEADS = 8
SWA_KV_HEADS = 2
SWA_GROUP = SWA_HEADS // SWA_KV_HEADS
SWA_Q_W = SWA_HEADS * HEAD_DIM
SWA_KV_W = SWA_KV_HEADS * HEAD_DIM
WINDOW = 128
ROT_DIM = 16
ROPE_THETA = 500000.0
PAST_LEN = 16384
QK_SCALE = 0.125

LANES = 128
C_QG, C_KG, C_VG, C_RG = 0, 256, 512, 1024
C_QS, C_KS, C_VS, C_A = 1536, 2048, 2176, 2304
IN_W_PAD = C_A + LANES

FFN_TILE = 1024
FFN_CHUNK = 256
MIX_TILE = 512
SMP_SEQS = 32
SMP_SUB = 8
VMEM_LIMIT = 56 * 1024 * 1024

NT_DIMS = (((1,), (1,)), ((), ()))


def _dot(a, b):
    return jnp.dot(a, b, preferred_element_type=F32)


def _dot_nt(a, b):
    return lax.dot_general(a, b, NT_DIMS, preferred_element_type=F32)


def _rmsnorm(x, g):
    return x * lax.rsqrt(jnp.mean(x * x, axis=-1, keepdims=True) + EPS) * g


def _log_sigmoid(z):
    return jnp.minimum(z, 0.0) - jnp.log1p(jnp.exp(-jnp.abs(z)))


def _silu(z):
    return z * jax.nn.sigmoid(z)


def _split_dot(m_bf16, g):
    hi = g.astype(BF16)
    lo = (g - hi.astype(F32)).astype(BF16)
    return _dot(m_bf16, hi) + _dot(m_bf16, lo)


def _rope(x, cos, sa, sb):
    half = ROT_DIM // 2
    outs = []
    for p in range(x.shape[1] // LANES):
        s = x[:, p * LANES:(p + 1) * LANES]
        outs.append(s * cos + pltpu.roll(s, half, 1) * sa + pltpu.roll(s, LANES - half, 1) * sb)
    return outs[0] if len(outs) == 1 else jnp.concatenate(outs, axis=1)


class SideCast(NamedTuple):
    src: jax.Array
    layer: int
    block_rows: int
    reorder_w_in: bool = False

    @property
    def n_blocks(self):
        return self.src.shape[1] // self.block_rows

    @property
    def out_cols(self):
        return IN_W_PAD if self.reorder_w_in else self.src.shape[2]


def _cast_block(src_ref, dst_ref, reorder_w_in):
    w = src_ref[...]
    if not reorder_w_in:
        dst_ref[...] = w.astype(BF16)
        return
    n_a = C_RG + GLA_V_W
    dst_ref[:, 0:n_a] = w[:, 0:n_a].astype(BF16)
    dst_ref[:, n_a:C_A] = w[:, n_a + GATE_RANK:].astype(BF16)
    pad = jnp.zeros((w.shape[0], LANES - GATE_RANK), F32)
    dst_ref[:, C_A:IN_W_PAD] = jnp.concatenate([w[:, n_a:n_a + GATE_RANK], pad], axis=1).astype(BF16)


def _side_cast_specs(sides):
    in_specs, out_specs, out_shapes = [], [], []
    for s in sides:
        last = s.n_blocks - 1
        in_specs.append(pl.BlockSpec((None, s.block_rows, s.src.shape[2]),
                                     lambda i, layer=s.layer, last=last: (layer, jnp.minimum(i, last), 0)))
        out_specs.append(pl.BlockSpec((s.block_rows, s.out_cols), lambda i, last=last: (jnp.minimum(i, last), 0)))
        out_shapes.append(jax.ShapeDtypeStruct((s.src.shape[1], s.out_cols), BF16))
    return in_specs, out_specs, out_shapes


def _run_side_casts(step, sides, src_refs, dst_refs):
    for (n_blocks, reorder), src, dst in zip(sides, src_refs, dst_refs):
        @pl.when(step < n_blocks)
        def _():
            _cast_block(src, dst, reorder)


def _cast_kernel(*refs, sides):
    n = len(sides)
    _run_side_casts(pl.program_id(0), sides, refs[:n], refs[n:])


def _cast_layers(sides):
    in_specs, out_specs, out_shapes = _side_cast_specs(sides)
    return pl.pallas_call(
        functools.partial(_cast_kernel, sides=tuple((s.n_blocks, s.reorder_w_in) for s in sides)),
        grid=(max(s.n_blocks for s in sides),),
        in_specs=in_specs,
        out_specs=out_specs,
        out_shape=out_shapes,
        compiler_params=pltpu.CompilerParams(
            dimension_semantics=("arbitrary",), vmem_limit_bytes=VMEM_LIMIT),
        name="cast_weights",
    )(*(s.src for s in sides))


def _ffn_kernel(*refs, final, sides):
    n_in = 5 if final else 4
    n_side = len(sides)
    x_ref, n_ref, wgu_ref, wd_ref = refs[:4]
    nf_ref = refs[4] if final else None
    o_ref = refs[n_in + n_side]
    acc_ref = refs[-1]
    _run_side_casts(pl.program_id(0), sides, refs[n_in:n_in + n_side],
                    refs[n_in + n_side + 1:n_in + 2 * n_side + 1])
    x = x_ref[...]
    h = _rmsnorm(x, n_ref[...]).astype(BF16)
    for c in range(D_FF // FFN_CHUNK):
        lo = c * FFN_CHUNK
        gate = _dot(h, wgu_ref[:, lo:lo + FFN_CHUNK])
        up = _dot(h, wgu_ref[:, D_FF + lo:D_FF + lo + FFN_CHUNK])
        act = (_silu(gate) * up).astype(BF16)
        part = _dot(act, wd_ref[lo:lo + FFN_CHUNK, :])
        if c == 0:
            acc_ref[...] = part
        else:
            acc_ref[...] += part
    y = x + 0.5 * acc_ref[...]
    if final:
        y = _rmsnorm(y, nf_ref[...])
    o_ref[...] = y


def _ffn_tile(n_rows):
    return min(FFN_TILE, n_rows)


def _ffn(x2d, norm, w_gu, w_down, layer, norm_final=None, sides=()):
    final = norm_final is not None
    const = lambda i: (0, 0)
    rows = _ffn_tile(x2d.shape[0])
    tile = pl.BlockSpec((rows, D_MODEL), lambda i: (i, 0))
    in_specs = [
        tile,
        pl.BlockSpec((None, 1, D_MODEL), lambda i: (layer, 0, 0)),
        pl.BlockSpec((D_MODEL, 2 * D_FF), const, pipeline_mode=pl.Buffered(1)),
        pl.BlockSpec((D_FF, D_MODEL), const, pipeline_mode=pl.Buffered(1)),
    ]
    args = [x2d, norm, w_gu, w_down]
    if final:
        in_specs.append(pl.BlockSpec((1, D_MODEL), const))
        args.append(norm_final.reshape(1, D_MODEL))
    side_in, side_out, side_shapes = _side_cast_specs(sides)
    outs = pl.pallas_call(
        functools.partial(_ffn_kernel, final=final,
                          sides=tuple((s.n_blocks, s.reorder_w_in) for s in sides)),
        grid=(x2d.shape[0] // rows,),
        in_specs=in_specs + side_in,
        out_specs=[tile] + side_out,
        out_shape=[jax.ShapeDtypeStruct(x2d.shape, F32)] + side_shapes,
        scratch_shapes=[pltpu.VMEM((rows, D_MODEL), F32)],
        compiler_params=pltpu.CompilerParams(
            dimension_semantics=("arbitrary",), vmem_limit_bytes=VMEM_LIMIT),
        name="ffn_final" if final else "ffn",
    )(*args, *(s.src for s in sides))
    return outs if sides else outs[0]


N_MIX_IN = 12


def _mix_project(x_ref, nmix_ref, win_ref, wa2_ref, ba_ref, glag_ref, cos_ref, sa_ref, sb_ref,
                 qb_s, qsw_s, kd_s, klt_s, dect_s, v_s, gate_s, qh_s, kt_s, vt_s, ko_ref, vo_ref):
    T = MIX_TILE
    G = 256

    x = x_ref[...]
    h = _rmsnorm(x, nmix_ref[...]).astype(BF16)

    a_low = _dot(h, win_ref[:, C_A:C_A + LANES]).astype(BF16)
    z = _dot(a_low, wa2_ref[...]) + ba_ref[...]
    g = _log_sigmoid(z) * (1.0 / GATE_NORM)
    qk = _dot(h, win_ref[:, C_QG:C_QG + 2 * GLA_QK_W])

    ri = lax.broadcasted_iota(jnp.int32, (G, G), 0)
    ci = lax.broadcasted_iota(jnp.int32, (G, G), 1)
    same_chunk = (ri >> 6) == (ci >> 6)
    incl = jnp.where(jnp.logical_and(same_chunk, ci <= ri), 1.0, 0.0).astype(BF16)
    C = GLA_CHUNK
    chunk_last = []
    for r in range(T // G):
        rows = slice(r * G, (r + 1) * G)
        b = _split_dot(incl, g[rows])
        lasts = [b[c * C + C - 1:(c + 1) * C, :] for c in range(G // C)]
        chunk_last.extend(lasts)
        rest = jnp.concatenate([jnp.broadcast_to(t, (C, GLA_QK_W)) for t in lasts], axis=0) - b
        q = qk[rows, 0:GLA_QK_W]
        k = qk[rows, GLA_QK_W:2 * GLA_QK_W]
        qb = q * QK_SCALE * jnp.exp(b)
        qb_s[rows, :] = qb.astype(BF16)
        qsw_s[rows, :] = jnp.concatenate(
            [pltpu.roll(qb[:, p * LANES:(p + 1) * LANES], GLA_DK, 1) for p in range(GLA_QK_W // LANES)],
            axis=1)
        kd_s[rows, :] = k * jnp.exp(-b)
        klt_s[:, rows] = (k * jnp.exp(rest)).T.astype(BF16)
    pad = jnp.zeros((LANES - len(chunk_last), GLA_QK_W), F32)
    dect_s[...] = jnp.exp(jnp.concatenate(chunk_last + [pad], axis=0)).T

    v_s[...] = _dot(h, win_ref[:, C_VG:C_VG + GLA_V_W]).astype(BF16)
    gate_s[...] = _silu(_dot(h, win_ref[:, C_RG:C_RG + GLA_V_W])) * glag_ref[...]

    cos, sa, sb = cos_ref[...], sa_ref[...], sb_ref[...]
    half_t = lax.broadcasted_iota(jnp.int32, (T, LANES), 1) >> 6
    q_rot = _rope(_dot(h, win_ref[:, C_QS:C_QS + SWA_Q_W]), cos, sa, sb) * QK_SCALE
    for hq in range(SWA_HEADS):
        slab = q_rot[:, (hq // 2) * LANES:(hq // 2 + 1) * LANES]
        j = hq // SWA_GROUP
        if hq % 2 != j:
            slab = pltpu.roll(slab, HEAD_DIM, 1)
        qh_s[hq] = jnp.where(half_t == j, slab, 0.0).astype(BF16)
    kv =_dot(h, win_ref[:, C_KS:C_KS + 2 * SWA_KV_W])
    k_rot = _rope(kv[:, 0:SWA_KV_W], cos, sa, sb)
    v_swa = kv[:, SWA_KV_W:2 * SWA_KV_W]
    kt_s[...] = k_rot.astype(BF16)
    vt_s[...] = v_swa.T.astype(BF16)
    ko_ref[...] = k_rot[T - WINDOW:T, :]
    vo_ref[...] = v_swa[T - WINDOW:T, :]


def _mix_sequential(first, layer, sink_ref, x_ref, swag_ref, wout_ref, xo_ref,
                    qb_s, qsw_s, kd_s, klt_s, dect_s, v_s, gate_s, qh_s, kt_s, vt_s,
                    s_s, og_s, cat_s, kcar_s, vcar_s, lhs_s, upd_s, st16_s, sc_s, pr_s, rden_s):
    T = MIX_TILE
    C = GLA_CHUNK

    hk_r = lax.broadcasted_iota(jnp.int32, (GLA_QK_W, GLA_QK_W), 0) >> 6
    hk_c = lax.broadcasted_iota(jnp.int32, (GLA_QK_W, GLA_QK_W), 1) >> 6
    bd_k = hk_r == hk_c
    qi = lax.broadcasted_iota(jnp.int32, (C, GLA_QK_W), 0)
    kj = lax.broadcasted_iota(jnp.int32, (C, GLA_QK_W), 1) & (C - 1)
    causal = kj <= qi
    half_c = lax.broadcasted_iota(jnp.int32, (C, LANES), 1) >> 6
    zeros_v = jnp.zeros((C, GLA_DV), BF16)
    n_c = T // C
    for c in range(n_c):
        rows = slice(c * C, (c + 1) * C)
        kd_bd = jnp.where(bd_k, jnp.concatenate([kd_s[rows, :]] * GLA_HEADS, axis=0), 0.0).astype(BF16)
        att = jnp.where(causal, _dot_nt(qb_s[rows, :], kd_bd), 0.0)
        tok = slice((c // 2) * 2 * C, (c // 2 + 1) * 2 * C)
        for hd in range(GLA_HEADS):
            slab = slice((hd // 2) * LANES, (hd // 2 + 1) * LANES)
            hk = slice(hd * GLA_DK, (hd + 1) * GLA_DK)
            v = v_s[rows, hd * GLA_DV:(hd + 1) * GLA_DV]
            lhs_s[c, hd] = jnp.where(half_c == hd % 2, att[:, slab], qsw_s[rows, slab]).astype(BF16)
            v_pair = jnp.concatenate([v, zeros_v] if c % 2 == 0 else [zeros_v, v], axis=0)
            upd_s[c, hk, :] = _dot(klt_s[hk, tok], v_pair)
    state = s_s[...]
    for c in range(n_c):
        st16_s[c] = state.astype(BF16)
        state = dect_s[:, c:c + 1] * state + upd_s[c]
    s_s[...] = state
    for c in range(n_c):
        rows = slice(c * C, (c + 1) * C)
        for hd in range(GLA_HEADS):
            hk = slice(hd * GLA_DK, (hd + 1) * GLA_DK)
            v = v_s[rows, hd * GLA_DV:(hd + 1) * GLA_DV]
            st = st16_s[c, hk, :]
            rhs = jnp.concatenate([v, st] if hd % 2 == 0 else [st, v], axis=0)
            og_s[rows, hd * GLA_DV:(hd + 1) * GLA_DV] = _dot(lhs_s[c, hd], rhs)

    for hd in range(GLA_HEADS):
        vs_ = slice(hd * GLA_DV, (hd + 1) * GLA_DV)
        o = og_s[:, vs_]
        o = o * lax.rsqrt(jnp.mean(o * o, axis=-1, keepdims=True) + EPS)
        cat_s[:, vs_] = (o * gate_s[:, vs_]).astype(BF16)

    swag = swag_ref[...]
    GR = SWA_GROUP * WINDOW
    kc_ = lax.broadcasted_iota(jnp.int32, (WINDOW, GR), 0)
    qr = lax.broadcasted_iota(jnp.int32, (WINDOW, GR), 1) & (WINDOW - 1)
    from_prev = kc_ > qr
    prev_bias = jnp.where(first > 0.0, -jnp.inf, 0.0)
    n_b = T // WINDOW
    sinks = [jnp.concatenate(
        [jnp.full((1, WINDOW), sink_ref[layer, j * SWA_GROUP + gq], F32) for gq in range(SWA_GROUP)], axis=1)
        for j in range(SWA_KV_HEADS)]
    for i in range(n_b):
        rows = slice(i * WINDOW, (i + 1) * WINDOW)
        if i == 0:
            k_blk = jnp.concatenate([kcar_s[...], kt_s[0:WINDOW, :]], axis=0)
        else:
            k_blk = kt_s[(i - 1) * WINDOW:(i + 1) * WINDOW, :]
        for j in range(SWA_KV_HEADS):
            q4 = jnp.concatenate([qh_s[j * SWA_GROUP + gq, rows, :] for gq in range(SWA_GROUP)], axis=0)
            s2 = _dot_nt(k_blk, q4)
            s_prev = s2[0:WINDOW, :] + prev_bias if i == 0 else s2[0:WINDOW, :]
            sc_s[i, j] = jnp.where(from_prev, s_prev, s2[WINDOW:2 * WINDOW, :])
    for i in range(n_b):
        for j in range(SWA_KV_HEADS):
            s = sc_s[i, j]
            m = jnp.maximum(jnp.max(s, axis=0, keepdims=True), sinks[j])
            pr = jnp.exp(s - m)
            rden_s[i, j] = 1.0 / (jnp.sum(pr, axis=0, keepdims=True) + jnp.exp(sinks[j] - m))
            pr_s[i, j, 0:WINDOW, :] = jnp.where(from_prev, pr, 0.0).astype(BF16)
            pr_s[i, j, WINDOW:2 * WINDOW, :] = jnp.where(from_prev, 0.0, pr).astype(BF16)
    for i in range(n_b):
        rows = slice(i * WINDOW, (i + 1) * WINDOW)
        if i == 0:
            vt_blk = jnp.concatenate([vcar_s[...], vt_s[:, 0:WINDOW]], axis=1)
        else:
            vt_blk = vt_s[:, (i - 1) * WINDOW:(i + 1) * WINDOW]
        pieces = []
        for j in range(SWA_KV_HEADS):
            vt_j = vt_blk[j * HEAD_DIM:(j + 1) * HEAD_DIM, :]
            o_t = _dot(vt_j, pr_s[i, j]) * rden_s[i, j]
            pieces.extend(o_t[:, gq * WINDOW:(gq + 1) * WINDOW] for gq in range(SWA_GROUP))
        o_s = jnp.concatenate(pieces, axis=0).T
        cat_s[rows, GLA_V_W:GLA_V_W + SWA_Q_W] = _rmsnorm(o_s, swag).astype(BF16)

    xo_ref[...] =(x_ref[...] + _dot(cat_s[:, 0:GLA_V_W], wout_ref[0:GLA_V_W, :])
                   + _dot(cat_s[:, GLA_V_W:D_MODEL], wout_ref[GLA_V_W:D_MODEL, :]))

    kcar_s[...] = kt_s[T - WINDOW:T, :]
    vcar_s[...] = vt_s[:, T - WINDOW:T]


N_MIX_OUT = 4
N_MIX_SCRATCH = 21


def _mix_prompt_kernel(*refs, layer, n_l, n_prev, sides):
    (sink_ref, x_ref, nmix_ref, win_ref, wa2_ref, ba_ref, glag_ref, swag_ref,
     wout_ref, cos_ref, sa_ref, sb_ref) = refs[:N_MIX_IN]
    n_side = len(sides)
    side_src = refs[N_MIX_IN + n_prev:N_MIX_IN + n_prev + n_side]
    outs = refs[N_MIX_IN + n_prev + n_side:len(refs) - N_MIX_SCRATCH]
    xo_ref, so_ref, ko_ref, vo_ref = outs[:N_MIX_OUT]
    (qb_s, qsw_s, kd_s, klt_s, dect_s, v_s, gate_s, qh_s, kt_s, vt_s,
     s_s, og_s, cat_s, kcar_s, vcar_s, lhs_s, upd_s, st16_s, sc_s, pr_s, rden_s) = refs[-N_MIX_SCRATCH:]
    handover = (qb_s, qsw_s, kd_s, klt_s, dect_s, v_s, gate_s, qh_s, kt_s, vt_s)
    t = pl.program_id(0)
    l_seq = t % n_l
    _run_side_casts(t, sides, side_src, outs[N_MIX_OUT:])

    @pl.when(l_seq == 0)
    def _():
        s_s[...] = jnp.zeros_like(s_s)
        kcar_s[...] = jnp.zeros_like(kcar_s)
        vcar_s[...] = jnp.zeros_like(vcar_s)

    first = jnp.where(l_seq == 0, 1.0, 0.0)
    _mix_project(x_ref, nmix_ref, win_ref, wa2_ref, ba_ref, glag_ref, cos_ref, sa_ref, sb_ref,
                 *handover, ko_ref, vo_ref)
    _mix_sequential(first, layer, sink_ref, x_ref, swag_ref, wout_ref, xo_ref,
                    *handover, s_s, og_s, cat_s, kcar_s, vcar_s, lhs_s, upd_s, st16_s, sc_s, pr_s, rden_s)

    @pl.when(l_seq == n_l - 1)
    def _():
        for hd in range(GLA_HEADS):
            so_ref[hd] = s_s[hd * GLA_DK:(hd + 1) * GLA_DK, :]


def _mixer_weight_specs(layer):
    at_layer = lambda i: (layer, 0, 0)
    const = lambda i: (0, 0)
    return [
        pl.BlockSpec((None, 1, D_MODEL), at_layer),
        pl.BlockSpec((D_MODEL, IN_W_PAD), const),
        pl.BlockSpec((None, LANES, GLA_QK_W), at_layer),
        pl.BlockSpec((None, 1, GLA_QK_W), at_layer),
        pl.BlockSpec((None, 1, GLA_V_W), at_layer),
        pl.BlockSpec((None, 1, SWA_Q_W), at_layer),
        pl.BlockSpec((D_MODEL, D_MODEL), const),
    ]


def _mix_prompt(x, sinks, mix_w, tabs, layer, depth, prev, sides=()):
    B, L, _ = x.shape
    T = MIX_TILE
    n_l = L // T
    x_spec = pl.BlockSpec((None, T, D_MODEL), lambda t: (t // n_l, t % n_l, 0))
    tab_spec = pl.BlockSpec((T, LANES), lambda t: (t % n_l, 0))
    n_prev = 0 if prev is None else len(prev)
    side_in, side_out, side_shapes = _side_cast_specs(sides)
    return pl.pallas_call(
        functools.partial(_mix_prompt_kernel, layer=layer, n_l=n_l, n_prev=n_prev,
                          sides=tuple((s.n_blocks, s.reorder_w_in) for s in sides)),
        grid=(B * n_l,),
        in_specs=[
            pl.BlockSpec(memory_space=pltpu.SMEM),
            x_spec,
            *_mixer_weight_specs(layer),
            tab_spec, tab_spec, tab_spec,
            *([pl.BlockSpec(memory_space=pl.ANY)] * n_prev),
            *side_in,
        ],
        out_specs=[
            x_spec,
            pl.BlockSpec((None, None, GLA_HEADS, GLA_DK, GLA_DV), lambda t: (layer, t // n_l, 0, 0, 0)),
            pl.BlockSpec((None, None, WINDOW, SWA_KV_W), lambda t: (layer, t // n_l, 0, 0)),
            pl.BlockSpec((None, None, WINDOW, SWA_KV_W), lambda t: (layer, t // n_l, 0, 0)),
            *side_out,
        ],
        out_shape=[
            jax.ShapeDtypeStruct((B, L, D_MODEL), F32),
            jax.ShapeDtypeStruct((depth, B, GLA_HEADS, GLA_DK, GLA_DV), F32),
            jax.ShapeDtypeStruct((depth, B, WINDOW, SWA_KV_W), F32),
            jax.ShapeDtypeStruct((depth, B, WINDOW, SWA_KV_W), F32),
            *side_shapes,
        ],
        input_output_aliases={N_MIX_IN + k: 1 + k for k in range(n_prev)},
        scratch_shapes=[
            pltpu.VMEM((T, GLA_QK_W), BF16),
            pltpu.VMEM((T, GLA_QK_W), F32),
            pltpu.VMEM((T, GLA_QK_W), F32),
            pltpu.VMEM((GLA_QK_W, T), BF16),
            pltpu.VMEM((GLA_QK_W, LANES), F32),
            pltpu.VMEM((T, GLA_V_W), BF16),
            pltpu.VMEM((T, GLA_V_W), F32),
            pltpu.VMEM((SWA_HEADS, T, LANES), BF16),
            pltpu.VMEM((T, SWA_KV_W), BF16),
            pltpu.VMEM((SWA_KV_W, T), BF16),
            pltpu.VMEM((GLA_QK_W, GLA_DV), F32),
            pltpu.VMEM((T, GLA_V_W), F32),
            pltpu.VMEM((T, D_MODEL), BF16),
            pltpu.VMEM((WINDOW, SWA_KV_W), BF16),
            pltpu.VMEM((SWA_KV_W, WINDOW), BF16),
            pltpu.VMEM((T // GLA_CHUNK, GLA_HEADS, GLA_CHUNK, LANES), BF16),
            pltpu.VMEM((T // GLA_CHUNK, GLA_QK_W, GLA_DV), F32),
            pltpu.VMEM((T // GLA_CHUNK, GLA_QK_W, GLA_DV), BF16),
            pltpu.VMEM((T // WINDOW, SWA_KV_HEADS, WINDOW, SWA_GROUP * WINDOW), F32),
            pltpu.VMEM((T // WINDOW, SWA_KV_HEADS, 2 * WINDOW, SWA_GROUP * WINDOW), BF16),
            pltpu.VMEM((T // WINDOW, SWA_KV_HEADS, 1, SWA_GROUP * WINDOW), F32),
        ],
        compiler_params=pltpu.CompilerParams(
            dimension_semantics=("arbitrary",), vmem_limit_bytes=VMEM_LIMIT),
        name="mix_prompt",
    )(sinks, x, *mix_w, *tabs, *(prev or ()), *(s.src for s in sides))


N_SMP_IN = 15


def _mix_sample_kernel(*refs, layer):
    (sink_ref, x_ref, s_ref, kc_ref, vc_ref, nmix_ref, win_ref, wa2_ref, ba_ref,
     glag_ref, swag_ref, wout_ref, cos_ref, sa_ref, sb_ref) = refs[:N_SMP_IN]
    xo_ref, so_ref, ko_ref, vo_ref = refs[-4:]
    R = x_ref.shape[0]
    NS = SMP_SEQS
    Q = R // NS
    q_shift = Q.bit_length() - 1

    x = x_ref[...]
    h = _rmsnorm(x, nmix_ref[...]).astype(BF16)
    proj = _dot(h, win_ref[...])
    z = _dot(proj[:, C_A:C_A + LANES].astype(BF16), wa2_ref[...]) + ba_ref[...]
    g = _log_sigmoid(z) * (1.0 / GATE_NORM)

    ri = lax.broadcasted_iota(jnp.int32, (R, R), 0)
    ci = lax.broadcasted_iota(jnp.int32, (R, R), 1)
    same = (ri >> q_shift) == (ci >> q_shift)
    causal = jnp.logical_and(same, ci <= ri)
    after = jnp.logical_and(same, ci > ri)
    b = _split_dot(jnp.where(causal, 1.0, 0.0).astype(BF16), g)
    rest = _split_dot(jnp.where(after, 1.0, 0.0).astype(BF16), g)

    qg = proj[:, C_QG:C_QG + GLA_QK_W]
    kg = proj[:, C_KG:C_KG + GLA_QK_W]
    qb = qg * QK_SCALE * jnp.exp(b)
    kd = (kg * jnp.exp(-b)).astype(BF16)
    kl_t = (kg * jnp.exp(rest)).T
    g_t = g.T

    own_tok = ((lax.broadcasted_iota(jnp.int32, (NS * GLA_DK, R), 0) >> 6)
               == (lax.broadcasted_iota(jnp.int32, (NS * GLA_DK, R), 1) >> q_shift))
    own_seq = ((lax.broadcasted_iota(jnp.int32, (R, NS * GLA_DK), 1) >> 6)
               == (lax.broadcasted_iota(jnp.int32, (R, NS * GLA_DK), 0) >> q_shift))
    lane_half = lax.broadcasted_iota(jnp.int32, (R, LANES), 1) >> 6
    ones = jnp.ones((R, GLA_DV), BF16)

    glag = glag_ref[...]
    og = []
    for hd in range(GLA_HEADS):
        ks_ = slice(hd * GLA_DK, (hd + 1) * GLA_DK)
        v = proj[:, C_VG + hd * GLA_DV:C_VG + (hd + 1) * GLA_DV].astype(BF16)
        att = jnp.where(causal, _dot_nt(qb[:, ks_].astype(BF16), kd[:, ks_]), 0.0).astype(BF16)
        o = _dot(att, v)
        s_old = s_ref[:, hd].reshape(NS * GLA_DK, GLA_DV)
        slab = qb[:, (hd // 2) * LANES:(hd // 2 + 1) * LANES]
        dup = jnp.where(lane_half == hd % 2, slab, pltpu.roll(slab, HEAD_DIM, 1))
        q_exp = jnp.where(own_seq, jnp.concatenate([dup] * (NS // 2), axis=1), 0.0).astype(BF16)
        o = o + _dot(q_exp, s_old.astype(BF16))
        o = o * lax.rsqrt(jnp.mean(o * o, axis=-1, keepdims=True) + EPS)
        rg = proj[:, C_RG + hd * GLA_DV:C_RG + (hd + 1) * GLA_DV]
        og.append(o * glag[:, hd * GLA_DV:(hd + 1) * GLA_DV] * _silu(rg))
        kl_exp = jnp.where(own_tok, jnp.concatenate([kl_t[ks_, :]] * NS, axis=0), 0.0).astype(BF16)
        g_exp = jnp.where(own_tok, jnp.concatenate([g_t[ks_, :]] * NS, axis=0), 0.0)
        b_last = _split_dot_rhs(g_exp, ones)
        s_new = jnp.exp(b_last) * s_old + _dot(kl_exp, v)
        so_ref[:, hd] = s_new.reshape(NS, GLA_DK, GLA_DV)

    cos, sa, sb = cos_ref[...], sa_ref[...], sb_ref[...]
    q_rot = _rope(proj[:, C_QS:C_QS + SWA_Q_W], cos, sa, sb) * QK_SCALE
    k_rot = _rope(proj[:, C_KS:C_KS + SWA_KV_W], cos, sa, sb)
    v_new = proj[:, C_VS:C_VS + SWA_KV_W]

    W = kc_ref.shape[1]
    SR = SMP_SUB * Q
    GR = SWA_GROUP * SR
    sub_shift = SR.bit_length() - 1
    w_shift = W.bit_length() - 1
    r_c = lax.broadcasted_iota(jnp.int32, (GR, SMP_SUB * W), 0)
    c_c = lax.broadcasted_iota(jnp.int32, (GR, SMP_SUB * W), 1)
    vis_c = jnp.logical_and(((r_c & (SR - 1)) >> q_shift) == (c_c >> w_shift),
                            (c_c & (W - 1)) > (r_c & (Q - 1)))
    r_n = lax.broadcasted_iota(jnp.int32, (GR, SR), 0)
    c_n = lax.broadcasted_iota(jnp.int32, (GR, SR), 1)
    vis_n = jnp.logical_and(((r_n & (SR - 1)) >> q_shift) == (c_n >> q_shift),
                            (c_n & (Q - 1)) <= (r_n & (Q - 1)))
    grp_row = lax.broadcasted_iota(jnp.int32, (GR, 1), 0) >> sub_shift
    half_sr = lax.broadcasted_iota(jnp.int32, (SR, LANES), 1) >> 6

    o_rows = []
    for sg in range(NS // SMP_SUB):
        rows = slice(sg * SR, (sg + 1) * SR)
        k_c = kc_ref[sg * SMP_SUB:(sg + 1) * SMP_SUB].reshape(SMP_SUB * W, SWA_KV_W).astype(BF16)
        v_c = vc_ref[sg * SMP_SUB:(sg + 1) * SMP_SUB].reshape(SMP_SUB * W, SWA_KV_W).astype(BF16)
        k_n = k_rot[rows, :].astype(BF16)
        v_n = v_new[rows, :].astype(BF16)
        o_kv = []
        for j in range(SWA_KV_HEADS):
            parts = []
            for gq in range(SWA_GROUP):
                hq = j * SWA_GROUP + gq
                slab = q_rot[rows, (hq // 2) * LANES:(hq // 2 + 1) * LANES]
                if hq % 2 != j:
                    slab = pltpu.roll(slab, HEAD_DIM, 1)
                parts.append(jnp.where(half_sr == j, slab, 0.0))
            q8 = jnp.concatenate(parts, axis=0).astype(BF16)
            s_c = jnp.where(vis_c, _dot_nt(q8, k_c), -jnp.inf)
            s_n = jnp.where(vis_n, _dot_nt(q8, k_n), -jnp.inf)
            sink = jnp.zeros((GR, 1), F32)
            for gq in range(SWA_GROUP):
                sink = jnp.where(grp_row == gq, sink_ref[layer, j * SWA_GROUP + gq], sink)
            m = jnp.maximum(jnp.maximum(jnp.max(s_c, axis=-1, keepdims=True),
                                        jnp.max(s_n, axis=-1, keepdims=True)), sink)
            p_c = jnp.exp(s_c - m)
            p_n = jnp.exp(s_n - m)
            den = (jnp.sum(p_c, axis=-1, keepdims=True) + jnp.sum(p_n, axis=-1, keepdims=True)
                   + jnp.exp(sink - m))
            o = _dot((p_c / den).astype(BF16), v_c) + _dot((p_n / den).astype(BF16), v_n)
            o_kv.append(o)
        slabs = []
        for p in range(SWA_HEADS // 2):
            halves = []
            for hq in (2 * p, 2 * p + 1):
                j, gq = hq // SWA_GROUP, hq % SWA_GROUP
                piece = o_kv[j][gq * SR:(gq + 1) * SR, :]
                if hq % 2 != j:
                    piece = pltpu.roll(piece, HEAD_DIM, 1)
                halves.append(piece)
            slabs.append(jnp.where(half_sr == 0, halves[0], halves[1]))
        o_rows.append(jnp.concatenate(slabs, axis=1))
    o_s = _rmsnorm(jnp.concatenate(o_rows, axis=0), swag_ref[...])

    cat = jnp.concatenate(og + [o_s], axis=1).astype(BF16)
    xo_ref[...] = x + _dot(cat, wout_ref[...])

    ko_ref[:, 0:W - Q, :] = kc_ref[:, Q:W, :]
    vo_ref[:, 0:W - Q, :] = vc_ref[:, Q:W, :]
    for s in range(NS):
        ko_ref[s, W - Q:W, :] = k_rot[s * Q:(s + 1) * Q, :]
        vo_ref[s, W - Q:W, :] = v_new[s * Q:(s + 1) * Q, :]


def _split_dot_rhs(g, m_bf16):
    hi = g.astype(BF16)
    lo = (g - hi.astype(F32)).astype(BF16)
    return _dot(hi, m_bf16) + _dot(lo, m_bf16)


def _mix_sample(x2d, state, kc, vc, sinks, mix_w, tabs, layer, prev):
    depth, NB, W = kc.shape[0], kc.shape[1], kc.shape[2]
    Q = x2d.shape[0] // NB
    R = SMP_SEQS * Q
    tab_spec = pl.BlockSpec((R, LANES), lambda i: (0, 0))
    st_spec = pl.BlockSpec((None, SMP_SEQS, GLA_HEADS, GLA_DK, GLA_DV), lambda i: (layer, i, 0, 0, 0))
    kv_spec = pl.BlockSpec((None, SMP_SEQS, W, SWA_KV_W), lambda i: (layer, i, 0, 0))
    n_prev = 0 if prev is None else len(prev)
    return pl.pallas_call(
        functools.partial(_mix_sample_kernel, layer=layer),
        grid=(NB // SMP_SEQS,),
        in_specs=[
            pl.BlockSpec(memory_space=pltpu.SMEM),
            pl.BlockSpec((R, D_MODEL), lambda i: (i, 0)),
            st_spec, kv_spec, kv_spec,
            *_mixer_weight_specs(layer),
            tab_spec, tab_spec, tab_spec,
            *([pl.BlockSpec(memory_space=pl.ANY)] * n_prev),
        ],
        out_specs=[pl.BlockSpec((R, D_MODEL), lambda i: (i, 0)), st_spec, kv_spec, kv_spec],
        out_shape=[
            jax.ShapeDtypeStruct(x2d.shape, F32),
            jax.ShapeDtypeStruct(state.shape, F32),
            jax.ShapeDtypeStruct(kc.shape, F32),
            jax.ShapeDtypeStruct(vc.shape, F32),
        ],
        input_output_aliases={N_SMP_IN + k: 1 + k for k in range(n_prev)},
        compiler_params=pltpu.CompilerParams(
            dimension_semantics=("arbitrary",), vmem_limit_bytes=VMEM_LIMIT),
        name="mix_sample",
    )(sinks, x2d, state, kc, vc, *mix_w, *tabs, *(prev or ()))


def _rope_tables(pos):
    half = ROT_DIM // 2
    inv = jnp.power(jnp.float32(ROPE_THETA), -jnp.arange(half, dtype=jnp.float32) * (2.0 / ROT_DIM))
    ang = pos.astype(jnp.float32)[:, None] * inv[None, :]
    cos, sin = jnp.cos(ang), jnp.sin(ang)
    n = pos.shape[0]
    pad = jnp.zeros((n, HEAD_DIM - ROT_DIM), F32)
    zero = jnp.zeros((n, half), F32)
    c64 = jnp.concatenate([cos, cos, pad + 1.0], axis=1)
    a64 = jnp.concatenate([zero, sin, pad], axis=1)
    b64 = jnp.concatenate([-sin, zero, pad], axis=1)
    rep = LANES // HEAD_DIM
    return tuple(jnp.tile(t, (1, rep)) for t in (c64, a64, b64))


def kernel(x_prompt, x_sample, state_gla, cache_swa_k, cache_swa_v, norm_ffn1, w_ffn1_gu, w_ffn1_down,
           norm_mix, w_in, w_gate_up, b_gate, gla_norm, attn_sinks, swa_norm, w_out, norm_ffn2,
           w_ffn2_gu, w_ffn2_down, norm_final):
    B, L, _ = x_prompt.shape
    NB, Q, _ = x_sample.shape
    depth = w_in.shape[0]
    W = cache_swa_k.shape[2]

    tabs_p = _rope_tables(jnp.arange(L))
    tabs_s = _rope_tables(jnp.tile(PAST_LEN + jnp.arange(Q), SMP_SEQS))

    wa2 = jnp.pad(w_gate_up, ((0, 0), (0, LANES - GATE_RANK), (0, 0))).astype(BF16)
    row = lambda p: p.reshape(depth, 1, -1)
    n1, n2 = row(norm_ffn1), row(norm_ffn2)
    gu_rows = D_MODEL // (B * L // _ffn_tile(B * L))
    dn_rows = D_FF // 11

    def ffn_sides(w_gu, w_down, layer):
        return (SideCast(w_gu, layer, gu_rows), SideCast(w_down, layer, dn_rows))

    xp = x_prompt.reshape(B * L, D_MODEL)
    xs = x_sample.reshape(NB * Q, D_MODEL)
    kc = cache_swa_k.reshape(depth, NB, W, SWA_KV_W)
    vc = cache_swa_v.reshape(depth, NB, W, SWA_KV_W)
    out_p = out_s = None
    gu1, dn1 = _cast_layers((SideCast(w_ffn1_gu, 0, D_MODEL // 8), SideCast(w_ffn1_down, 0, D_FF // 4)))
    for l in range(depth):
        nf = norm_final if l == depth - 1 else None
        xp, win, wout = _ffn(xp, n1, gu1, dn1, l, sides=(SideCast(w_in, l, gu_rows, True),
                                                        SideCast(w_out, l, gu_rows)))
        xs = _ffn(xs, n1, gu1, dn1, l)
        mix_w = (row(norm_mix), win, wa2, row(b_gate), row(gla_norm), row(swa_norm), wout)
        xp3, *out_p = _mix_prompt(xp.reshape(B, L, D_MODEL), attn_sinks, mix_w, tabs_p, l, depth, out_p,
                                  sides=ffn_sides(w_ffn2_gu, w_ffn2_down, l))
        gu2, dn2 = out_p[3:]
        out_p = out_p[:3]
        xs, *out_s = _mix_sample(xs, state_gla, kc, vc, attn_sinks, mix_w, tabs_s, l, out_s)
        xs = _ffn(xs, n2, gu2, dn2, l, nf)
        xp3 = xp3.reshape(B * L, D_MODEL)
        if l + 1 < depth:
            xp, gu1, dn1 = _ffn(xp3, n2, gu2, dn2, l, nf, sides=ffn_sides(w_ffn1_gu, w_ffn1_down, l + 1))
        else:
            xp = _ffn(xp3, n2, gu2, dn2, l, nf)

    gla_p, k_p, v_p = out_p
    gla_s, k_s, v_s = out_s
    kv5 = lambda t: t.reshape(*t.shape[:3], SWA_KV_HEADS, HEAD_DIM)
    return (xp.reshape(B, L, D_MODEL), xs.reshape(NB, Q, D_MODEL), gla_p, kv5(k_p), kv5(v_p),
            gla_s, kv5(k_s), kv5(v_s))
```

```python
import functools
from typing import NamedTuple

import jax
import jax.numpy as jnp
from jax import lax
from jax.experimental import pallas as pl
from jax.experimental.pallas import tpu as pltpu

F32 = jnp.float32
BF16 = jnp.bfloat16

D_MODEL = 1024
D_FF = 2816
EPS = 1e-6
GLA_HEADS = 4
GLA_DK = 64
GLA_DV = 128
GLA_QK_W = GLA_HEADS * GLA_DK
GLA_V_W = GLA_HEADS * GLA_DV
GATE_RANK = 16
GATE_NORM = 16.0
GLA_CHUNK = 64
HEAD_DIM = 64
SWA_HEADS = 8
SWA_KV_HEADS = 2
SWA_GROUP = SWA_HEADS // SWA_KV_HEADS
SWA_Q_W = SWA_HEADS * HEAD_DIM
SWA_KV_W = SWA_KV_HEADS * HEAD_DIM
WINDOW = 128
ROT_DIM = 16
ROPE_THETA = 500000.0
PAST_LEN = 16384
QK_SCALE = 0.125

LANES = 128
C_QG, C_KG, C_VG, C_RG = 0, 256, 512, 1024
C_QS, C_KS, C_VS, C_A = 1536, 2048, 2176, 2304
IN_W_PAD = C_A + LANES

FFN_TILE = 1024
FFN_CHUNK = 256
MIX_TILE = 512
SMP_SEQS = 32
SMP_SUB = 8
VMEM_LIMIT = 56 * 1024 * 1024

NT_DIMS = (((1,), (1,)), ((), ()))


def _dot(a, b):
    return jnp.dot(a, b, preferred_element_type=F32)


def _dot_nt(a, b):
    return lax.dot_general(a, b, NT_DIMS, preferred_element_type=F32)


def _rmsnorm(x, g):
    return x * lax.rsqrt(jnp.mean(x * x, axis=-1, keepdims=True) + EPS) * g


def _log_sigmoid(z):
    return jnp.minimum(z, 0.0) - jnp.log1p(jnp.exp(-jnp.abs(z)))


def _silu(z):
    return z * jax.nn.sigmoid(z)


def _split_dot(m_bf16, g):
    hi = g.astype(BF16)
    lo = (g - hi.astype(F32)).astype(BF16)
    return _dot(m_bf16, hi) + _dot(m_bf16, lo)


def _rope(x, cos, sa, sb):
    half = ROT_DIM // 2
    outs = []
    for p in range(x.shape[1] // LANES):
        s = x[:, p * LANES:(p + 1) * LANES]
        outs.append(s * cos + pltpu.roll(s, half, 1) * sa + pltpu.roll(s, LANES - half, 1) * sb)
    return outs[0] if len(outs) == 1 else jnp.concatenate(outs, axis=1)


class SideCast(NamedTuple):
    src: jax.Array
    layer: int
    block_rows: int
    reorder_w_in: bool = False

    @property
    def n_blocks(self):
        return self.src.shape[1] // self.block_rows

    @property
    def out_cols(self):
        return IN_W_PAD if self.reorder_w_in else self.src.shape[2]


def _cast_block(src_ref, dst_ref, reorder_w_in):
    w = src_ref[...]
    if not reorder_w_in:
        dst_ref[...] = w.astype(BF16)
        return
    n_a = C_RG + GLA_V_W
    dst_ref[:, 0:n_a] = w[:, 0:n_a].astype(BF16)
    dst_ref[:, n_a:C_A] = w[:, n_a + GATE_RANK:].astype(BF16)
    pad = jnp.zeros((w.shape[0], LANES - GATE_RANK), F32)
    dst_ref[:, C_A:IN_W_PAD] = jnp.concatenate([w[:, n_a:n_a + GATE_RANK], pad], axis=1).astype(BF16)


def _side_cast_specs(sides):
    in_specs, out_specs, out_shapes = [], [], []
    for s in sides:
        last = s.n_blocks - 1
        in_specs.append(pl.BlockSpec((None, s.block_rows, s.src.shape[2]),
                                     lambda i, layer=s.layer, last=last: (layer, jnp.minimum(i, last), 0)))
        out_specs.append(pl.BlockSpec((s.block_rows, s.out_cols), lambda i, last=last: (jnp.minimum(i, last), 0)))
        out_shapes.append(jax.ShapeDtypeStruct((s.src.shape[1], s.out_cols), BF16))
    return in_specs, out_specs, out_shapes


def _run_side_casts(step, sides, src_refs, dst_refs):
    for (n_blocks, reorder), src, dst in zip(sides, src_refs, dst_refs):
        @pl.when(step < n_blocks)
        def _():
            _cast_block(src, dst, reorder)


def _cast_kernel(*refs, sides):
    n = len(sides)
    _run_side_casts(pl.program_id(0), sides, refs[:n], refs[n:])


def _cast_layers(sides):
    in_specs, out_specs, out_shapes = _side_cast_specs(sides)
    return pl.pallas_call(
        functools.partial(_cast_kernel, sides=tuple((s.n_blocks, s.reorder_w_in) for s in sides)),
        grid=(max(s.n_blocks for s in sides),),
        in_specs=in_specs,
        out_specs=out_specs,
        out_shape=out_shapes,
        compiler_params=pltpu.CompilerParams(
            dimension_semantics=("arbitrary",), vmem_limit_bytes=VMEM_LIMIT),
        name="cast_weights",
    )(*(s.src for s in sides))


def _ffn_kernel(*refs, final, sides):
    n_in = 5 if final else 4
    n_side = len(sides)
    x_ref, n_ref, wgu_ref, wd_ref = refs[:4]
    nf_ref = refs[4] if final else None
    o_ref = refs[n_in + n_side]
    acc_ref = refs[-1]
    _run_side_casts(pl.program_id(0), sides, refs[n_in:n_in + n_side],
                    refs[n_in + n_side + 1:n_in + 2 * n_side + 1])
    x = x_ref[...]
    h = _rmsnorm(x, n_ref[...]).astype(BF16)
    for c in range(D_FF // FFN_CHUNK):
        lo = c * FFN_CHUNK
        gate = _dot(h, wgu_ref[:, lo:lo + FFN_CHUNK])
        up = _dot(h, wgu_ref[:, D_FF + lo:D_FF + lo + FFN_CHUNK])
        act = (_silu(gate) * up).astype(BF16)
        part = _dot(act, wd_ref[lo:lo + FFN_CHUNK, :])
        if c == 0:
            acc_ref[...] = part
        else:
            acc_ref[...] += part
    y = x + 0.5 * acc_ref[...]
    if final:
        y = _rmsnorm(y, nf_ref[...])
    o_ref[...] = y


def _ffn_tile(n_rows):
    return min(FFN_TILE, n_rows)


def _ffn(x2d, norm, w_gu, w_down, layer, norm_final=None, sides=()):
    final = norm_final is not None
    const = lambda i: (0, 0)
    rows = _ffn_tile(x2d.shape[0])
    tile = pl.BlockSpec((rows, D_MODEL), lambda i: (i, 0))
    in_specs = [
        tile,
        pl.BlockSpec((None, 1, D_MODEL), lambda i: (layer, 0, 0)),
        pl.BlockSpec((D_MODEL, 2 * D_FF), const, pipeline_mode=pl.Buffered(1)),
        pl.BlockSpec((D_FF, D_MODEL), const, pipeline_mode=pl.Buffered(1)),
    ]
    args = [x2d, norm, w_gu, w_down]
    if final:
        in_specs.append(pl.BlockSpec((1, D_MODEL), const))
        args.append(norm_final.reshape(1, D_MODEL))
    side_in, side_out, side_shapes = _side_cast_specs(sides)
    outs = pl.pallas_call(
        functools.partial(_ffn_kernel, final=final,
                          sides=tuple((s.n_blocks, s.reorder_w_in) for s in sides)),
        grid=(x2d.shape[0] // rows,),
        in_specs=in_specs + side_in,
        out_specs=[tile] + side_out,
        out_shape=[jax.ShapeDtypeStruct(x2d.shape, F32)] + side_shapes,
        scratch_shapes=[pltpu.VMEM((rows, D_MODEL), F32)],
        compiler_params=pltpu.CompilerParams(
            dimension_semantics=("arbitrary",), vmem_limit_bytes=VMEM_LIMIT),
        name="ffn_final" if final else "ffn",
    )(*args, *(s.src for s in sides))
    return outs if sides else outs[0]


N_MIX_IN = 10


def _mix_project(x_ref, nmix_ref, win_ref, wa2_ref, ba_ref, glag_ref, rope_ref,
                 qb_s, qsw_s, kd_s, klt_s, dect_s, v_s, gate_s, qh_s, kt_s, vt_s, ko_ref, vo_ref):
    T = MIX_TILE
    G = 256

    x = x_ref[...]
    h = _rmsnorm(x, nmix_ref[...]).astype(BF16)

    a_low = _dot(h, win_ref[:, C_A:C_A + LANES]).astype(BF16)
    z = _dot(a_low, wa2_ref[...]) + ba_ref[...]
    g = _log_sigmoid(z) * (1.0 / GATE_NORM)
    qk = _dot(h, win_ref[:, C_QG:C_QG + 2 * GLA_QK_W])

    ri = lax.broadcasted_iota(jnp.int32, (G, G), 0)
    ci = lax.broadcasted_iota(jnp.int32, (G, G), 1)
    same_chunk = (ri >> 6) == (ci >> 6)
    incl = jnp.where(jnp.logical_and(same_chunk, ci <= ri), 1.0, 0.0).astype(BF16)
    C = GLA_CHUNK
    chunk_last = []
    for r in range(T // G):
        rows = slice(r * G, (r + 1) * G)
        b = _split_dot(incl, g[rows])
        lasts = [b[c * C + C - 1:(c + 1) * C, :] for c in range(G // C)]
        chunk_last.extend(lasts)
        rest = jnp.concatenate([jnp.broadcast_to(t, (C, GLA_QK_W)) for t in lasts], axis=0) - b
        q = qk[rows, 0:GLA_QK_W]
        k = qk[rows, GLA_QK_W:2 * GLA_QK_W]
        qb = q * QK_SCALE * jnp.exp(b)
        qb_s[rows, :] = qb.astype(BF16)
        qsw_s[rows, :] = jnp.concatenate(
            [pltpu.roll(qb[:, p * LANES:(p + 1) * LANES], GLA_DK, 1) for p in range(GLA_QK_W // LANES)],
            axis=1)
        kd_s[rows, :] = k * jnp.exp(-b)
        klt_s[:, rows] = (k * jnp.exp(rest)).T.astype(BF16)
    pad = jnp.zeros((LANES - len(chunk_last), GLA_QK_W), F32)
    dect_s[...] = jnp.exp(jnp.concatenate(chunk_last + [pad], axis=0)).T

    v_s[...] = _dot(h, win_ref[:, C_VG:C_VG + GLA_V_W]).astype(BF16)
    gate_s[...] = _silu(_dot(h, win_ref[:, C_RG:C_RG + GLA_V_W])) * glag_ref[...]

    cos, sa, sb = (rope_ref[:, k * LANES:(k + 1) * LANES] for k in range(3))
    half_t = lax.broadcasted_iota(jnp.int32, (T, LANES), 1) >> 6
    q_rot = _rope(_dot(h, win_ref[:, C_QS:C_QS + SWA_Q_W]), cos, sa, sb) * QK_SCALE
    for hq in range(SWA_HEADS):
        slab = q_rot[:, (hq // 2) * LANES:(hq // 2 + 1) * LANES]
        j = hq // SWA_GROUP
        if hq % 2 != j:
            slab = pltpu.roll(slab, HEAD_DIM, 1)
        qh_s[hq] = jnp.where(half_t == j, slab, 0.0).astype(BF16)
    kv =_dot(h, win_ref[:, C_KS:C_KS + 2 * SWA_KV_W])
    k_rot = _rope(kv[:, 0:SWA_KV_W], cos, sa, sb)
    v_swa = kv[:, SWA_KV_W:2 * SWA_KV_W]
    kt_s[...] = k_rot.astype(BF16)
    vt_s[...] = v_swa.T.astype(BF16)
    ko_ref[...] = k_rot[T - WINDOW:T, :]
    vo_ref[...] = v_swa[T - WINDOW:T, :]


def _mix_sequential(first, layer, sink_ref, x_ref, swag_ref, wout_ref, xo_ref,
                    qb_s, qsw_s, kd_s, klt_s, dect_s, v_s, gate_s, qh_s, kt_s, vt_s,
                    s_s, og_s, cat_s, kcar_s, vcar_s, lhs_s, upd_s, st16_s, sc_s, pr_s, rden_s):
    T = MIX_TILE
    C = GLA_CHUNK

    hk_r = lax.broadcasted_iota(jnp.int32, (GLA_QK_W, GLA_QK_W), 0) >> 6
    hk_c = lax.broadcasted_iota(jnp.int32, (GLA_QK_W, GLA_QK_W), 1) >> 6
    bd_k = hk_r == hk_c
    qi = lax.broadcasted_iota(jnp.int32, (C, GLA_QK_W), 0)
    kj = lax.broadcasted_iota(jnp.int32, (C, GLA_QK_W), 1) & (C - 1)
    causal = kj <= qi
    half_c = lax.broadcasted_iota(jnp.int32, (C, LANES), 1) >> 6
    zeros_v = jnp.zeros((C, GLA_DV), BF16)
    n_c = T // C
    for c in range(n_c):
        rows = slice(c * C, (c + 1) * C)
        kd_bd = jnp.where(bd_k, jnp.concatenate([kd_s[rows, :]] * GLA_HEADS, axis=0), 0.0).astype(BF16)
        att = jnp.where(causal, _dot_nt(qb_s[rows, :], kd_bd), 0.0)
        tok = slice((c // 2) * 2 * C, (c // 2 + 1) * 2 * C)
        for hd in range(GLA_HEADS):
            slab = slice((hd // 2) * LANES, (hd // 2 + 1) * LANES)
            hk = slice(hd * GLA_DK, (hd + 1) * GLA_DK)
            v = v_s[rows, hd * GLA_DV:(hd + 1) * GLA_DV]
            lhs_s[c, hd] = jnp.where(half_c == hd % 2, att[:, slab], qsw_s[rows, slab]).astype(BF16)
            v_pair = jnp.concatenate([v, zeros_v] if c % 2 == 0 else [zeros_v, v], axis=0)
            upd_s[c, hk, :] = _dot(klt_s[hk, tok], v_pair)
    state = s_s[...]
    for c in range(n_c):
        st16_s[c] = state.astype(BF16)
        state = dect_s[:, c:c + 1] * state + upd_s[c]
    s_s[...] = state
    for c in range(n_c):
        rows = slice(c * C, (c + 1) * C)
        for hd in range(GLA_HEADS):
            hk = slice(hd * GLA_DK, (hd + 1) * GLA_DK)
            v = v_s[rows, hd * GLA_DV:(hd + 1) * GLA_DV]
            st = st16_s[c, hk, :]
            rhs = jnp.concatenate([v, st] if hd % 2 == 0 else [st, v], axis=0)
            og_s[rows, hd * GLA_DV:(hd + 1) * GLA_DV] = _dot(lhs_s[c, hd], rhs)

    for hd in range(GLA_HEADS):
        vs_ = slice(hd * GLA_DV, (hd + 1) * GLA_DV)
        o = og_s[:, vs_]
        o = o * lax.rsqrt(jnp.mean(o * o, axis=-1, keepdims=True) + EPS)
        cat_s[:, vs_] = (o * gate_s[:, vs_]).astype(BF16)

    swag = swag_ref[...]
    GR = SWA_GROUP * WINDOW
    kc_ = lax.broadcasted_iota(jnp.int32, (WINDOW, GR), 0)
    qr = lax.broadcasted_iota(jnp.int32, (WINDOW, GR), 1) & (WINDOW - 1)
    from_prev = kc_ > qr
    prev_bias = jnp.where(first > 0.0, -jnp.inf, 0.0)
    n_b = T // WINDOW
    sinks = [jnp.concatenate(
        [jnp.full((1, WINDOW), sink_ref[layer, j * SWA_GROUP + gq], F32) for gq in range(SWA_GROUP)], axis=1)
        for j in range(SWA_KV_HEADS)]
    for i in range(n_b):
        rows = slice(i * WINDOW, (i + 1) * WINDOW)
        if i == 0:
            k_blk = jnp.concatenate([kcar_s[...], kt_s[0:WINDOW, :]], axis=0)
        else:
            k_blk = kt_s[(i - 1) * WINDOW:(i + 1) * WINDOW, :]
        for j in range(SWA_KV_HEADS):
            q4 = jnp.concatenate([qh_s[j * SWA_GROUP + gq, rows, :] for gq in range(SWA_GROUP)], axis=0)
            s2 = _dot_nt(k_blk, q4)
            s_prev = s2[0:WINDOW, :] + prev_bias if i == 0 else s2[0:WINDOW, :]
            sc_s[i, j] = jnp.where(from_prev, s_prev, s2[WINDOW:2 * WINDOW, :])
    for i in range(n_b):
        for j in range(SWA_KV_HEADS):
            s = sc_s[i, j]
            m = jnp.maximum(jnp.max(s, axis=0, keepdims=True), sinks[j])
            pr = jnp.exp(s - m)
            rden_s[i, j] = 1.0 / (jnp.sum(pr, axis=0, keepdims=True) + jnp.exp(sinks[j] - m))
            pr_s[i, j, 0:WINDOW, :] = jnp.where(from_prev, pr, 0.0).astype(BF16)
            pr_s[i, j, WINDOW:2 * WINDOW, :] = jnp.where(from_prev, 0.0, pr).astype(BF16)
    for i in range(n_b):
        rows = slice(i * WINDOW, (i + 1) * WINDOW)
        if i == 0:
            vt_blk = jnp.concatenate([vcar_s[...], vt_s[:, 0:WINDOW]], axis=1)
        else:
            vt_blk = vt_s[:, (i - 1) * WINDOW:(i + 1) * WINDOW]
        pieces = []
        for j in range(SWA_KV_HEADS):
            vt_j = vt_blk[j * HEAD_DIM:(j + 1) * HEAD_DIM, :]
            o_t = _dot(vt_j, pr_s[i, j]) * rden_s[i, j]
            pieces.extend(o_t[:, gq * WINDOW:(gq + 1) * WINDOW] for gq in range(SWA_GROUP))
        o_s = jnp.concatenate(pieces, axis=0).T
        cat_s[rows, GLA_V_W:GLA_V_W + SWA_Q_W] = _rmsnorm(o_s, swag).astype(BF16)

    xo_ref[...] =(x_ref[...] + _dot(cat_s[:, 0:GLA_V_W], wout_ref[0:GLA_V_W, :])
                   + _dot(cat_s[:, GLA_V_W:D_MODEL], wout_ref[GLA_V_W:D_MODEL, :]))

    kcar_s[...] = kt_s[T - WINDOW:T, :]
    vcar_s[...] = vt_s[:, T - WINDOW:T]


N_MIX_OUT = 4
N_MIX_SCRATCH = 21


def _mix_prompt_kernel(*refs, layer, n_l, n_prev, sides):
    (sink_ref, x_ref, nmix_ref, win_ref, wa2_ref, ba_ref, glag_ref, swag_ref,
     wout_ref, rope_ref) = refs[:N_MIX_IN]
    n_side = len(sides)
    side_src = refs[N_MIX_IN + n_prev:N_MIX_IN + n_prev + n_side]
    outs = refs[N_MIX_IN + n_prev + n_side:len(refs) - N_MIX_SCRATCH]
    xo_ref, so_ref, ko_ref, vo_ref = outs[:N_MIX_OUT]
    (qb_s, qsw_s, kd_s, klt_s, dect_s, v_s, gate_s, qh_s, kt_s, vt_s,
     s_s, og_s, cat_s, kcar_s, vcar_s, lhs_s, upd_s, st16_s, sc_s, pr_s, rden_s) = refs[-N_MIX_SCRATCH:]
    handover = (qb_s, qsw_s, kd_s, klt_s, dect_s, v_s, gate_s, qh_s, kt_s, vt_s)
    t = pl.program_id(0)
    l_seq = t % n_l
    _run_side_casts(t, sides, side_src, outs[N_MIX_OUT:])

    @pl.when(l_seq == 0)
    def _():
        s_s[...] = jnp.zeros_like(s_s)
        kcar_s[...] = jnp.zeros_like(kcar_s)
        vcar_s[...] = jnp.zeros_like(vcar_s)

    first = jnp.where(l_seq == 0, 1.0, 0.0)
    _mix_project(x_ref, nmix_ref, win_ref, wa2_ref, ba_ref, glag_ref, rope_ref, *handover, ko_ref, vo_ref)
    _mix_sequential(first, layer, sink_ref, x_ref, swag_ref, wout_ref, xo_ref,
                    *handover, s_s, og_s, cat_s, kcar_s, vcar_s, lhs_s, upd_s, st16_s, sc_s, pr_s, rden_s)

    @pl.when(l_seq == n_l - 1)
    def _():
        for hd in range(GLA_HEADS):
            so_ref[hd] = s_s[hd * GLA_DK:(hd + 1) * GLA_DK, :]


def _mixer_weight_specs(layer):
    at_layer = lambda i: (layer, 0, 0)
    const = lambda i: (0, 0)
    return [
        pl.BlockSpec((None, 1, D_MODEL), at_layer),
        pl.BlockSpec((D_MODEL, IN_W_PAD), const),
        pl.BlockSpec((None, LANES, GLA_QK_W), at_layer),
        pl.BlockSpec((None, 1, GLA_QK_W), at_layer),
        pl.BlockSpec((None, 1, GLA_V_W), at_layer),
        pl.BlockSpec((None, 1, SWA_Q_W), at_layer),
        pl.BlockSpec((D_MODEL, D_MODEL), const),
    ]


def _mix_prompt(x, sinks, mix_w, tabs, layer, depth, prev, sides=()):
    B, L, _ = x.shape
    T = MIX_TILE
    n_l = L // T
    x_spec = pl.BlockSpec((None, T, D_MODEL), lambda t: (t // n_l, t % n_l, 0))
    tab_spec = pl.BlockSpec((T, 3 * LANES), lambda t: (t % n_l, 0))
    n_prev = 0 if prev is None else len(prev)
    side_in, side_out, side_shapes = _side_cast_specs(sides)
    return pl.pallas_call(
        functools.partial(_mix_prompt_kernel, layer=layer, n_l=n_l, n_prev=n_prev,
                          sides=tuple((s.n_blocks, s.reorder_w_in) for s in sides)),
        grid=(B * n_l,),
        in_specs=[
            pl.BlockSpec(memory_space=pltpu.SMEM),
            x_spec,
            *_mixer_weight_specs(layer),
            tab_spec,
            *([pl.BlockSpec(memory_space=pl.ANY)] * n_prev),
            *side_in,
        ],
        out_specs=[
            x_spec,
            pl.BlockSpec((None, None, GLA_HEADS, GLA_DK, GLA_DV), lambda t: (layer, t // n_l, 0, 0, 0)),
            pl.BlockSpec((None, None, WINDOW, SWA_KV_W), lambda t: (layer, t // n_l, 0, 0)),
            pl.BlockSpec((None, None, WINDOW, SWA_KV_W), lambda t: (layer, t // n_l, 0, 0)),
            *side_out,
        ],
        out_shape=[
            jax.ShapeDtypeStruct((B, L, D_MODEL), F32),
            jax.ShapeDtypeStruct((depth, B, GLA_HEADS, GLA_DK, GLA_DV), F32),
            jax.ShapeDtypeStruct((depth, B, WINDOW, SWA_KV_W), F32),
            jax.ShapeDtypeStruct((depth, B, WINDOW, SWA_KV_W), F32),
            *side_shapes,
        ],
        input_output_aliases={N_MIX_IN + k: 1 + k for k in range(n_prev)},
        scratch_shapes=[
            pltpu.VMEM((T, GLA_QK_W), BF16),
            pltpu.VMEM((T, GLA_QK_W), F32),
            pltpu.VMEM((T, GLA_QK_W), F32),
            pltpu.VMEM((GLA_QK_W, T), BF16),
            pltpu.VMEM((GLA_QK_W, LANES), F32),
            pltpu.VMEM((T, GLA_V_W), BF16),
            pltpu.VMEM((T, GLA_V_W), F32),
            pltpu.VMEM((SWA_HEADS, T, LANES), BF16),
            pltpu.VMEM((T, SWA_KV_W), BF16),
            pltpu.VMEM((SWA_KV_W, T), BF16),
            pltpu.VMEM((GLA_QK_W, GLA_DV), F32),
            pltpu.VMEM((T, GLA_V_W), F32),
            pltpu.VMEM((T, D_MODEL), BF16),
            pltpu.VMEM((WINDOW, SWA_KV_W), BF16),
            pltpu.VMEM((SWA_KV_W, WINDOW), BF16),
            pltpu.VMEM((T // GLA_CHUNK, GLA_HEADS, GLA_CHUNK, LANES), BF16),
            pltpu.VMEM((T // GLA_CHUNK, GLA_QK_W, GLA_DV), F32),
            pltpu.VMEM((T // GLA_CHUNK, GLA_QK_W, GLA_DV), BF16),
            pltpu.VMEM((T // WINDOW, SWA_KV_HEADS, WINDOW, SWA_GROUP * WINDOW), F32),
            pltpu.VMEM((T // WINDOW, SWA_KV_HEADS, 2 * WINDOW, SWA_GROUP * WINDOW), BF16),
            pltpu.VMEM((T // WINDOW, SWA_KV_HEADS, 1, SWA_GROUP * WINDOW), F32),
        ],
        compiler_params=pltpu.CompilerParams(
            dimension_semantics=("arbitrary",), vmem_limit_bytes=VMEM_LIMIT),
        name="mix_prompt",
    )(sinks, x, *mix_w, tabs, *(prev or ()), *(s.src for s in sides))


N_SMP_IN = 13


def _mix_sample_kernel(*refs, layer):
    (sink_ref, x_ref, s_ref, kc_ref, vc_ref, nmix_ref, win_ref, wa2_ref, ba_ref,
     glag_ref, swag_ref, wout_ref, rope_ref) = refs[:N_SMP_IN]
    xo_ref, so_ref, ko_ref, vo_ref = refs[-4:]
    R = x_ref.shape[0]
    NS = SMP_SEQS
    Q = R // NS
    q_shift = Q.bit_length() - 1

    x = x_ref[...]
    h = _rmsnorm(x, nmix_ref[...]).astype(BF16)
    proj = _dot(h, win_ref[...])
    z = _dot(proj[:, C_A:C_A + LANES].astype(BF16), wa2_ref[...]) + ba_ref[...]
    g = _log_sigmoid(z) * (1.0 / GATE_NORM)

    ri = lax.broadcasted_iota(jnp.int32, (R, R), 0)
    ci = lax.broadcasted_iota(jnp.int32, (R, R), 1)
    same = (ri >> q_shift) == (ci >> q_shift)
    causal = jnp.logical_and(same, ci <= ri)
    after = jnp.logical_and(same, ci > ri)
    b = _split_dot(jnp.where(causal, 1.0, 0.0).astype(BF16), g)
    rest = _split_dot(jnp.where(after, 1.0, 0.0).astype(BF16), g)

    qg = proj[:, C_QG:C_QG + GLA_QK_W]
    kg = proj[:, C_KG:C_KG + GLA_QK_W]
    qb = qg * QK_SCALE * jnp.exp(b)
    kd = (kg * jnp.exp(-b)).astype(BF16)
    kl_t = (kg * jnp.exp(rest)).T
    g_t = g.T

    own_tok = ((lax.broadcasted_iota(jnp.int32, (NS * GLA_DK, R), 0) >> 6)
               == (lax.broadcasted_iota(jnp.int32, (NS * GLA_DK, R), 1) >> q_shift))
    own_seq = ((lax.broadcasted_iota(jnp.int32, (R, NS * GLA_DK), 1) >> 6)
               == (lax.broadcasted_iota(jnp.int32, (R, NS * GLA_DK), 0) >> q_shift))
    lane_half = lax.broadcasted_iota(jnp.int32, (R, LANES), 1) >> 6
    ones = jnp.ones((R, GLA_DV), BF16)

    glag = glag_ref[...]
    og = []
    for hd in range(GLA_HEADS):
        ks_ = slice(hd * GLA_DK, (hd + 1) * GLA_DK)
        v = proj[:, C_VG + hd * GLA_DV:C_VG + (hd + 1) * GLA_DV].astype(BF16)
        att = jnp.where(causal, _dot_nt(qb[:, ks_].astype(BF16), kd[:, ks_]), 0.0).astype(BF16)
        o = _dot(att, v)
        s_old = s_ref[:, hd].reshape(NS * GLA_DK, GLA_DV)
        slab = qb[:, (hd // 2) * LANES:(hd // 2 + 1) * LANES]
        dup = jnp.where(lane_half == hd % 2, slab, pltpu.roll(slab, HEAD_DIM, 1))
        q_exp = jnp.where(own_seq, jnp.concatenate([dup] * (NS // 2), axis=1), 0.0).astype(BF16)
        o = o + _dot(q_exp, s_old.astype(BF16))
        o = o * lax.rsqrt(jnp.mean(o * o, axis=-1, keepdims=True) + EPS)
        rg = proj[:, C_RG + hd * GLA_DV:C_RG + (hd + 1) * GLA_DV]
        og.append(o * glag[:, hd * GLA_DV:(hd + 1) * GLA_DV] * _silu(rg))
        kl_exp = jnp.where(own_tok, jnp.concatenate([kl_t[ks_, :]] * NS, axis=0), 0.0).astype(BF16)
        g_exp = jnp.where(own_tok, jnp.concatenate([g_t[ks_, :]] * NS, axis=0), 0.0)
        b_last = _split_dot_rhs(g_exp, ones)
        s_new = jnp.exp(b_last) * s_old + _dot(kl_exp, v)
        so_ref[:, hd] = s_new.reshape(NS, GLA_DK, GLA_DV)

    cos, sa, sb = (rope_ref[:, k * LANES:(k + 1) * LANES] for k in range(3))
    q_rot = _rope(proj[:, C_QS:C_QS + SWA_Q_W], cos, sa, sb) * QK_SCALE
    k_rot = _rope(proj[:, C_KS:C_KS + SWA_KV_W], cos, sa, sb)
    v_new = proj[:, C_VS:C_VS + SWA_KV_W]

    W = kc_ref.shape[1]
    SR = SMP_SUB * Q
    GR = SWA_GROUP * SR
    sub_shift = SR.bit_length() - 1
    w_shift = W.bit_length() - 1
    r_c = lax.broadcasted_iota(jnp.int32, (GR, SMP_SUB * W), 0)
    c_c = lax.broadcasted_iota(jnp.int32, (GR, SMP_SUB * W), 1)
    vis_c = jnp.logical_and(((r_c & (SR - 1)) >> q_shift) == (c_c >> w_shift),
                            (c_c & (W - 1)) > (r_c & (Q - 1)))
    r_n = lax.broadcasted_iota(jnp.int32, (GR, SR), 0)
    c_n = lax.broadcasted_iota(jnp.int32, (GR, SR), 1)
    vis_n = jnp.logical_and(((r_n & (SR - 1)) >> q_shift) == (c_n >> q_shift),
                            (c_n & (Q - 1)) <= (r_n & (Q - 1)))
    grp_row = lax.broadcasted_iota(jnp.int32, (GR, 1), 0) >> sub_shift
    half_sr = lax.broadcasted_iota(jnp.int32, (SR, LANES), 1) >> 6

    o_rows = []
    for sg in range(NS // SMP_SUB):
        rows = slice(sg * SR, (sg + 1) * SR)
        k_c = kc_ref[sg * SMP_SUB:(sg + 1) * SMP_SUB].reshape(SMP_SUB * W, SWA_KV_W).astype(BF16)
        v_c = vc_ref[sg * SMP_SUB:(sg + 1) * SMP_SUB].reshape(SMP_SUB * W, SWA_KV_W).astype(BF16)
        k_n = k_rot[rows, :].astype(BF16)
        v_n = v_new[rows, :].astype(BF16)
        o_kv = []
        for j in range(SWA_KV_HEADS):
            parts = []
            for gq in range(SWA_GROUP):
                hq = j * SWA_GROUP + gq
                slab = q_rot[rows, (hq // 2) * LANES:(hq // 2 + 1) * LANES]
                if hq % 2 != j:
                    slab = pltpu.roll(slab, HEAD_DIM, 1)
                parts.append(jnp.where(half_sr == j, slab, 0.0))
            q8 = jnp.concatenate(parts, axis=0).astype(BF16)
            s_c = jnp.where(vis_c, _dot_nt(q8, k_c), -jnp.inf)
            s_n = jnp.where(vis_n, _dot_nt(q8, k_n), -jnp.inf)
            sink = jnp.zeros((GR, 1), F32)
            for gq in range(SWA_GROUP):
                sink = jnp.where(grp_row == gq, sink_ref[layer, j * SWA_GROUP + gq], sink)
            m = jnp.maximum(jnp.maximum(jnp.max(s_c, axis=-1, keepdims=True),
                                        jnp.max(s_n, axis=-1, keepdims=True)), sink)
            p_c = jnp.exp(s_c - m)
            p_n = jnp.exp(s_n - m)
            den = (jnp.sum(p_c, axis=-1, keepdims=True) + jnp.sum(p_n, axis=-1, keepdims=True)
                   + jnp.exp(sink - m))
            o = _dot((p_c / den).astype(BF16), v_c) + _dot((p_n / den).astype(BF16), v_n)
            o_kv.append(o)
        slabs = []
        for p in range(SWA_HEADS // 2):
            halves = []
            for hq in (2 * p, 2 * p + 1):
                j, gq = hq // SWA_GROUP, hq % SWA_GROUP
                piece = o_kv[j][gq * SR:(gq + 1) * SR, :]
                if hq % 2 != j:
                    piece = pltpu.roll(piece, HEAD_DIM, 1)
                halves.append(piece)
            slabs.append(jnp.where(half_sr == 0, halves[0], halves[1]))
        o_rows.append(jnp.concatenate(slabs, axis=1))
    o_s = _rmsnorm(jnp.concatenate(o_rows, axis=0), swag_ref[...])

    cat = jnp.concatenate(og + [o_s], axis=1).astype(BF16)
    xo_ref[...] = x + _dot(cat, wout_ref[...])

    ko_ref[:, 0:W - Q, :] = kc_ref[:, Q:W, :]
    vo_ref[:, 0:W - Q, :] = vc_ref[:, Q:W, :]
    for s in range(NS):
        ko_ref[s, W - Q:W, :] = k_rot[s * Q:(s + 1) * Q, :]
        vo_ref[s, W - Q:W, :] = v_new[s * Q:(s + 1) * Q, :]


def _split_dot_rhs(g, m_bf16):
    hi = g.astype(BF16)
    lo = (g - hi.astype(F32)).astype(BF16)
    return _dot(hi, m_bf16) + _dot(lo, m_bf16)


def _mix_sample(x2d, state, kc, vc, sinks, mix_w, tabs, layer, prev):
    depth, NB, W = kc.shape[0], kc.shape[1], kc.shape[2]
    Q = x2d.shape[0] // NB
    R = SMP_SEQS * Q
    tab_spec = pl.BlockSpec((R, 3 * LANES), lambda i: (0, 0))
    st_spec = pl.BlockSpec((None, SMP_SEQS, GLA_HEADS, GLA_DK, GLA_DV), lambda i: (layer, i, 0, 0, 0))
    kv_spec = pl.BlockSpec((None, SMP_SEQS, W, SWA_KV_W), lambda i: (layer, i, 0, 0))
    n_prev = 0 if prev is None else len(prev)
    return pl.pallas_call(
        functools.partial(_mix_sample_kernel, layer=layer),
        grid=(NB // SMP_SEQS,),
        in_specs=[
            pl.BlockSpec(memory_space=pltpu.SMEM),
            pl.BlockSpec((R, D_MODEL), lambda i: (i, 0)),
            st_spec, kv_spec, kv_spec,
            *_mixer_weight_specs(layer),
            tab_spec,
            *([pl.BlockSpec(memory_space=pl.ANY)] * n_prev),
        ],
        out_specs=[pl.BlockSpec((R, D_MODEL), lambda i: (i, 0)), st_spec, kv_spec, kv_spec],
        out_shape=[
            jax.ShapeDtypeStruct(x2d.shape, F32),
            jax.ShapeDtypeStruct(state.shape, F32),
            jax.ShapeDtypeStruct(kc.shape, F32),
            jax.ShapeDtypeStruct(vc.shape, F32),
        ],
        input_output_aliases={N_SMP_IN + k: 1 + k for k in range(n_prev)},
        compiler_params=pltpu.CompilerParams(
            dimension_semantics=("arbitrary",), vmem_limit_bytes=VMEM_LIMIT),
        name="mix_sample",
    )(sinks, x2d, state, kc, vc, *mix_w, tabs, *(prev or ()))


def _rope_tables(pos):
    half = ROT_DIM // 2
    inv = jnp.power(jnp.float32(ROPE_THETA), -jnp.arange(half, dtype=jnp.float32) * (2.0 / ROT_DIM))
    ang = pos.astype(jnp.float32)[:, None] * inv[None, :]
    cos, sin = jnp.cos(ang), jnp.sin(ang)
    n = pos.shape[0]
    pad = jnp.zeros((n, HEAD_DIM - ROT_DIM), F32)
    zero = jnp.zeros((n, half), F32)
    c64 = jnp.concatenate([cos, cos, pad + 1.0], axis=1)
    a64 = jnp.concatenate([zero, sin, pad], axis=1)
    b64 = jnp.concatenate([-sin, zero, pad], axis=1)
    rep = LANES // HEAD_DIM
    return jnp.concatenate([jnp.tile(t, (1, rep)) for t in (c64, a64, b64)], axis=1)


def kernel(x_prompt, x_sample, state_gla, cache_swa_k, cache_swa_v, norm_ffn1, w_ffn1_gu, w_ffn1_down,
           norm_mix, w_in, w_gate_up, b_gate, gla_norm, attn_sinks, swa_norm, w_out, norm_ffn2,
           w_ffn2_gu, w_ffn2_down, norm_final):
    B, L, _ = x_prompt.shape
    NB, Q, _ = x_sample.shape
    depth = w_in.shape[0]
    W = cache_swa_k.shape[2]

    tabs_p = _rope_tables(jnp.arange(L))
    tabs_s = _rope_tables(jnp.tile(PAST_LEN + jnp.arange(Q), SMP_SEQS))

    wa2 = jnp.pad(w_gate_up, ((0, 0), (0, LANES - GATE_RANK), (0, 0))).astype(BF16)
    row = lambda p: p.reshape(depth, 1, -1)
    n1, n2 = row(norm_ffn1), row(norm_ffn2)
    gu_rows = D_MODEL // (B * L // _ffn_tile(B * L))
    dn_rows = D_FF // 11

    def ffn_sides(w_gu, w_down, layer):
        return (SideCast(w_gu, layer, gu_rows), SideCast(w_down, layer, dn_rows))

    xp = x_prompt.reshape(B * L, D_MODEL)
    xs = x_sample.reshape(NB * Q, D_MODEL)
    kc = cache_swa_k.reshape(depth, NB, W, SWA_KV_W)
    vc = cache_swa_v.reshape(depth, NB, W, SWA_KV_W)
    out_p = out_s = None
    gu1, dn1 = _cast_layers((SideCast(w_ffn1_gu, 0, D_MODEL // 8), SideCast(w_ffn1_down, 0, D_FF // 4)))
    for l in range(depth):
        nf = norm_final if l == depth - 1 else None
        xp, win, wout = _ffn(xp, n1, gu1, dn1, l, sides=(SideCast(w_in, l, gu_rows, True),
                                                        SideCast(w_out, l, gu_rows)))
        xs = _ffn(xs, n1, gu1, dn1, l)
        mix_w = (row(norm_mix), win, wa2, row(b_gate), row(gla_norm), row(swa_norm), wout)
        xp3, *out_p = _mix_prompt(xp.reshape(B, L, D_MODEL), attn_sinks, mix_w, tabs_p, l, depth, out_p,
                                  sides=ffn_sides(w_ffn2_gu, w_ffn2_down, l))
        gu2, dn2 = out_p[3:]
        out_p = out_p[:3]
        xs, *out_s = _mix_sample(xs, state_gla, kc, vc, attn_sinks, mix_w, tabs_s, l, out_s)
        xs = _ffn(xs, n2, gu2, dn2, l, nf)
        xp3 = xp3.reshape(B * L, D_MODEL)
        if l + 1 < depth:
            xp, gu1, dn1 = _ffn(xp3, n2, gu2, dn2, l, nf, sides=ffn_sides(w_ffn1_gu, w_ffn1_down, l + 1))
        else:
            xp = _ffn(xp3, n2, gu2, dn2, l, nf)

    gla_p, k_p, v_p = out_p
    gla_s, k_s, v_s = out_s
    kv5 = lambda t: t.reshape(*t.shape[:3], SWA_KV_HEADS, HEAD_DIM)
    return (xp.reshape(B, L, D_MODEL), xs.reshape(NB, Q, D_MODEL), gla_p, kv5(k_p), kv5(v_p),
            gla_s, kv5(k_s), kv5(v_s))
```

```python
import functools
from typing import NamedTuple

import jax
import jax.numpy as jnp
from jax import lax
from jax.experimental import pallas as pl
from jax.experimental.pallas import tpu as pltpu

F32 = jnp.float32
BF16 = jnp.bfloat16

D_MODEL = 1024
D_FF = 2816
EPS = 1e-6
GLA_HEADS = 4
GLA_DK = 64
GLA_DV = 128
GLA_QK_W = GLA_HEADS * GLA_DK
GLA_V_W = GLA_HEADS * GLA_DV
GATE_RANK = 16
GATE_NORM = 16.0
GLA_CHUNK = 64
HEAD_DIM = 64
SWA_HEADS = 8
SWA_KV_HEADS = 2
SWA_GROUP = SWA_HEADS // SWA_KV_HEADS
SWA_Q_W = SWA_HEADS * HEAD_DIM
SWA_KV_W = SWA_KV_HEADS * HEAD_DIM
WINDOW = 128
ROT_DIM = 16
ROPE_THETA = 500000.0
PAST_LEN = 16384
QK_SCALE = 0.125
LOG2_E = 1.4426950408889634

LANES = 128
C_QG, C_KG, C_VG, C_RG = 0, 256, 512, 1024
C_QS, C_KS, C_VS, C_A = 1536, 2048, 2176, 2304
IN_W_PAD = C_A + LANES

FFN_TILE = 1024
FFN_CHUNK = 256
MIX_TILE = 512
SMP_SEQS = 32
SMP_SUB = 8
VMEM_LIMIT = 56 * 1024 * 1024

NT_DIMS = (((1,), (1,)), ((), ()))


def _dot(a, b):
    return jnp.dot(a, b, preferred_element_type=F32)


def _dot_nt(a, b):
    return lax.dot_general(a, b, NT_DIMS, preferred_element_type=F32)


def _rmsnorm(x, g):
    return x * lax.rsqrt(jnp.mean(x * x, axis=-1, keepdims=True) + EPS) * g


def _log_sigmoid(z):
    return jnp.minimum(z, 0.0) - jnp.log1p(jnp.exp(-jnp.abs(z)))


def _silu(z):
    return z * jax.nn.sigmoid(z)


def _split_dot(m_bf16, g):
    hi = g.astype(BF16)
    lo = (g - hi.astype(F32)).astype(BF16)
    return _dot(m_bf16, hi) + _dot(m_bf16, lo)


def _rope(x, cos, sa, sb):
    half = ROT_DIM // 2
    outs = []
    for p in range(x.shape[1] // LANES):
        s = x[:, p * LANES:(p + 1) * LANES]
        outs.append(s * cos + pltpu.roll(s, half, 1) * sa + pltpu.roll(s, LANES - half, 1) * sb)
    return outs[0] if len(outs) == 1 else jnp.concatenate(outs, axis=1)


class SideCast(NamedTuple):
    src: jax.Array
    layer: int
    block_rows: int
    reorder_w_in: bool = False

    @property
    def n_blocks(self):
        return self.src.shape[1] // self.block_rows

    @property
    def out_cols(self):
        return IN_W_PAD if self.reorder_w_in else self.src.shape[2]


def _cast_block(src_ref, dst_ref, reorder_w_in):
    w = src_ref[...]
    if not reorder_w_in:
        dst_ref[...] = w.astype(BF16)
        return
    n_a = C_RG + GLA_V_W
    dst_ref[:, 0:n_a] = w[:, 0:n_a].astype(BF16)
    dst_ref[:, n_a:C_A] = w[:, n_a + GATE_RANK:].astype(BF16)
    pad = jnp.zeros((w.shape[0], LANES - GATE_RANK), F32)
    dst_ref[:, C_A:IN_W_PAD] = jnp.concatenate([w[:, n_a:n_a + GATE_RANK], pad], axis=1).astype(BF16)


def _side_cast_specs(sides):
    in_specs, out_specs, out_shapes = [], [], []
    for s in sides:
        last = s.n_blocks - 1
        in_specs.append(pl.BlockSpec((None, s.block_rows, s.src.shape[2]),
                                     lambda i, layer=s.layer, last=last: (layer, jnp.minimum(i, last), 0)))
        out_specs.append(pl.BlockSpec((s.block_rows, s.out_cols), lambda i, last=last: (jnp.minimum(i, last), 0)))
        out_shapes.append(jax.ShapeDtypeStruct((s.src.shape[1], s.out_cols), BF16))
    return in_specs, out_specs, out_shapes


def _run_side_casts(step, sides, src_refs, dst_refs):
    for (n_blocks, reorder), src, dst in zip(sides, src_refs, dst_refs):
        @pl.when(step < n_blocks)
        def _():
            _cast_block(src, dst, reorder)


def _cast_kernel(*refs, sides):
    n = len(sides)
    _run_side_casts(pl.program_id(0), sides, refs[:n], refs[n:])


def _cast_layers(sides):
    in_specs, out_specs, out_shapes = _side_cast_specs(sides)
    return pl.pallas_call(
        functools.partial(_cast_kernel, sides=tuple((s.n_blocks, s.reorder_w_in) for s in sides)),
        grid=(max(s.n_blocks for s in sides),),
        in_specs=in_specs,
        out_specs=out_specs,
        out_shape=out_shapes,
        compiler_params=pltpu.CompilerParams(
            dimension_semantics=("arbitrary",), vmem_limit_bytes=VMEM_LIMIT),
        name="cast_weights",
    )(*(s.src for s in sides))


def _ffn_kernel(*refs, final, sides):
    n_in = 5 if final else 4
    n_side = len(sides)
    x_ref, n_ref, wgu_ref, wd_ref = refs[:4]
    nf_ref = refs[4] if final else None
    o_ref = refs[n_in + n_side]
    acc_ref = refs[-1]
    _run_side_casts(pl.program_id(0), sides, refs[n_in:n_in + n_side],
                    refs[n_in + n_side + 1:n_in + 2 * n_side + 1])
    x = x_ref[...]
    h = _rmsnorm(x, n_ref[...]).astype(BF16)
    for c in range(D_FF // FFN_CHUNK):
        lo = c * FFN_CHUNK
        gate = _dot(h, wgu_ref[:, lo:lo + FFN_CHUNK])
        up = _dot(h, wgu_ref[:, D_FF + lo:D_FF + lo + FFN_CHUNK])
        act = (_silu(gate) * up).astype(BF16)
        part = _dot(act, wd_ref[lo:lo + FFN_CHUNK, :])
        if c == 0:
            acc_ref[...] = part
        else:
            acc_ref[...] += part
    y = x + 0.5 * acc_ref[...]
    if final:
        y = _rmsnorm(y, nf_ref[...])
    o_ref[...] = y


def _ffn_tile(n_rows):
    return min(FFN_TILE, n_rows)


def _ffn(x2d, norm, w_gu, w_down, layer, norm_final=None, sides=()):
    final = norm_final is not None
    const = lambda i: (0, 0)
    rows = _ffn_tile(x2d.shape[0])
    tile = pl.BlockSpec((rows, D_MODEL), lambda i: (i, 0))
    in_specs = [
        tile,
        pl.BlockSpec((None, 1, D_MODEL), lambda i: (layer, 0, 0)),
        pl.BlockSpec((D_MODEL, 2 * D_FF), const, pipeline_mode=pl.Buffered(1)),
        pl.BlockSpec((D_FF, D_MODEL), const, pipeline_mode=pl.Buffered(1)),
    ]
    args = [x2d, norm, w_gu, w_down]
    if final:
        in_specs.append(pl.BlockSpec((1, D_MODEL), const))
        args.append(norm_final.reshape(1, D_MODEL))
    side_in, side_out, side_shapes = _side_cast_specs(sides)
    outs = pl.pallas_call(
        functools.partial(_ffn_kernel, final=final,
                          sides=tuple((s.n_blocks, s.reorder_w_in) for s in sides)),
        grid=(x2d.shape[0] // rows,),
        in_specs=in_specs + side_in,
        out_specs=[tile] + side_out,
        out_shape=[jax.ShapeDtypeStruct(x2d.shape, F32)] + side_shapes,
        scratch_shapes=[pltpu.VMEM((rows, D_MODEL), F32)],
        compiler_params=pltpu.CompilerParams(
            dimension_semantics=("arbitrary",), vmem_limit_bytes=VMEM_LIMIT),
        name="ffn_final" if final else "ffn",
    )(*args, *(s.src for s in sides))
    return outs if sides else outs[0]


N_MIX_IN = 10


def _mix_project(x_ref, nmix_ref, win_ref, wa2_ref, ba_ref, glag_ref, rope_ref,
                 qb_s, qsw_s, kd_s, klt_s, dect_s, v_s, gate_s, qh_s, kt_s, vt_s, ko_ref, vo_ref):
    T = MIX_TILE
    G = 256

    x = x_ref[...]
    h = _rmsnorm(x, nmix_ref[...]).astype(BF16)

    a_low = _dot(h, win_ref[:, C_A:C_A + LANES]).astype(BF16)
    z = _dot(a_low, wa2_ref[...]) + ba_ref[...]
    g = _log_sigmoid(z) * (LOG2_E / GATE_NORM)
    qk =_dot(h, win_ref[:, C_QG:C_QG + 2 * GLA_QK_W])

    ri = lax.broadcasted_iota(jnp.int32, (G, G), 0)
    ci = lax.broadcasted_iota(jnp.int32, (G, G), 1)
    same_chunk = (ri >> 6) == (ci >> 6)
    incl = jnp.where(jnp.logical_and(same_chunk, ci <= ri), 1.0, 0.0).astype(BF16)
    C = GLA_CHUNK
    chunk_last = []
    for r in range(T // G):
        rows = slice(r * G, (r + 1) * G)
        b = _split_dot(incl, g[rows])
        lasts = [b[c * C + C - 1:(c + 1) * C, :] for c in range(G // C)]
        chunk_last.extend(lasts)
        rest = jnp.concatenate([jnp.broadcast_to(t, (C, GLA_QK_W)) for t in lasts], axis=0) - b
        q = qk[rows, 0:GLA_QK_W]
        k = qk[rows, GLA_QK_W:2 * GLA_QK_W]
        qb = q * QK_SCALE * jnp.exp2(b)
        qb_s[rows, :] = qb.astype(BF16)
        qsw_s[rows, :] = jnp.concatenate(
            [pltpu.roll(qb[:, p * LANES:(p + 1) * LANES], GLA_DK, 1) for p in range(GLA_QK_W // LANES)],
            axis=1)
        kd_s[rows, :] = k * jnp.exp2(-b)
        klt_s[:, rows] = (k * jnp.exp2(rest)).T.astype(BF16)
    pad = jnp.zeros((LANES - len(chunk_last), GLA_QK_W), F32)
    dect_s[...] = jnp.exp2(jnp.concatenate(chunk_last + [pad], axis=0)).T

    v_s[...] = _dot(h, win_ref[:, C_VG:C_VG + GLA_V_W]).astype(BF16)
    gate_s[...] = _silu(_dot(h, win_ref[:, C_RG:C_RG + GLA_V_W])) * glag_ref[...]

    cos, sa, sb = (rope_ref[:, k * LANES:(k + 1) * LANES] for k in range(3))
    half_t = lax.broadcasted_iota(jnp.int32, (T, LANES), 1) >> 6
    q_rot = _rope(_dot(h, win_ref[:, C_QS:C_QS + SWA_Q_W]), cos, sa, sb) * (QK_SCALE * LOG2_E)
    for hq in range(SWA_HEADS):
        slab = q_rot[:, (hq // 2) * LANES:(hq // 2 + 1) * LANES]
        j = hq // SWA_GROUP
        if hq % 2 != j:
            slab = pltpu.roll(slab, HEAD_DIM, 1)
        qh_s[hq] = jnp.where(half_t == j, slab, 0.0).astype(BF16)
    kv = _dot(h, win_ref[:, C_KS:C_KS + 2 * SWA_KV_W])
    k_rot = _rope(kv[:, 0:SWA_KV_W], cos, sa, sb)
    v_swa = kv[:, SWA_KV_W:2 * SWA_KV_W]
    kt_s[...] = k_rot.astype(BF16)
    vt_s[...] = v_swa.T.astype(BF16)
    ko_ref[...] = k_rot[T - WINDOW:T, :]
    vo_ref[...] = v_swa[T - WINDOW:T, :]


def _mix_sequential(first, layer, sink_ref, x_ref, swag_ref, wout_ref, xo_ref,
                    qb_s, qsw_s, kd_s, klt_s, dect_s, v_s, gate_s, qh_s, kt_s, vt_s,
                    s_s, og_s, cat_s, kcar_s, vcar_s, lhs_s, upd_s, st16_s, sc_s, pr_s, rden_s):
    T = MIX_TILE
    C = GLA_CHUNK

    hk_r = lax.broadcasted_iota(jnp.int32, (GLA_QK_W, GLA_QK_W), 0) >> 6
    hk_c = lax.broadcasted_iota(jnp.int32, (GLA_QK_W, GLA_QK_W), 1) >> 6
    bd_k = hk_r == hk_c
    qi = lax.broadcasted_iota(jnp.int32, (C, GLA_QK_W), 0)
    kj = lax.broadcasted_iota(jnp.int32, (C, GLA_QK_W), 1) & (C - 1)
    causal = kj <= qi
    half_c = lax.broadcasted_iota(jnp.int32, (C, LANES), 1) >> 6
    zeros_v = jnp.zeros((C, GLA_DV), BF16)
    n_c = T // C
    for c in range(n_c):
        rows = slice(c * C, (c + 1) * C)
        kd_bd = jnp.where(bd_k, jnp.concatenate([kd_s[rows, :]] * GLA_HEADS, axis=0), 0.0).astype(BF16)
        att = jnp.where(causal, _dot_nt(qb_s[rows, :], kd_bd), 0.0)
        tok = slice((c // 2) * 2 * C, (c // 2 + 1) * 2 * C)
        for hd in range(GLA_HEADS):
            slab = slice((hd // 2) * LANES, (hd // 2 + 1) * LANES)
            hk = slice(hd * GLA_DK, (hd + 1) * GLA_DK)
            v = v_s[rows, hd * GLA_DV:(hd + 1) * GLA_DV]
            lhs_s[c, hd] = jnp.where(half_c == hd % 2, att[:, slab], qsw_s[rows, slab]).astype(BF16)
            v_pair = jnp.concatenate([v, zeros_v] if c % 2 == 0 else [zeros_v, v], axis=0)
            upd_s[c, hk, :] = _dot(klt_s[hk, tok], v_pair)
    state = s_s[...]
    for c in range(n_c):
        st16_s[c] = state.astype(BF16)
        state = dect_s[:, c:c + 1] * state + upd_s[c]
    s_s[...] = state
    for c in range(n_c):
        rows = slice(c * C, (c + 1) * C)
        for hd in range(GLA_HEADS):
            hk = slice(hd * GLA_DK, (hd + 1) * GLA_DK)
            v = v_s[rows, hd * GLA_DV:(hd + 1) * GLA_DV]
            st = st16_s[c, hk, :]
            rhs = jnp.concatenate([v, st] if hd % 2 == 0 else [st, v], axis=0)
            og_s[rows, hd * GLA_DV:(hd + 1) * GLA_DV] = _dot(lhs_s[c, hd], rhs)

    for hd in range(GLA_HEADS):
        vs_ = slice(hd * GLA_DV, (hd + 1) * GLA_DV)
        o = og_s[:, vs_]
        o = o * lax.rsqrt(jnp.mean(o * o, axis=-1, keepdims=True) + EPS)
        cat_s[:, vs_] = (o * gate_s[:, vs_]).astype(BF16)

    swag = swag_ref[...]
    GR = SWA_GROUP * WINDOW
    kc_ = lax.broadcasted_iota(jnp.int32, (WINDOW, GR), 0)
    qr = lax.broadcasted_iota(jnp.int32, (WINDOW, GR), 1) & (WINDOW - 1)
    from_prev = kc_ > qr
    prev_bias = jnp.where(first > 0.0, -jnp.inf, 0.0)
    n_b = T // WINDOW
    sinks = [jnp.concatenate(
        [jnp.full((1, WINDOW), sink_ref[layer, j * SWA_GROUP + gq] * LOG2_E, F32) for gq in range(SWA_GROUP)],
        axis=1)
        for j in range(SWA_KV_HEADS)]
    for i in range(n_b):
        rows = slice(i * WINDOW, (i + 1) * WINDOW)
        if i == 0:
            k_blk = jnp.concatenate([kcar_s[...], kt_s[0:WINDOW, :]], axis=0)
        else:
            k_blk = kt_s[(i - 1) * WINDOW:(i + 1) * WINDOW, :]
        for j in range(SWA_KV_HEADS):
            q4 = jnp.concatenate([qh_s[j * SWA_GROUP + gq, rows, :] for gq in range(SWA_GROUP)], axis=0)
            s2 = _dot_nt(k_blk, q4)
            s_prev = s2[0:WINDOW, :] + prev_bias if i == 0 else s2[0:WINDOW, :]
            sc_s[i, j] = jnp.where(from_prev, s_prev, s2[WINDOW:2 * WINDOW, :])
    for i in range(n_b):
        for j in range(SWA_KV_HEADS):
            s = sc_s[i, j]
            m = jnp.maximum(jnp.max(s, axis=0, keepdims=True), sinks[j])
            pr = jnp.exp2(s - m)
            rden_s[i, j] = 1.0 / (jnp.sum(pr, axis=0, keepdims=True) + jnp.exp2(sinks[j] - m))
            pr_s[i, j, 0:WINDOW, :] = jnp.where(from_prev, pr, 0.0).astype(BF16)
            pr_s[i, j, WINDOW:2 * WINDOW, :] = jnp.where(from_prev, 0.0, pr).astype(BF16)
    for i in range(n_b):
        rows = slice(i * WINDOW, (i + 1) * WINDOW)
        if i == 0:
            vt_blk = jnp.concatenate([vcar_s[...], vt_s[:, 0:WINDOW]], axis=1)
        else:
            vt_blk = vt_s[:, (i - 1) * WINDOW:(i + 1) * WINDOW]
        pieces = []
        for j in range(SWA_KV_HEADS):
            vt_j = vt_blk[j * HEAD_DIM:(j + 1) * HEAD_DIM, :]
            o_t = _dot(vt_j, pr_s[i, j]) * rden_s[i, j]
            pieces.extend(o_t[:, gq * WINDOW:(gq + 1) * WINDOW] for gq in range(SWA_GROUP))
        o_s = jnp.concatenate(pieces, axis=0).T
        cat_s[rows, GLA_V_W:GLA_V_W + SWA_Q_W] = _rmsnorm(o_s, swag).astype(BF16)

    xo_ref[...] =(x_ref[...] + _dot(cat_s[:, 0:GLA_V_W], wout_ref[0:GLA_V_W, :])
                   + _dot(cat_s[:, GLA_V_W:D_MODEL], wout_ref[GLA_V_W:D_MODEL, :]))

    kcar_s[...] = kt_s[T - WINDOW:T, :]
    vcar_s[...] = vt_s[:, T - WINDOW:T]


N_MIX_OUT = 4
N_MIX_SCRATCH = 21


def _mix_prompt_kernel(*refs, layer, n_l, n_prev, sides):
    (sink_ref, x_ref, nmix_ref, win_ref, wa2_ref, ba_ref, glag_ref, swag_ref,
     wout_ref, rope_ref) = refs[:N_MIX_IN]
    n_side = len(sides)
    side_src = refs[N_MIX_IN + n_prev:N_MIX_IN + n_prev + n_side]
    outs = refs[N_MIX_IN + n_prev + n_side:len(refs) - N_MIX_SCRATCH]
    xo_ref, so_ref, ko_ref, vo_ref = outs[:N_MIX_OUT]
    (qb_s, qsw_s, kd_s, klt_s, dect_s, v_s, gate_s, qh_s, kt_s, vt_s,
     s_s, og_s, cat_s, kcar_s, vcar_s, lhs_s, upd_s, st16_s, sc_s, pr_s, rden_s) = refs[-N_MIX_SCRATCH:]
    handover = (qb_s, qsw_s, kd_s, klt_s, dect_s, v_s, gate_s, qh_s, kt_s, vt_s)
    t = pl.program_id(0)
    l_seq = t % n_l
    _run_side_casts(t, sides, side_src, outs[N_MIX_OUT:])

    @pl.when(l_seq == 0)
    def _():
        s_s[...] = jnp.zeros_like(s_s)
        kcar_s[...] = jnp.zeros_like(kcar_s)
        vcar_s[...] = jnp.zeros_like(vcar_s)

    first = jnp.where(l_seq == 0, 1.0, 0.0)
    _mix_project(x_ref, nmix_ref, win_ref, wa2_ref, ba_ref, glag_ref, rope_ref, *handover, ko_ref, vo_ref)
    _mix_sequential(first, layer, sink_ref, x_ref, swag_ref, wout_ref, xo_ref,
                    *handover, s_s, og_s, cat_s, kcar_s, vcar_s, lhs_s, upd_s, st16_s, sc_s, pr_s, rden_s)

    @pl.when(l_seq == n_l - 1)
    def _():
        for hd in range(GLA_HEADS):
            so_ref[hd] = s_s[hd * GLA_DK:(hd + 1) * GLA_DK, :]


def _mixer_weight_specs(layer):
    at_layer = lambda i: (layer, 0, 0)
    const = lambda i: (0, 0)
    return [
        pl.BlockSpec((None, 1, D_MODEL), at_layer),
        pl.BlockSpec((D_MODEL, IN_W_PAD), const),
        pl.BlockSpec((None, LANES, GLA_QK_W), at_layer),
        pl.BlockSpec((None, 1, GLA_QK_W), at_layer),
        pl.BlockSpec((None, 1, GLA_V_W), at_layer),
        pl.BlockSpec((None, 1, SWA_Q_W), at_layer),
        pl.BlockSpec((D_MODEL, D_MODEL), const),
    ]


def _mix_prompt(x, sinks, mix_w, tabs, layer, depth, prev, sides=()):
    B, L, _ = x.shape
    T = MIX_TILE
    n_l = L // T
    x_spec = pl.BlockSpec((None, T, D_MODEL), lambda t: (t // n_l, t % n_l, 0))
    tab_spec = pl.BlockSpec((T, 3 * LANES), lambda t: (t % n_l, 0))
    n_prev = 0 if prev is None else len(prev)
    side_in, side_out, side_shapes = _side_cast_specs(sides)
    return pl.pallas_call(
        functools.partial(_mix_prompt_kernel, layer=layer, n_l=n_l, n_prev=n_prev,
                          sides=tuple((s.n_blocks, s.reorder_w_in) for s in sides)),
        grid=(B * n_l,),
        in_specs=[
            pl.BlockSpec(memory_space=pltpu.SMEM),
            x_spec,
            *_mixer_weight_specs(layer),
            tab_spec,
            *([pl.BlockSpec(memory_space=pl.ANY)] * n_prev),
            *side_in,
        ],
        out_specs=[
            x_spec,
            pl.BlockSpec((None, None, GLA_HEADS, GLA_DK, GLA_DV), lambda t: (layer, t // n_l, 0, 0, 0)),
            pl.BlockSpec((None, None, WINDOW, SWA_KV_W), lambda t: (layer, t // n_l, 0, 0)),
            pl.BlockSpec((None, None, WINDOW, SWA_KV_W), lambda t: (layer, t // n_l, 0, 0)),
            *side_out,
        ],
        out_shape=[
            jax.ShapeDtypeStruct((B, L, D_MODEL), F32),
            jax.ShapeDtypeStruct((depth, B, GLA_HEADS, GLA_DK, GLA_DV), F32),
            jax.ShapeDtypeStruct((depth, B, WINDOW, SWA_KV_W), F32),
            jax.ShapeDtypeStruct((depth, B, WINDOW, SWA_KV_W), F32),
            *side_shapes,
        ],
        input_output_aliases={N_MIX_IN + k: 1 + k for k in range(n_prev)},
        scratch_shapes=[
            pltpu.VMEM((T, GLA_QK_W), BF16),
            pltpu.VMEM((T, GLA_QK_W), F32),
            pltpu.VMEM((T, GLA_QK_W), F32),
            pltpu.VMEM((GLA_QK_W, T), BF16),
            pltpu.VMEM((GLA_QK_W, LANES), F32),
            pltpu.VMEM((T, GLA_V_W), BF16),
            pltpu.VMEM((T, GLA_V_W), F32),
            pltpu.VMEM((SWA_HEADS, T, LANES), BF16),
            pltpu.VMEM((T, SWA_KV_W), BF16),
            pltpu.VMEM((SWA_KV_W, T), BF16),
            pltpu.VMEM((GLA_QK_W, GLA_DV), F32),
            pltpu.VMEM((T, GLA_V_W), F32),
            pltpu.VMEM((T, D_MODEL), BF16),
            pltpu.VMEM((WINDOW, SWA_KV_W), BF16),
            pltpu.VMEM((SWA_KV_W, WINDOW), BF16),
            pltpu.VMEM((T // GLA_CHUNK, GLA_HEADS, GLA_CHUNK, LANES), BF16),
            pltpu.VMEM((T // GLA_CHUNK, GLA_QK_W, GLA_DV), F32),
            pltpu.VMEM((T // GLA_CHUNK, GLA_QK_W, GLA_DV), BF16),
            pltpu.VMEM((T // WINDOW, SWA_KV_HEADS, WINDOW, SWA_GROUP * WINDOW), F32),
            pltpu.VMEM((T // WINDOW, SWA_KV_HEADS, 2 * WINDOW, SWA_GROUP * WINDOW), BF16),
            pltpu.VMEM((T // WINDOW, SWA_KV_HEADS, 1, SWA_GROUP * WINDOW), F32),
        ],
        compiler_params=pltpu.CompilerParams(
            dimension_semantics=("arbitrary",), vmem_limit_bytes=VMEM_LIMIT),
        name="mix_prompt",
    )(sinks, x, *mix_w, tabs, *(prev or ()), *(s.src for s in sides))


N_SMP_IN = 13


def _mix_sample_kernel(*refs, layer):
    (sink_ref, x_ref, s_ref, kc_ref, vc_ref, nmix_ref, win_ref, wa2_ref, ba_ref,
     glag_ref, swag_ref, wout_ref, rope_ref) = refs[:N_SMP_IN]
    xo_ref, so_ref, ko_ref, vo_ref = refs[-4:]
    R = x_ref.shape[0]
    NS = SMP_SEQS
    Q = R // NS
    q_shift = Q.bit_length() - 1

    x = x_ref[...]
    h = _rmsnorm(x, nmix_ref[...]).astype(BF16)
    proj = _dot(h, win_ref[...])
    z = _dot(proj[:, C_A:C_A + LANES].astype(BF16), wa2_ref[...]) + ba_ref[...]
    g = _log_sigmoid(z) * (1.0 / GATE_NORM)

    ri = lax.broadcasted_iota(jnp.int32, (R, R), 0)
    ci = lax.broadcasted_iota(jnp.int32, (R, R), 1)
    same = (ri >> q_shift) == (ci >> q_shift)
    causal = jnp.logical_and(same, ci <= ri)
    after = jnp.logical_and(same, ci > ri)
    b = _split_dot(jnp.where(causal, 1.0, 0.0).astype(BF16), g)
    rest = _split_dot(jnp.where(after, 1.0, 0.0).astype(BF16), g)

    qg = proj[:, C_QG:C_QG + GLA_QK_W]
    kg = proj[:, C_KG:C_KG + GLA_QK_W]
    qb = qg * QK_SCALE * jnp.exp(b)
    kd = (kg * jnp.exp(-b)).astype(BF16)
    kl_t = (kg * jnp.exp(rest)).T
    g_t = g.T

    own_tok = ((lax.broadcasted_iota(jnp.int32, (NS * GLA_DK, R), 0) >> 6)
               == (lax.broadcasted_iota(jnp.int32, (NS * GLA_DK, R), 1) >> q_shift))
    own_seq = ((lax.broadcasted_iota(jnp.int32, (R, NS * GLA_DK), 1) >> 6)
               == (lax.broadcasted_iota(jnp.int32, (R, NS * GLA_DK), 0) >> q_shift))
    lane_half = lax.broadcasted_iota(jnp.int32, (R, LANES), 1) >> 6
    ones = jnp.ones((R, GLA_DV), BF16)

    glag = glag_ref[...]
    og = []
    for hd in range(GLA_HEADS):
        ks_ = slice(hd * GLA_DK, (hd + 1) * GLA_DK)
        v = proj[:, C_VG + hd * GLA_DV:C_VG + (hd + 1) * GLA_DV].astype(BF16)
        att = jnp.where(causal, _dot_nt(qb[:, ks_].astype(BF16), kd[:, ks_]), 0.0).astype(BF16)
        o = _dot(att, v)
        s_old = s_ref[:, hd].reshape(NS * GLA_DK, GLA_DV)
        slab = qb[:, (hd // 2) * LANES:(hd // 2 + 1) * LANES]
        dup = jnp.where(lane_half == hd % 2, slab, pltpu.roll(slab, HEAD_DIM, 1))
        q_exp = jnp.where(own_seq, jnp.concatenate([dup] * (NS // 2), axis=1), 0.0).astype(BF16)
        o = o + _dot(q_exp, s_old.astype(BF16))
        o = o * lax.rsqrt(jnp.mean(o * o, axis=-1, keepdims=True) + EPS)
        rg = proj[:, C_RG + hd * GLA_DV:C_RG + (hd + 1) * GLA_DV]
        og.append(o * glag[:, hd * GLA_DV:(hd + 1) * GLA_DV] * _silu(rg))
        kl_exp = jnp.where(own_tok, jnp.concatenate([kl_t[ks_, :]] * NS, axis=0), 0.0).astype(BF16)
        g_exp = jnp.where(own_tok, jnp.concatenate([g_t[ks_, :]] * NS, axis=0), 0.0)
        b_last = _split_dot_rhs(g_exp, ones)
        s_new = jnp.exp(b_last) * s_old + _dot(kl_exp, v)
        so_ref[:, hd] = s_new.reshape(NS, GLA_DK, GLA_DV)

    cos, sa, sb = (rope_ref[:, k * LANES:(k + 1) * LANES] for k in range(3))
    q_rot = _rope(proj[:, C_QS:C_QS + SWA_Q_W], cos, sa, sb) * QK_SCALE
    k_rot = _rope(proj[:, C_KS:C_KS + SWA_KV_W], cos, sa, sb)
    v_new = proj[:, C_VS:C_VS + SWA_KV_W]

    W = kc_ref.shape[1]
    SR = SMP_SUB * Q
    GR = SWA_GROUP * SR
    sub_shift = SR.bit_length() - 1
    w_shift = W.bit_length() - 1
    r_c = lax.broadcasted_iota(jnp.int32, (GR, SMP_SUB * W), 0)
    c_c = lax.broadcasted_iota(jnp.int32, (GR, SMP_SUB * W), 1)
    vis_c = jnp.logical_and(((r_c & (SR - 1)) >> q_shift) == (c_c >> w_shift),
                            (c_c & (W - 1)) > (r_c & (Q - 1)))
    r_n = lax.broadcasted_iota(jnp.int32, (GR, SR), 0)
    c_n = lax.broadcasted_iota(jnp.int32, (GR, SR), 1)
    vis_n = jnp.logical_and(((r_n & (SR - 1)) >> q_shift) == (c_n >> q_shift),
                            (c_n & (Q - 1)) <= (r_n & (Q - 1)))
    grp_row = lax.broadcasted_iota(jnp.int32, (GR, 1), 0) >> sub_shift
    half_sr = lax.broadcasted_iota(jnp.int32, (SR, LANES), 1) >> 6

    o_rows = []
    for sg in range(NS // SMP_SUB):
        rows = slice(sg * SR, (sg + 1) * SR)
        k_c = kc_ref[sg * SMP_SUB:(sg + 1) * SMP_SUB].reshape(SMP_SUB * W, SWA_KV_W).astype(BF16)
        v_c = vc_ref[sg * SMP_SUB:(sg + 1) * SMP_SUB].reshape(SMP_SUB * W, SWA_KV_W).astype(BF16)
        k_n = k_rot[rows, :].astype(BF16)
        v_n = v_new[rows, :].astype(BF16)
        o_kv = []
        for j in range(SWA_KV_HEADS):
            parts = []
            for gq in range(SWA_GROUP):
                hq = j * SWA_GROUP + gq
                slab = q_rot[rows, (hq // 2) * LANES:(hq // 2 + 1) * LANES]
                if hq % 2 != j:
                    slab = pltpu.roll(slab, HEAD_DIM, 1)
                parts.append(jnp.where(half_sr == j, slab, 0.0))
            q8 = jnp.concatenate(parts, axis=0).astype(BF16)
            s_c = jnp.where(vis_c, _dot_nt(q8, k_c), -jnp.inf)
            s_n = jnp.where(vis_n, _dot_nt(q8, k_n), -jnp.inf)
            sink = jnp.zeros((GR, 1), F32)
            for gq in range(SWA_GROUP):
                sink = jnp.where(grp_row == gq, sink_ref[layer, j * SWA_GROUP + gq], sink)
            m = jnp.maximum(jnp.maximum(jnp.max(s_c, axis=-1, keepdims=True),
                                        jnp.max(s_n, axis=-1, keepdims=True)), sink)
            p_c = jnp.exp(s_c - m)
            p_n = jnp.exp(s_n - m)
            den = (jnp.sum(p_c, axis=-1, keepdims=True) + jnp.sum(p_n, axis=-1, keepdims=True)
                   + jnp.exp(sink - m))
            o = _dot((p_c / den).astype(BF16), v_c) + _dot((p_n / den).astype(BF16), v_n)
            o_kv.append(o)
        slabs = []
        for p in range(SWA_HEADS // 2):
            halves = []
            for hq in (2 * p, 2 * p + 1):
                j, gq = hq // SWA_GROUP, hq % SWA_GROUP
                piece = o_kv[j][gq * SR:(gq + 1) * SR, :]
                if hq % 2 != j:
                    piece = pltpu.roll(piece, HEAD_DIM, 1)
                halves.append(piece)
            slabs.append(jnp.where(half_sr == 0, halves[0], halves[1]))
        o_rows.append(jnp.concatenate(slabs, axis=1))
    o_s = _rmsnorm(jnp.concatenate(o_rows, axis=0), swag_ref[...])

    cat = jnp.concatenate(og + [o_s], axis=1).astype(BF16)
    xo_ref[...] = x + _dot(cat, wout_ref[...])

    ko_ref[:, 0:W - Q, :] = kc_ref[:, Q:W, :]
    vo_ref[:, 0:W - Q, :] = vc_ref[:, Q:W, :]
    for s in range(NS):
        ko_ref[s, W - Q:W, :] = k_rot[s * Q:(s + 1) * Q, :]
        vo_ref[s, W - Q:W, :] = v_new[s * Q:(s + 1) * Q, :]


def _split_dot_rhs(g, m_bf16):
    hi = g.astype(BF16)
    lo = (g - hi.astype(F32)).astype(BF16)
    return _dot(hi, m_bf16) + _dot(lo, m_bf16)


def _mix_sample(x2d, state, kc, vc, sinks, mix_w, tabs, layer, prev):
    depth, NB, W = kc.shape[0], kc.shape[1], kc.shape[2]
    Q = x2d.shape[0] // NB
    R = SMP_SEQS * Q
    tab_spec = pl.BlockSpec((R, 3 * LANES), lambda i: (0, 0))
    st_spec = pl.BlockSpec((None, SMP_SEQS, GLA_HEADS, GLA_DK, GLA_DV), lambda i: (layer, i, 0, 0, 0))
    kv_spec = pl.BlockSpec((None, SMP_SEQS, W, SWA_KV_W), lambda i: (layer, i, 0, 0))
    n_prev = 0 if prev is None else len(prev)
    return pl.pallas_call(
        functools.partial(_mix_sample_kernel, layer=layer),
        grid=(NB // SMP_SEQS,),
        in_specs=[
            pl.BlockSpec(memory_space=pltpu.SMEM),
            pl.BlockSpec((R, D_MODEL), lambda i: (i, 0)),
            st_spec, kv_spec, kv_spec,
            *_mixer_weight_specs(layer),
            tab_spec,
            *([pl.BlockSpec(memory_space=pl.ANY)] * n_prev),
        ],
        out_specs=[pl.BlockSpec((R, D_MODEL), lambda i: (i, 0)), st_spec, kv_spec, kv_spec],
        out_shape=[
            jax.ShapeDtypeStruct(x2d.shape, F32),
            jax.ShapeDtypeStruct(state.shape, F32),
            jax.ShapeDtypeStruct(kc.shape, F32),
            jax.ShapeDtypeStruct(vc.shape, F32),
        ],
        input_output_aliases={N_SMP_IN + k: 1 + k for k in range(n_prev)},
        compiler_params=pltpu.CompilerParams(
            dimension_semantics=("arbitrary",), vmem_limit_bytes=VMEM_LIMIT),
        name="mix_sample",
    )(sinks, x2d, state, kc, vc, *mix_w, tabs, *(prev or ()))


def _rope_tables(pos):
    half = ROT_DIM // 2
    inv = jnp.power(jnp.float32(ROPE_THETA), -jnp.arange(half, dtype=jnp.float32) * (2.0 / ROT_DIM))
    ang = pos.astype(jnp.float32)[:, None] * inv[None, :]
    cos, sin = jnp.cos(ang), jnp.sin(ang)
    n = pos.shape[0]
    pad = jnp.zeros((n, HEAD_DIM - ROT_DIM), F32)
    zero = jnp.zeros((n, half), F32)
    c64 = jnp.concatenate([cos, cos, pad + 1.0], axis=1)
    a64 = jnp.concatenate([zero, sin, pad], axis=1)
    b64 = jnp.concatenate([-sin, zero, pad], axis=1)
    rep = LANES // HEAD_DIM
    return jnp.concatenate([jnp.tile(t, (1, rep)) for t in (c64, a64, b64)], axis=1)


def kernel(x_prompt, x_sample, state_gla, cache_swa_k, cache_swa_v, norm_ffn1, w_ffn1_gu, w_ffn1_down,
           norm_mix, w_in, w_gate_up, b_gate, gla_norm, attn_sinks, swa_norm, w_out, norm_ffn2,
           w_ffn2_gu, w_ffn2_down, norm_final):
    B, L, _ = x_prompt.shape
    NB, Q, _ = x_sample.shape
    depth = w_in.shape[0]
    W = cache_swa_k.shape[2]

    tabs_p = _rope_tables(jnp.arange(L))
    tabs_s = _rope_tables(jnp.tile(PAST_LEN + jnp.arange(Q), SMP_SEQS))

    wa2 = jnp.pad(w_gate_up, ((0, 0), (0, LANES - GATE_RANK), (0, 0))).astype(BF16)
    row = lambda p: p.reshape(depth, 1, -1)
    n1, n2 = row(norm_ffn1), row(norm_ffn2)
    gu_rows = D_MODEL // (B * L // _ffn_tile(B * L))
    dn_rows = D_FF // 11

    def ffn_sides(w_gu, w_down, layer):
        return (SideCast(w_gu, layer, gu_rows), SideCast(w_down, layer, dn_rows))

    xp = x_prompt.reshape(B * L, D_MODEL)
    xs = x_sample.reshape(NB * Q, D_MODEL)
    kc = cache_swa_k.reshape(depth, NB, W, SWA_KV_W)
    vc = cache_swa_v.reshape(depth, NB, W, SWA_KV_W)
    out_p = out_s = None
    gu1, dn1 = _cast_layers((SideCast(w_ffn1_gu, 0, D_MODEL // 8), SideCast(w_ffn1_down, 0, D_FF // 4)))
    for l in range(depth):
        nf = norm_final if l == depth - 1 else None
        xp, win, wout = _ffn(xp, n1, gu1, dn1, l, sides=(SideCast(w_in, l, gu_rows, True),
                                                        SideCast(w_out, l, gu_rows)))
        xs = _ffn(xs, n1, gu1, dn1, l)
        mix_w = (row(norm_mix), win, wa2, row(b_gate), row(gla_norm), row(swa_norm), wout)
        xp3, *out_p = _mix_prompt(xp.reshape(B, L, D_MODEL), attn_sinks, mix_w, tabs_p, l, depth, out_p,
                                  sides=ffn_sides(w_ffn2_gu, w_ffn2_down, l))
        gu2, dn2 = out_p[3:]
        out_p = out_p[:3]
        xs, *out_s = _mix_sample(xs, state_gla, kc, vc, attn_sinks, mix_w, tabs_s, l, out_s)
        xs = _ffn(xs, n2, gu2, dn2, l, nf)
        xp3 = xp3.reshape(B * L, D_MODEL)
        if l + 1 < depth:
            xp, gu1, dn1 = _ffn(xp3, n2, gu2, dn2, l, nf, sides=ffn_sides(w_ffn1_gu, w_ffn1_down, l + 1))
        else:
            xp = _ffn(xp3, n2, gu2, dn2, l, nf)

    gla_p, k_p, v_p = out_p
    gla_s, k_s, v_s = out_s
    kv5 = lambda t: t.reshape(*t.shape[:3], SWA_KV_HEADS, HEAD_DIM)
    return (xp.reshape(B, L, D_MODEL), xs.reshape(NB, Q, D_MODEL), gla_p, kv5(k_p), kv5(v_p),
            gla_s, kv5(k_s), kv5(v_s))
```
